```python
import jax, jax.numpy as jnp
from jax import lax
import numpy as np

D_MODEL = 1024
BATCH = 16
SEQ = 2048
DEPTH = 1

HEAD_DIM = 64
NA_HEADS = 8
NA_WIDTH = NA_HEADS * HEAD_DIM
GRID_W = 64
NA_ROWS_MAX = 8
NA_COLS = 16
SW_HEADS = 8
SW_KV_HEADS = 2
SW_GROUP = SW_HEADS // SW_KV_HEADS
SW_WIDTH = SW_HEADS * HEAD_DIM
SW_KV_WIDTH = SW_KV_HEADS * HEAD_DIM
SW_WINDOW = 128
SW_BLOCK = 128
MIX_WIDTH = NA_WIDTH + SW_WIDTH
IN_WIDTH = 3 * NA_WIDTH + SW_WIDTH + 2 * SW_KV_WIDTH
D_FF = 2816
CONV_W = 3
ROPE_THETA = 10000.0
EPS = 1e-6
NEG = -1e30

kernel_name = "hybrid_natten_swa_convffn_block"


def rmsnorm(x, g):
    xf = x.astype(jnp.float32)
    y = xf * lax.rsqrt(jnp.mean(xf * xf, axis=-1, keepdims=True) + EPS)
    return (y * g.astype(jnp.float32)).astype(x.dtype)


def rope(t, pos):
    half = HEAD_DIM // 2
    inv = ROPE_THETA ** (-jnp.arange(half, dtype=jnp.float32) / half)
    ang = pos.astype(jnp.float32)[:, None] * inv[None, :]
    cos = jnp.cos(ang)[None, :, None, :].astype(t.dtype)
    sin = jnp.sin(ang)[None, :, None, :].astype(t.dtype)
    t1, t2 = t[..., :half], t[..., half:]
    return jnp.concatenate([t1 * cos - t2 * sin, t2 * cos + t1 * sin], axis=-1)


def neighbourhood_attention(q, k, v, rpb):
    B, S = q.shape[0], q.shape[1]
    rows = S // GRID_W
    wr = min(NA_ROWS_MAX, rows)

    def grid(t):
        return t.reshape(B, rows, GRID_W, NA_HEADS, HEAD_DIM).transpose(0, 3, 1, 2, 4)

    r = jnp.arange(rows)
    rs = jnp.clip(r - wr // 2, 0, rows - wr)
    row_idx = rs[:, None] + jnp.arange(wr)[None, :]
    kb = jnp.take(grid(k), row_idx, axis=2).reshape(B, NA_HEADS, rows, wr * GRID_W, HEAD_DIM)
    vb = jnp.take(grid(v), row_idx, axis=2).reshape(B, NA_HEADS, rows, wr * GRID_W, HEAD_DIM)

    col = jnp.arange(GRID_W)
    cs = jnp.clip(col - NA_COLS // 2, 0, GRID_W - NA_COLS)
    col_ok = (col[None, :] >= cs[:, None]) & (col[None, :] < cs[:, None] + NA_COLS)
    dc = jnp.clip(col[None, :] - col[:, None] + NA_COLS - 1, 0, 2 * NA_COLS - 2)
    dr = row_idx - r[:, None] + NA_ROWS_MAX - 1
    bias = rpb[:, dr[:, None, :, None], dc[None, :, None, :]]
    bias = jnp.where(col_ok[None, None, :, None, :], bias.astype(jnp.float32), NEG)
    bias = bias.reshape(NA_HEADS, rows, GRID_W, wr * GRID_W)

    s = jnp.einsum('bhrqd,bhrkd->bhrqk', grid(q), kb,
                   preferred_element_type=jnp.float32) * (HEAD_DIM ** -0.5) + bias[None]
    p = jax.nn.softmax(s, axis=-1).astype(v.dtype)
    o = jnp.einsum('bhrqk,bhrkd->bhrqd', p, vb)
    return o.transpose(0, 2, 3, 1, 4).reshape(B, S, NA_WIDTH)


def window_sink_attention(q, k, v, sink):
    B, S = q.shape[0], q.shape[1]
    nb = S // SW_BLOCK
    qb = q.reshape(B, nb, SW_BLOCK, SW_KV_HEADS, SW_GROUP, HEAD_DIM)

    def band(t):
        tp = jnp.pad(t, ((0, 0), (SW_BLOCK, SW_BLOCK), (0, 0), (0, 0)))
        tp = tp.reshape(B, nb + 2, SW_BLOCK, SW_KV_HEADS, HEAD_DIM)
        return jnp.concatenate([tp[:, :-2], tp[:, 1:-1], tp[:, 2:]], axis=2)

    kw, vw = band(k), band(v)
    blk = jnp.arange(nb)[:, None] * SW_BLOCK
    qpos = blk + jnp.arange(SW_BLOCK)[None, :]
    kpos = blk - SW_BLOCK + jnp.arange(3 * SW_BLOCK)[None, :]
    ok = (jnp.abs(qpos[:, :, None] - kpos[:, None, :]) <= SW_WINDOW) \
        & ((kpos >= 0) & (kpos < S))[:, None, :]

    s = jnp.einsum('bnqhgd,bnkhd->bhgnqk', qb, kw,
                   preferred_element_type=jnp.float32) * (HEAD_DIM ** -0.5)
    s = jnp.where(ok[None, None, None], s, NEG)
    sk = sink.astype(jnp.float32).reshape(1, SW_KV_HEADS, SW_GROUP, 1, 1)
    m = jnp.maximum(jnp.max(s, axis=-1), sk)
    p = jnp.exp(s - m[..., None])
    den = jnp.sum(p, axis=-1) + jnp.exp(sk - m)
    p = (p / den[..., None]).astype(v.dtype)
    o = jnp.einsum('bhgnqk,bnkhd->bnqhgd', p, vw)
    return o.reshape(B, S, SW_WIDTH)


def setup_inputs(seed: int = 0) -> dict:
    key = jax.random.key(seed)
    ks = jax.random.split(key, 20)
    L, D = DEPTH, D_MODEL

    def nrm(k, shape, scale):
        return jax.random.normal(k, shape, jnp.float32) * scale

    return {
        "x": nrm(ks[0], (BATCH, SEQ, D), 1.0),
        "c": nrm(ks[1], (BATCH, D), 1.0),
        "w_ada": nrm(ks[2], (L, D, 6 * D), 0.5 * D ** -0.5),
        "b_ada": nrm(ks[3], (L, 6 * D), 0.02),
        "g_attn": 1.0 + nrm(ks[4], (L, D), 0.02),
        "w_in": nrm(ks[5], (L, D, IN_WIDTH), D ** -0.5),
        "na_rpb": nrm(ks[6], (L, NA_HEADS, 2 * NA_ROWS_MAX - 1, 2 * NA_COLS - 1), 0.1),
        "sw_sink": nrm(ks[7], (L, SW_HEADS), 0.5),
        "g_na_out": 1.0 + nrm(ks[8], (L, NA_WIDTH), 0.02),
        "g_sw_out": 1.0 + nrm(ks[9], (L, SW_WIDTH), 0.02),
        "w_out": nrm(ks[10], (L, MIX_WIDTH, D), MIX_WIDTH ** -0.5),
        "g_ffn": 1.0 + nrm(ks[11], (L, D), 0.02),
        "w_up": nrm(ks[12], (L, D, 2 * D_FF), D ** -0.5),
        "conv_w": nrm(ks[13], (L, CONV_W, D_FF), CONV_W ** -0.5),
        "conv_b": nrm(ks[14], (L, D_FF), 0.02),
        "w_down": nrm(ks[15], (L, D_FF, D), D_FF ** -0.5),
        "g_final": 1.0 + nrm(ks[16], (D,), 0.02),
    }


def reference(x, c, w_ada, b_ada, g_attn, w_in, na_rpb, sw_sink, g_na_out, g_sw_out,
              w_out, g_ffn, w_up, conv_w, conv_b, w_down, g_final):
    S = x.shape[1]
    pos = jnp.arange(S)
    splits = [NA_WIDTH, 2 * NA_WIDTH, 3 * NA_WIDTH, 3 * NA_WIDTH + SW_WIDTH,
              3 * NA_WIDTH + SW_WIDTH + SW_KV_WIDTH]
    for l in range(DEPTH):
        mod = jax.nn.silu(c) @ w_ada[l] + b_ada[l]
        shift_a, scale_a, gate_a, shift_f, scale_f, gate_f = [
            m[:, None, :] for m in jnp.split(mod, 6, axis=-1)]

        h = rmsnorm(x, g_attn[l]) * (1.0 + scale_a) + shift_a
        proj = h @ w_in[l]
        qa, ka, va, qb, kb, vb = jnp.split(proj, splits, axis=-1)
        Bn = x.shape[0]
        qa = qa.reshape(Bn, S, NA_HEADS, HEAD_DIM)
        ka = ka.reshape(Bn, S, NA_HEADS, HEAD_DIM)
        va = va.reshape(Bn, S, NA_HEADS, HEAD_DIM)
        qb = rope(qb.reshape(Bn, S, SW_HEADS, HEAD_DIM), pos)
        kb = rope(kb.reshape(Bn, S, SW_KV_HEADS, HEAD_DIM), pos)
        vb = vb.reshape(Bn, S, SW_KV_HEADS, HEAD_DIM)

        o_a = rmsnorm(neighbourhood_attention(qa, ka, va, na_rpb[l]), g_na_out[l])
        o_b = rmsnorm(window_sink_attention(qb, kb, vb, sw_sink[l]), g_sw_out[l])
        mix = jnp.concatenate([o_a, o_b], axis=-1) @ w_out[l]
        x = x + gate_a * mix

        h = rmsnorm(x, g_ffn[l]) * (1.0 + scale_f) + shift_f
        val, gt = jnp.split(h @ w_up[l], 2, axis=-1)
        gp = jnp.pad(gt, ((0, 0), (1, 1), (0, 0)))
        cw = conv_w[l]
        gc = gp[:, :-2] * cw[0] + gp[:, 1:-1] * cw[1] + gp[:, 2:] * cw[2] + conv_b[l]
        x = x + gate_f * ((jax.nn.silu(gc) * val) @ w_down[l])
    return rmsnorm(x, g_final)
```

```python
import functools

import jax
import jax.numpy as jnp
import numpy as np
from jax import lax
from jax.experimental import pallas as pl
from jax.experimental.pallas import tpu as pltpu

F32 = jnp.float32
BF16 = jnp.bfloat16

HEAD_DIM = 64
LANES = 128
NA_HEADS = 8
NA_WIDTH = NA_HEADS * HEAD_DIM
GRID_W = 64
NA_ROWS_WIN = 8
NA_COLS = 16
SW_HEADS = 8
SW_KV_HEADS = 2
SW_GROUP = SW_HEADS // SW_KV_HEADS
SW_WIDTH = SW_HEADS * HEAD_DIM
SW_KV_WIDTH = SW_KV_HEADS * HEAD_DIM
SW_BLOCK = 128
CONV_W = 3
ROPE_THETA = 10000.0
EPS = 1e-6
NEG = -1e30
QK_SCALE = HEAD_DIM ** -0.5

VMEM_LIMIT = 56 * 1024 * 1024

IN_TM = 512
OUT_TM = 512
FF_CHUNK = 256


def _rms(x):
    return x * lax.rsqrt(jnp.mean(x * x, axis=-1, keepdims=True) + EPS)


def _nt_dot(a, b):
    return lax.dot_general(a, b, (((1,), (1,)), ((), ())), preferred_element_type=F32)


def _lo_mask(rows):
    return lax.broadcasted_iota(jnp.int32, (rows, LANES), 1) < HEAD_DIM


def _ada_kernel(c_ref, w_ref, b_ref, o_ref):
    c = c_ref[...]
    sc = c * jax.nn.sigmoid(c)
    o_ref[...] = jnp.dot(sc, w_ref[...], preferred_element_type=F32,
                         precision=lax.Precision.HIGHEST) + b_ref[...]


def _ada(c, w, b):
    bsz, d = c.shape
    n = w.shape[1]
    tn = 1024
    return pl.pallas_call(
        _ada_kernel,
        grid=(n // tn,),
        in_specs=[pl.BlockSpec((bsz, d), lambda j: (0, 0)),
                  pl.BlockSpec((d, tn), lambda j: (0, j)),
                  pl.BlockSpec((1, tn), lambda j: (0, j))],
        out_specs=pl.BlockSpec((bsz, tn), lambda j: (0, j)),
        out_shape=jax.ShapeDtypeStruct((bsz, n), F32),
        compiler_params=pltpu.CompilerParams(dimension_semantics=("arbitrary",),
                                             vmem_limit_bytes=VMEM_LIMIT),
        name="ada",
    )(c, w, b.reshape(1, n))


MOD_SHIFT_A, MOD_SCALE_A, MOD_GATE_A, MOD_SHIFT_F, MOD_SCALE_F, MOD_GATE_F = range(6)


def _mod_spec(which, d):
    return pl.BlockSpec((1, 1, 1, d), lambda b, t: (b, which, 0, 0))


def _rope(t, cos, sin_signed, first_half):
    rot = jnp.where(first_half, pltpu.roll(t, LANES - HEAD_DIM // 2, 1), pltpu.roll(t, HEAD_DIM // 2, 1))
    return t * cos + rot * sin_signed


def _in_proj_kernel(x_ref, g_ref, scale_ref, shift_ref, w_ref, cos_ref, sin_ref,
                    qa_ref, ka_ref, va_ref, qb_ref, kvb_ref):
    x = x_ref[0]
    h = ((_rms(x) * g_ref[...]) * (1.0 + scale_ref[0, 0]) + shift_ref[0, 0]).astype(BF16)
    cos = cos_ref[...]
    sin = sin_ref[...]
    tm = x.shape[0]
    first_half = (lax.broadcasted_iota(jnp.int32, (tm, LANES), 1) % HEAD_DIM) < HEAD_DIM // 2

    def proj(lo, hi):
        return jnp.dot(h, w_ref[:, lo:hi], preferred_element_type=F32)

    qa_ref[0] = (proj(0, NA_WIDTH) * QK_SCALE).astype(BF16)
    ka_ref[0] = proj(NA_WIDTH, 2 * NA_WIDTH).astype(BF16)
    va_ref[0] = proj(2 * NA_WIDTH, 3 * NA_WIDTH).astype(BF16)
    base = 3 * NA_WIDTH
    qb = proj(base, base + SW_WIDTH)
    for j in range(SW_WIDTH // LANES):
        blk = _rope(qb[:, j * LANES:(j + 1) * LANES], cos, sin, first_half) * QK_SCALE
        qb_ref[0, :, j * LANES:(j + 1) * LANES] = blk.astype(BF16)
    kv = proj(base + SW_WIDTH, base + SW_WIDTH + 2 * SW_KV_WIDTH)
    kvb_ref[0, :, :LANES] = _rope(kv[:, :LANES], cos, sin, first_half).astype(BF16)
    kvb_ref[0, :, LANES:] = kv[:, LANES:].astype(BF16)


def _in_proj(x, g, mod, w, cos, sin):
    bsz, s, d = x.shape
    n = w.shape[1]
    tm = IN_TM
    row = lambda b, t: (b, t, 0)
    outs = [jax.ShapeDtypeStruct((bsz, s, NA_WIDTH), BF16)] * 3 + [
        jax.ShapeDtypeStruct((bsz, s, SW_WIDTH), BF16),
        jax.ShapeDtypeStruct((bsz, s, 2 * SW_KV_WIDTH), BF16)]
    return pl.pallas_call(
        _in_proj_kernel,
        grid=(bsz, s // tm),
        in_specs=[pl.BlockSpec((1, tm, d), row),
                  pl.BlockSpec((1, d), lambda b, t: (0, 0)),
                  _mod_spec(MOD_SCALE_A, d),
                  _mod_spec(MOD_SHIFT_A, d),
                  pl.BlockSpec((d, n), lambda b, t: (0, 0)),
                  pl.BlockSpec((tm, LANES), lambda b, t: (t, 0)),
                  pl.BlockSpec((tm, LANES), lambda b, t: (t, 0))],
        out_specs=[pl.BlockSpec((1, tm, NA_WIDTH), row)] * 3 + [
            pl.BlockSpec((1, tm, SW_WIDTH), row),
            pl.BlockSpec((1, tm, 2 * SW_KV_WIDTH), row)],
        out_shape=outs,
        compiler_params=pltpu.CompilerParams(dimension_semantics=("arbitrary", "arbitrary"),
                                             vmem_limit_bytes=VMEM_LIMIT),
        name="in_proj",
    )(x, g, mod, mod, w, cos, sin)


def _na_kernel(q_ref, k_ref, v_ref, bias_ref, o_ref, *, rows):
    lo = _lo_mask(GRID_W)
    win = NA_ROWS_WIN * GRID_W

    def body(r, carry):
        rs = jnp.clip(r - NA_ROWS_WIN // 2, 0, rows - NA_ROWS_WIN)
        q = q_ref[0, pl.ds(pl.multiple_of(r * GRID_W, GRID_W), GRID_W), :]
        k0 = pl.multiple_of(rs * GRID_W, GRID_W)
        kw = k_ref[0, pl.ds(k0, win), :]
        vw = v_ref[0, pl.ds(k0, win), :]
        zero = jnp.zeros_like(q)
        qs = jnp.concatenate([jnp.where(lo, q, zero), jnp.where(lo, zero, q)], axis=0)
        s = _nt_dot(qs, kw) + bias_ref[0, r - rs]
        m = jnp.max(s, axis=-1, keepdims=True)
        p = jnp.exp(s - m)
        den = jnp.sum(p, axis=-1, keepdims=True)
        o = jnp.dot(p.astype(BF16), vw, preferred_element_type=F32) * (1.0 / den)
        out = jnp.where(lo, o[:GRID_W], o[GRID_W:])
        o_ref[0, pl.ds(pl.multiple_of(r * GRID_W, GRID_W), GRID_W), :] = out.astype(o_ref.dtype)
        return carry

    lax.fori_loop(0, rows, body, 0, unroll=2)


def _na_attention(q, k, v, bias):
    bsz, s, _ = q.shape
    rows = s // GRID_W
    pairs = NA_WIDTH // LANES
    blk = pl.BlockSpec((1, s, LANES), lambda hp, b: (b, 0, hp))
    return pl.pallas_call(
        functools.partial(_na_kernel, rows=rows),
        grid=(pairs, bsz),
        in_specs=[blk, blk, blk,
                  pl.BlockSpec((1,) + bias.shape[1:], lambda hp, b: (hp, 0, 0, 0))],
        out_specs=blk,
        out_shape=jax.ShapeDtypeStruct((bsz, s, NA_WIDTH), BF16),
        compiler_params=pltpu.CompilerParams(dimension_semantics=("arbitrary", "arbitrary"),
                                             vmem_limit_bytes=VMEM_LIMIT),
        name="na_attn",
    )(q, k, v, bias)


def _na_bias_table(rpb, rows):
    assert rows >= 2 * NA_ROWS_WIN
    col = np.arange(GRID_W)
    cs = np.clip(col - NA_COLS // 2, 0, GRID_W - NA_COLS)
    col_ok = (col[None, :] >= cs[:, None]) & (col[None, :] < cs[:, None] + NA_COLS)
    dc = np.clip(col[None, :] - col[:, None] + NA_COLS - 1, 0, 2 * NA_COLS - 2)
    cls = np.arange(NA_ROWS_WIN)
    dr = np.arange(NA_ROWS_WIN)[None, :] - cls[:, None] + NA_ROWS_WIN - 1
    b = rpb.astype(F32)[:, dr[:, None, :, None], dc[None, :, None, :]]
    b = jnp.where(col_ok[None, None, :, None, :], b, NEG)
    b = b.reshape(NA_HEADS // 2, 2, NA_ROWS_WIN, GRID_W, NA_ROWS_WIN * GRID_W)
    return b.transpose(0, 2, 1, 3, 4).reshape(NA_HEADS // 2, NA_ROWS_WIN, 2 * GRID_W, NA_ROWS_WIN * GRID_W)


def _sw_kernel(sink_ref, q_ref, k_ref, v_ref, bias_ref, o_ref, *, nblocks):
    n = pl.program_id(1)
    lo = _lo_mask(SW_BLOCK)
    start = pl.multiple_of(jnp.clip(n - 1, 0, nblocks - 3) * SW_BLOCK, SW_BLOCK)
    variant = jnp.where(n == 0, 0, jnp.where(n == nblocks - 1, 2, 1))
    kw = k_ref[0, pl.ds(start, 3 * SW_BLOCK), :]
    vw = v_ref[0, pl.ds(start, 3 * SW_BLOCK), :]
    q = q_ref[0]
    zero = jnp.zeros((SW_BLOCK, LANES), q.dtype)
    parts = []
    for j in range(SW_WIDTH // LANES):
        qj = q[:, j * LANES:(j + 1) * LANES]
        parts += [jnp.where(lo, qj, zero), jnp.where(lo, zero, qj)]
    s = _nt_dot(jnp.concatenate(parts, axis=0), kw)
    bias = bias_ref[variant]
    ps, invs = [], []
    for g in range(SW_HEADS):
        sg = s[g * SW_BLOCK:(g + 1) * SW_BLOCK] + bias
        sk = sink_ref[g]
        m = jnp.maximum(jnp.max(sg, axis=-1, keepdims=True), sk)
        p = jnp.exp(sg - m)
        den = jnp.sum(p, axis=-1, keepdims=True) + jnp.exp(sk - m)
        ps.append(p.astype(BF16))
        invs.append(1.0 / den)
    o = jnp.dot(jnp.concatenate(ps, axis=0), vw, preferred_element_type=F32)
    for j in range(SW_WIDTH // LANES):
        olo = o[(2 * j) * SW_BLOCK:(2 * j + 1) * SW_BLOCK] * invs[2 * j]
        ohi = o[(2 * j + 1) * SW_BLOCK:(2 * j + 2) * SW_BLOCK] * invs[2 * j + 1]
        o_ref[0, :, j * LANES:(j + 1) * LANES] = jnp.where(lo, olo, ohi).astype(o_ref.dtype)


def _sw_attention(sink, q, kv, bias):
    bsz, s, _ = q.shape
    nblocks = s // SW_BLOCK
    assert nblocks >= 3
    grid_spec = pltpu.PrefetchScalarGridSpec(
        num_scalar_prefetch=1,
        grid=(bsz, nblocks),
        in_specs=[pl.BlockSpec((1, SW_BLOCK, SW_WIDTH), lambda b, n, sk: (b, n, 0)),
                  pl.BlockSpec((1, s, LANES), lambda b, n, sk: (b, 0, 0)),
                  pl.BlockSpec((1, s, LANES), lambda b, n, sk: (b, 0, 1)),
                  pl.BlockSpec(bias.shape, lambda b, n, sk: (0, 0, 0))],
        out_specs=pl.BlockSpec((1, SW_BLOCK, SW_WIDTH), lambda b, n, sk: (b, n, 0)),
    )
    return pl.pallas_call(
        functools.partial(_sw_kernel, nblocks=nblocks),
        grid_spec=grid_spec,
        out_shape=jax.ShapeDtypeStruct((bsz, s, SW_WIDTH), BF16),
        compiler_params=pltpu.CompilerParams(dimension_semantics=("arbitrary", "arbitrary"),
                                             vmem_limit_bytes=VMEM_LIMIT),
        name="sw_attn",
    )(sink, q, kv, kv, bias)


def _sw_bias_table():
    i = np.arange(SW_BLOCK)[:, None]
    j = np.arange(3 * SW_BLOCK)[None, :]
    ok = np.stack([np.abs(i - j) <= SW_BLOCK,
                   np.abs(i + SW_BLOCK - j) <= SW_BLOCK,
                   np.abs(i + 2 * SW_BLOCK - j) <= SW_BLOCK])
    return jnp.asarray(np.where(ok, 0.0, NEG), F32)


def _out_proj_kernel(x_ref, oa_ref, ob_ref, gna_ref, gsw_ref, w_ref, gate_ref, o_ref):
    oa = (_rms(oa_ref[0].astype(F32)) * gna_ref[...]).astype(BF16)
    ob = (_rms(ob_ref[0].astype(F32)) * gsw_ref[...]).astype(BF16)
    mix = (jnp.dot(oa, w_ref[:NA_WIDTH], preferred_element_type=F32)
           + jnp.dot(ob, w_ref[NA_WIDTH:], preferred_element_type=F32))
    o_ref[0] = x_ref[0] + gate_ref[0, 0] * mix


def _out_proj(x, oa, ob, gna, gsw, w, mod):
    bsz, s, d = x.shape
    tm = OUT_TM
    row = lambda b, t: (b, t, 0)
    return pl.pallas_call(
        _out_proj_kernel,
        grid=(bsz, s // tm),
        in_specs=[pl.BlockSpec((1, tm, d), row),
                  pl.BlockSpec((1, tm, NA_WIDTH), row),
                  pl.BlockSpec((1, tm, SW_WIDTH), row),
                  pl.BlockSpec((1, NA_WIDTH), lambda b, t: (0, 0)),
                  pl.BlockSpec((1, SW_WIDTH), lambda b, t: (0, 0)),
                  pl.BlockSpec(w.shape, lambda b, t: (0, 0)),
                  _mod_spec(MOD_GATE_A, d)],
        out_specs=pl.BlockSpec((1, tm, d), row),
        out_shape=jax.ShapeDtypeStruct((bsz, s, d), F32),
        compiler_params=pltpu.CompilerParams(dimension_semantics=("arbitrary", "arbitrary"),
                                             vmem_limit_bytes=VMEM_LIMIT),
        name="out_proj",
    )(x, oa, ob, gna, gsw, w, mod)


def _ffn_kernel(x_ref, g_ref, scale_ref, shift_ref, gate_ref, wup_ref, cw_ref, cb_ref, wdn_ref, gfin_ref,
                o_ref, h_ref, *, row_chunk):
    f = pl.program_id(1)
    s = x_ref.shape[1]
    nchunks = s // row_chunk

    @pl.when(f == 0)
    def _():
        gm = g_ref[...]
        sc = 1.0 + scale_ref[0, 0]
        sh = shift_ref[0, 0]

        def body(i, carry):
            r0 = pl.multiple_of(i * row_chunk, row_chunk)
            xr = x_ref[0, pl.ds(r0, row_chunk), :]
            h_ref[pl.ds(r0, row_chunk), :] = ((_rms(xr) * gm) * sc + sh).astype(BF16)
            return carry

        lax.fori_loop(0, nchunks, body, 0)

    up = jnp.dot(h_ref[...], wup_ref[0], preferred_element_type=F32)
    val = up[:, :FF_CHUNK]
    gt = up[:, FF_CHUNK:]
    ridx = lax.broadcasted_iota(jnp.int32, gt.shape, 0)
    prev = jnp.where(ridx == 0, 0.0, pltpu.roll(gt, 1, 0))
    nxt = jnp.where(ridx == s - 1, 0.0, pltpu.roll(gt, s - 1, 0))
    cw = cw_ref[0]
    gc = prev * cw[0:1] + gt * cw[1:2] + nxt * cw[2:3] + cb_ref[0]
    act = (gc * jax.nn.sigmoid(gc) * val).astype(BF16)
    part = jnp.dot(act, wdn_ref[...], preferred_element_type=F32)

    @pl.when(f == 0)
    def _():
        o_ref[0] = part

    @pl.when(f > 0)
    def _():
        o_ref[0] += part

    @pl.when(f == pl.num_programs(1) - 1)
    def _():
        gate = gate_ref[0, 0]
        gfin = gfin_ref[...]

        def body(i, carry):
            r0 = pl.multiple_of(i * row_chunk, row_chunk)
            y = x_ref[0, pl.ds(r0, row_chunk), :] + gate * o_ref[0, pl.ds(r0, row_chunk), :]
            o_ref[0, pl.ds(r0, row_chunk), :] = _rms(y) * gfin
            return carry

        lax.fori_loop(0, nchunks, body, 0)


def _ffn(x1, g, mod, wup, cw, cb, wdn, gfin):
    bsz, s, d = x1.shape
    nf = wup.shape[0]
    return pl.pallas_call(
        functools.partial(_ffn_kernel, row_chunk=256),
        grid=(bsz, nf),
        in_specs=[pl.BlockSpec((1, s, d), lambda b, f: (b, 0, 0), pipeline_mode=pl.Buffered(1)),
                  pl.BlockSpec((1, d), lambda b, f: (0, 0)),
                  _mod_spec(MOD_SCALE_F, d),
                  _mod_spec(MOD_SHIFT_F, d),
                  _mod_spec(MOD_GATE_F, d),
                  pl.BlockSpec((1, d, 2 * FF_CHUNK), lambda b, f: (f, 0, 0)),
                  pl.BlockSpec((1, CONV_W, FF_CHUNK), lambda b, f: (f, 0, 0)),
                  pl.BlockSpec((1, 1, FF_CHUNK), lambda b, f: (f, 0, 0)),
                  pl.BlockSpec((FF_CHUNK, d), lambda b, f: (f, 0)),
                  pl.BlockSpec((1, d), lambda b, f: (0, 0))],
        out_specs=pl.BlockSpec((1, s, d), lambda b, f: (b, 0, 0)),
        out_shape=jax.ShapeDtypeStruct((bsz, s, d), F32),
        scratch_shapes=[pltpu.VMEM((s, d), BF16)],
        compiler_params=pltpu.CompilerParams(dimension_semantics=("arbitrary", "arbitrary"),
                                             vmem_limit_bytes=VMEM_LIMIT),
        name="ffn",
    )(x1, g, mod, mod, mod, wup, cw, cb, wdn, gfin)


def _rope_tables(s):
    half = HEAD_DIM // 2
    inv = ROPE_THETA ** (-jnp.arange(half, dtype=F32) / half)
    ang = jnp.arange(s).astype(F32)[:, None] * inv[None, :]
    cos = jnp.cos(ang)
    sin = jnp.sin(ang)
    reps = LANES // HEAD_DIM
    return (jnp.tile(jnp.concatenate([cos, cos], axis=-1), (1, reps)),
            jnp.tile(jnp.concatenate([-sin, sin], axis=-1), (1, reps)))


_SW_HEAD_ORDER = np.arange(SW_HEADS).reshape(SW_KV_HEADS, SW_GROUP).T.reshape(-1)
_SW_COL_ORDER = (_SW_HEAD_ORDER[:, None] * HEAD_DIM + np.arange(HEAD_DIM)[None, :]).reshape(-1)


def kernel(x, c, w_ada, b_ada, g_attn, w_in, na_rpb, sw_sink, g_na_out, g_sw_out, w_out, g_ffn, w_up,
           conv_w, conv_b, w_down, g_final):
    bsz, s, d = x.shape
    depth = w_ada.shape[0]
    d_ff = w_down.shape[1]
    nf = d_ff // FF_CHUNK
    assert depth == 1, "the final rmsnorm is fused into the (only) layer's ffn call"
    assert d_ff % FF_CHUNK == 0 and s % GRID_W == 0 and s % SW_BLOCK == 0
    cos, sin = _rope_tables(s)
    sw_bias = _sw_bias_table()
    qb0 = 3 * NA_WIDTH
    for l in range(depth):
        mod = _ada(c, w_ada[l], b_ada[l]).reshape(bsz, 6, 1, d)

        wi = w_in[l]
        wi = jnp.concatenate([wi[:, :qb0], wi[:, qb0:qb0 + SW_WIDTH][:, _SW_COL_ORDER],
                              wi[:, qb0 + SW_WIDTH:]], axis=1).astype(BF16)
        qa, ka, va, qb, kvb = _in_proj(x, g_attn[l].reshape(1, d), mod, wi, cos, sin)

        o_a = _na_attention(qa, ka, va, _na_bias_table(na_rpb[l], s // GRID_W))
        o_b = _sw_attention(sw_sink[l][_SW_HEAD_ORDER].astype(F32), qb, kvb, sw_bias)

        wo = w_out[l]
        wo = jnp.concatenate([wo[:NA_WIDTH], wo[NA_WIDTH:][_SW_COL_ORDER]], axis=0).astype(BF16)
        x = _out_proj(x, o_a, o_b, g_na_out[l].reshape(1, -1), g_sw_out[l][_SW_COL_ORDER].reshape(1, -1),
                      wo, mod)

        wu = w_up[l]
        wu = jnp.concatenate([wu[:, :d_ff].reshape(d, nf, FF_CHUNK), wu[:, d_ff:].reshape(d, nf, FF_CHUNK)],
                             axis=-1).transpose(1, 0, 2).astype(BF16)
        cw = conv_w[l].reshape(CONV_W, nf, FF_CHUNK).transpose(1, 0, 2)
        cb = conv_b[l].reshape(nf, 1, FF_CHUNK)
        x = _ffn(x, g_ffn[l].reshape(1, d), mod, wu, cw, cb, w_down[l].astype(BF16), g_final.reshape(1, d))
    return x
```

```python
import functools

import jax
import jax.numpy as jnp
import numpy as np
from jax import lax
from jax.experimental import pallas as pl
from jax.experimental.pallas import tpu as pltpu

F32 = jnp.float32
BF16 = jnp.bfloat16

HEAD_DIM = 64
LANES = 128
NA_HEADS = 8
NA_WIDTH = NA_HEADS * HEAD_DIM
GRID_W = 64
NA_ROWS_WIN = 8
NA_COLS = 16
SW_HEADS = 8
SW_KV_HEADS = 2
SW_GROUP = SW_HEADS // SW_KV_HEADS
SW_WIDTH = SW_HEADS * HEAD_DIM
SW_KV_WIDTH = SW_KV_HEADS * HEAD_DIM
SW_BLOCK = 128
CONV_W = 3
ROPE_THETA = 10000.0
EPS = 1e-6
NEG = -1e30
QK_SCALE = HEAD_DIM ** -0.5

VMEM_LIMIT = 56 * 1024 * 1024

IN_TM = 512
OUT_TM = 512
FF_CHUNK = 256
FF_ROWS = 512
NA_GROUP = 8


def _rms(x):
    return x * lax.rsqrt(jnp.mean(x * x, axis=-1, keepdims=True) + EPS)


def _nt_dot(a, b):
    return lax.dot_general(a, b, (((1,), (1,)), ((), ())), preferred_element_type=F32)


def _lo_mask(rows):
    return lax.broadcasted_iota(jnp.int32, (rows, LANES), 1) < HEAD_DIM


def _ada_kernel(c_ref, w_ref, b_ref, o_ref):
    c = c_ref[...]
    sc = c * jax.nn.sigmoid(c)
    o_ref[...] = jnp.dot(sc, w_ref[...], preferred_element_type=F32,
                         precision=lax.Precision.HIGHEST) + b_ref[...]


def _ada(c, w, b):
    bsz, d = c.shape
    n = w.shape[1]
    tn = 1024
    return pl.pallas_call(
        _ada_kernel,
        grid=(n // tn,),
        in_specs=[pl.BlockSpec((bsz, d), lambda j: (0, 0)),
                  pl.BlockSpec((d, tn), lambda j: (0, j)),
                  pl.BlockSpec((1, tn), lambda j: (0, j))],
        out_specs=pl.BlockSpec((bsz, tn), lambda j: (0, j)),
        out_shape=jax.ShapeDtypeStruct((bsz, n), F32),
        compiler_params=pltpu.CompilerParams(dimension_semantics=("arbitrary",),
                                             vmem_limit_bytes=VMEM_LIMIT),
        name="ada",
    )(c, w, b.reshape(1, n))


MOD_SHIFT_A, MOD_SCALE_A, MOD_GATE_A, MOD_SHIFT_F, MOD_SCALE_F, MOD_GATE_F = range(6)


def _mod_spec(which, d):
    return pl.BlockSpec((1, 1, 1, d), lambda b, t: (b, which, 0, 0))


def _rope(t, cos, sin_signed, first_half):
    rot = jnp.where(first_half, pltpu.roll(t, LANES - HEAD_DIM // 2, 1), pltpu.roll(t, HEAD_DIM // 2, 1))
    return t * cos + rot * sin_signed


def _in_proj_kernel(x_ref, g_ref, scale_ref, shift_ref, w_ref, cos_ref, sin_ref,
                    qa_ref, ka_ref, va_ref, qb_ref, kvb_ref):
    x = x_ref[0]
    h = ((_rms(x) * g_ref[...]) * (1.0 + scale_ref[0, 0]) + shift_ref[0, 0]).astype(BF16)
    cos = cos_ref[...]
    sin = sin_ref[...]
    tm = x.shape[0]
    first_half = (lax.broadcasted_iota(jnp.int32, (tm, LANES), 1) % HEAD_DIM) < HEAD_DIM // 2

    def proj(lo, hi):
        return jnp.dot(h, w_ref[:, lo:hi], preferred_element_type=F32)

    qa_ref[0] = (proj(0, NA_WIDTH) * QK_SCALE).astype(BF16)
    ka_ref[0] = proj(NA_WIDTH, 2 * NA_WIDTH).astype(BF16)
    va_ref[0] = proj(2 * NA_WIDTH, 3 * NA_WIDTH).astype(BF16)
    base = 3 * NA_WIDTH
    qb = proj(base, base + SW_WIDTH)
    for j in range(SW_WIDTH // LANES):
        blk = _rope(qb[:, j * LANES:(j + 1) * LANES], cos, sin, first_half) * QK_SCALE
        qb_ref[0, :, j * LANES:(j + 1) * LANES] = blk.astype(BF16)
    kv = proj(base + SW_WIDTH, base + SW_WIDTH + 2 * SW_KV_WIDTH)
    kvb_ref[0, :, :LANES] = _rope(kv[:, :LANES], cos, sin, first_half).astype(BF16)
    kvb_ref[0, :, LANES:] = kv[:, LANES:].astype(BF16)


def _in_proj(x, g, mod, w, cos, sin):
    bsz, s, d = x.shape
    n = w.shape[1]
    tm = IN_TM
    row = lambda b, t: (b, t, 0)
    outs = [jax.ShapeDtypeStruct((bsz, s, NA_WIDTH), BF16)] * 3 + [
        jax.ShapeDtypeStruct((bsz, s, SW_WIDTH), BF16),
        jax.ShapeDtypeStruct((bsz, s, 2 * SW_KV_WIDTH), BF16)]
    return pl.pallas_call(
        _in_proj_kernel,
        grid=(bsz, s // tm),
        in_specs=[pl.BlockSpec((1, tm, d), row),
                  pl.BlockSpec((1, d), lambda b, t: (0, 0)),
                  _mod_spec(MOD_SCALE_A, d),
                  _mod_spec(MOD_SHIFT_A, d),
                  pl.BlockSpec((d, n), lambda b, t: (0, 0)),
                  pl.BlockSpec((tm, LANES), lambda b, t: (t, 0)),
                  pl.BlockSpec((tm, LANES), lambda b, t: (t, 0))],
        out_specs=[pl.BlockSpec((1, tm, NA_WIDTH), row)] * 3 + [
            pl.BlockSpec((1, tm, SW_WIDTH), row),
            pl.BlockSpec((1, tm, 2 * SW_KV_WIDTH), row)],
        out_shape=outs,
        compiler_params=pltpu.CompilerParams(dimension_semantics=("arbitrary", "arbitrary"),
                                             vmem_limit_bytes=VMEM_LIMIT),
        name="in_proj",
    )(x, g, mod, mod, w, cos, sin)


def _na_kernel(q_ref, k_ref, v_ref, bias_ref, o_ref, *, rows):
    lo = _lo_mask(GRID_W)
    win = NA_ROWS_WIN * GRID_W

    def scores(r):
        rs = jnp.clip(r - NA_ROWS_WIN // 2, 0, rows - NA_ROWS_WIN)
        q = q_ref[0, pl.ds(pl.multiple_of(r * GRID_W, GRID_W), GRID_W), :]
        k0 = pl.multiple_of(rs * GRID_W, GRID_W)
        zero = jnp.zeros_like(q)
        qs = jnp.concatenate([jnp.where(lo, q, zero), jnp.where(lo, zero, q)], axis=0)
        return _nt_dot(qs, k_ref[0, pl.ds(k0, win), :]) + bias_ref[0, r - rs], k0

    def finish(r, s, k0):
        m = jnp.max(s, axis=-1, keepdims=True)
        p = jnp.exp(s - m)
        den = jnp.sum(p, axis=-1, keepdims=True)
        o = jnp.dot(p.astype(BF16), v_ref[0, pl.ds(k0, win), :], preferred_element_type=F32) * (1.0 / den)
        out = jnp.where(lo, o[:GRID_W], o[GRID_W:])
        o_ref[0, pl.ds(pl.multiple_of(r * GRID_W, GRID_W), GRID_W), :] = out.astype(o_ref.dtype)

    def body(g, carry):
        staged = [scores(g * NA_GROUP + i) for i in range(NA_GROUP)]
        for i, (s, k0) in enumerate(staged):
            finish(g * NA_GROUP + i, s, k0)
        return carry

    lax.fori_loop(0, rows // NA_GROUP, body, 0)


def _na_attention(q, k, v, bias):
    bsz, s, _ = q.shape
    rows = s // GRID_W
    pairs = NA_WIDTH // LANES
    blk = pl.BlockSpec((1, s, LANES), lambda hp, b: (b, 0, hp))
    return pl.pallas_call(
        functools.partial(_na_kernel, rows=rows),
        grid=(pairs, bsz),
        in_specs=[blk, blk, blk,
                  pl.BlockSpec((1,) + bias.shape[1:], lambda hp, b: (hp, 0, 0, 0))],
        out_specs=blk,
        out_shape=jax.ShapeDtypeStruct((bsz, s, NA_WIDTH), BF16),
        compiler_params=pltpu.CompilerParams(dimension_semantics=("arbitrary", "arbitrary"),
                                             vmem_limit_bytes=VMEM_LIMIT),
        name="na_attn",
    )(q, k, v, bias)


def _na_bias_table(rpb, rows):
    assert rows >= 2 * NA_ROWS_WIN
    col = np.arange(GRID_W)
    cs = np.clip(col - NA_COLS // 2, 0, GRID_W - NA_COLS)
    col_ok = (col[None, :] >= cs[:, None]) & (col[None, :] < cs[:, None] + NA_COLS)
    dc = np.clip(col[None, :] - col[:, None] + NA_COLS - 1, 0, 2 * NA_COLS - 2)
    onehot = (np.arange(2 * NA_COLS - 1)[:, None] == dc.reshape(1, -1)).astype(np.float32)
    t = jnp.dot(rpb.astype(F32).reshape(-1, 2 * NA_COLS - 1), onehot, precision=lax.Precision.HIGHEST)
    t = jnp.where(col_ok[None, None], t.reshape(NA_HEADS, 2 * NA_ROWS_WIN - 1, GRID_W, GRID_W), NEG)
    b = jnp.stack([t[:, NA_ROWS_WIN - 1 - c:2 * NA_ROWS_WIN - 1 - c] for c in range(NA_ROWS_WIN)], axis=1)
    b = b.reshape(NA_HEADS // 2, 2, NA_ROWS_WIN, NA_ROWS_WIN, GRID_W, GRID_W)
    b = b.transpose(0, 2, 1, 4, 3, 5)
    return b.reshape(NA_HEADS // 2, NA_ROWS_WIN, 2 * GRID_W, NA_ROWS_WIN * GRID_W)


def _sw_kernel(sink_ref, q_ref, k_ref, v_ref, bias_ref, o_ref, *, nblocks):
    n = pl.program_id(1)
    lo = _lo_mask(SW_BLOCK)
    start = pl.multiple_of(jnp.clip(n - 1, 0, nblocks - 3) * SW_BLOCK, SW_BLOCK)
    variant = jnp.where(n == 0, 0, jnp.where(n == nblocks - 1, 2, 1))
    kw = k_ref[0, pl.ds(start, 3 * SW_BLOCK), :]
    vw = v_ref[0, pl.ds(start, 3 * SW_BLOCK), :]
    q = q_ref[0]
    zero = jnp.zeros((SW_BLOCK, LANES), q.dtype)
    parts = []
    for j in range(SW_WIDTH // LANES):
        qj = q[:, j * LANES:(j + 1) * LANES]
        parts += [jnp.where(lo, qj, zero), jnp.where(lo, zero, qj)]
    s = _nt_dot(jnp.concatenate(parts, axis=0), kw)
    bias = bias_ref[variant]
    ps, invs = [], []
    for g in range(SW_HEADS):
        sg = s[g * SW_BLOCK:(g + 1) * SW_BLOCK] + bias
        sk = sink_ref[g]
        m = jnp.maximum(jnp.max(sg, axis=-1, keepdims=True), sk)
        p = jnp.exp(sg - m)
        den = jnp.sum(p, axis=-1, keepdims=True) + jnp.exp(sk - m)
        ps.append(p.astype(BF16))
        invs.append(1.0 / den)
    o = jnp.dot(jnp.concatenate(ps, axis=0), vw, preferred_element_type=F32)
    for j in range(SW_WIDTH // LANES):
        olo = o[(2 * j) * SW_BLOCK:(2 * j + 1) * SW_BLOCK] * invs[2 * j]
        ohi = o[(2 * j + 1) * SW_BLOCK:(2 * j + 2) * SW_BLOCK] * invs[2 * j + 1]
        o_ref[0, :, j * LANES:(j + 1) * LANES] = jnp.where(lo, olo, ohi).astype(o_ref.dtype)


def _sw_attention(sink, q, kv, bias):
    bsz, s, _ = q.shape
    nblocks = s // SW_BLOCK
    assert nblocks >= 3
    grid_spec = pltpu.PrefetchScalarGridSpec(
        num_scalar_prefetch=1,
        grid=(bsz, nblocks),
        in_specs=[pl.BlockSpec((1, SW_BLOCK, SW_WIDTH), lambda b, n, sk: (b, n, 0)),
                  pl.BlockSpec((1, s, LANES), lambda b, n, sk: (b, 0, 0)),
                  pl.BlockSpec((1, s, LANES), lambda b, n, sk: (b, 0, 1)),
                  pl.BlockSpec(bias.shape, lambda b, n, sk: (0, 0, 0))],
        out_specs=pl.BlockSpec((1, SW_BLOCK, SW_WIDTH), lambda b, n, sk: (b, n, 0)),
    )
    return pl.pallas_call(
        functools.partial(_sw_kernel, nblocks=nblocks),
        grid_spec=grid_spec,
        out_shape=jax.ShapeDtypeStruct((bsz, s, SW_WIDTH), BF16),
        compiler_params=pltpu.CompilerParams(dimension_semantics=("arbitrary", "arbitrary"),
                                             vmem_limit_bytes=VMEM_LIMIT),
        name="sw_attn",
    )(sink, q, kv, kv, bias)


def _sw_bias_table():
    i = np.arange(SW_BLOCK)[:, None]
    j = np.arange(3 * SW_BLOCK)[None, :]
    ok = np.stack([np.abs(i - j) <= SW_BLOCK,
                   np.abs(i + SW_BLOCK - j) <= SW_BLOCK,
                   np.abs(i + 2 * SW_BLOCK - j) <= SW_BLOCK])
    return jnp.asarray(np.where(ok, 0.0, NEG), F32)


def _out_proj_kernel(x_ref, oa_ref, ob_ref, gna_ref, gsw_ref, w_ref, gate_ref, o_ref):
    oa = (_rms(oa_ref[0].astype(F32)) * gna_ref[...]).astype(BF16)
    ob = (_rms(ob_ref[0].astype(F32)) * gsw_ref[...]).astype(BF16)
    mix = (jnp.dot(oa, w_ref[:NA_WIDTH], preferred_element_type=F32)
           + jnp.dot(ob, w_ref[NA_WIDTH:], preferred_element_type=F32))
    o_ref[0] = x_ref[0] + gate_ref[0, 0] * mix


def _out_proj(x, oa, ob, gna, gsw, w, mod):
    bsz, s, d = x.shape
    tm = OUT_TM
    row = lambda b, t: (b, t, 0)
    return pl.pallas_call(
        _out_proj_kernel,
        grid=(bsz, s // tm),
        in_specs=[pl.BlockSpec((1, tm, d), row),
                  pl.BlockSpec((1, tm, NA_WIDTH), row),
                  pl.BlockSpec((1, tm, SW_WIDTH), row),
                  pl.BlockSpec((1, NA_WIDTH), lambda b, t: (0, 0)),
                  pl.BlockSpec((1, SW_WIDTH), lambda b, t: (0, 0)),
                  pl.BlockSpec(w.shape, lambda b, t: (0, 0)),
                  _mod_spec(MOD_GATE_A, d)],
        out_specs=pl.BlockSpec((1, tm, d), row),
        out_shape=jax.ShapeDtypeStruct((bsz, s, d), F32),
        compiler_params=pltpu.CompilerParams(dimension_semantics=("arbitrary", "arbitrary"),
                                             vmem_limit_bytes=VMEM_LIMIT),
        name="out_proj",
    )(x, oa, ob, gna, gsw, w, mod)


def _ffn_kernel(x_ref, g_ref, scale_ref, shift_ref, gate_ref, wup_ref, cw_ref, cb_ref, wdn_ref, gfin_ref,
                o_ref, h_ref, *, row_chunk):
    f = pl.program_id(1)
    s = x_ref.shape[1]
    nchunks = s // row_chunk

    @pl.when(f == 0)
    def _():
        gm = g_ref[...]
        sc = 1.0 + scale_ref[0, 0]
        sh = shift_ref[0, 0]

        def body(i, carry):
            r0 = pl.multiple_of(i * row_chunk, row_chunk)
            xr = x_ref[0, pl.ds(r0, row_chunk), :]
            h_ref[pl.ds(r0, row_chunk), :] = ((_rms(xr) * gm) * sc + sh).astype(BF16)
            o_ref[0, pl.ds(r0, row_chunk), :] = jnp.zeros((row_chunk, xr.shape[1]), F32)
            return carry

        lax.fori_loop(0, nchunks, body, 0)

    nblk = s // FF_ROWS
    cw = cw_ref[0]
    cb = cb_ref[0]
    ridx = lax.broadcasted_iota(jnp.int32, (FF_ROWS, FF_CHUNK), 0)
    edge = jnp.zeros((1, FF_CHUNK), F32)

    def up(i):
        return jnp.dot(h_ref[i * FF_ROWS:(i + 1) * FF_ROWS, :], wup_ref[0], preferred_element_type=F32)

    def act(i, ups):
        val = ups[i][:, :FF_CHUNK]
        gt = ups[i][:, FF_CHUNK:]
        before = ups[i - 1][FF_ROWS - 1:, FF_CHUNK:] if i > 0 else edge
        after = ups[i + 1][:1, FF_CHUNK:] if i < nblk - 1 else edge
        prev = jnp.where(ridx == 0, before, pltpu.roll(gt, 1, 0))
        nxt = jnp.where(ridx == FF_ROWS - 1, after, pltpu.roll(gt, FF_ROWS - 1, 0))
        gc = prev * cw[0:1] + gt * cw[1:2] + nxt * cw[2:3] + cb
        return (gc * jax.nn.sigmoid(gc) * val).astype(BF16)

    ups = {0: up(0), 1: up(1)}
    for i in range(nblk):
        a = act(i, ups)
        o_ref[0, i * FF_ROWS:(i + 1) * FF_ROWS, :] += jnp.dot(a, wdn_ref[...], preferred_element_type=F32)
        if i + 2 < nblk:
            ups[i + 2] = up(i + 2)

    @pl.when(f == pl.num_programs(1) - 1)
    def _():
        gate = gate_ref[0, 0]
        gfin = gfin_ref[...]

        def body(i, carry):
            r0 = pl.multiple_of(i * row_chunk, row_chunk)
            y = x_ref[0, pl.ds(r0, row_chunk), :] + gate * o_ref[0, pl.ds(r0, row_chunk), :]
            o_ref[0, pl.ds(r0, row_chunk), :] = _rms(y) * gfin
            return carry

        lax.fori_loop(0, nchunks, body, 0)


def _ffn(x1, g, mod, wup, cw, cb, wdn, gfin):
    bsz, s, d = x1.shape
    nf = wup.shape[0]
    return pl.pallas_call(
        functools.partial(_ffn_kernel, row_chunk=256),
        grid=(bsz, nf),
        in_specs=[pl.BlockSpec((1, s, d), lambda b, f: (b, 0, 0), pipeline_mode=pl.Buffered(1)),
                  pl.BlockSpec((1, d), lambda b, f: (0, 0)),
                  _mod_spec(MOD_SCALE_F, d),
                  _mod_spec(MOD_SHIFT_F, d),
                  _mod_spec(MOD_GATE_F, d),
                  pl.BlockSpec((1, d, 2 * FF_CHUNK), lambda b, f: (f, 0, 0)),
                  pl.BlockSpec((1, CONV_W, FF_CHUNK), lambda b, f: (f, 0, 0)),
                  pl.BlockSpec((1, 1, FF_CHUNK), lambda b, f: (f, 0, 0)),
                  pl.BlockSpec((FF_CHUNK, d), lambda b, f: (f, 0)),
                  pl.BlockSpec((1, d), lambda b, f: (0, 0))],
        out_specs=pl.BlockSpec((1, s, d), lambda b, f: (b, 0, 0)),
        out_shape=jax.ShapeDtypeStruct((bsz, s, d), F32),
        scratch_shapes=[pltpu.VMEM((s, d), BF16)],
        compiler_params=pltpu.CompilerParams(dimension_semantics=("arbitrary", "arbitrary"),
                                             vmem_limit_bytes=VMEM_LIMIT),
        name="ffn",
    )(x1, g, mod, mod, mod, wup, cw, cb, wdn, gfin)


def _rope_tables(s):
    half = HEAD_DIM // 2
    inv = ROPE_THETA ** (-jnp.arange(half, dtype=F32) / half)
    ang = jnp.arange(s).astype(F32)[:, None] * inv[None, :]
    cos = jnp.cos(ang)
    sin = jnp.sin(ang)
    reps = LANES // HEAD_DIM
    return (jnp.tile(jnp.concatenate([cos, cos], axis=-1), (1, reps)),
            jnp.tile(jnp.concatenate([-sin, sin], axis=-1), (1, reps)))


_SW_HEAD_ORDER = np.arange(SW_HEADS).reshape(SW_KV_HEADS, SW_GROUP).T.reshape(-1)
_SW_COL_ORDER = (_SW_HEAD_ORDER[:, None] * HEAD_DIM + np.arange(HEAD_DIM)[None, :]).reshape(-1)


def kernel(x, c, w_ada, b_ada, g_attn, w_in, na_rpb, sw_sink, g_na_out, g_sw_out, w_out, g_ffn, w_up,
           conv_w, conv_b, w_down, g_final):
    bsz, s, d = x.shape
    depth = w_ada.shape[0]
    d_ff = w_down.shape[1]
    nf = d_ff // FF_CHUNK
    assert depth == 1, "the final rmsnorm is fused into the (only) layer's ffn call"
    assert d_ff % FF_CHUNK == 0 and s % GRID_W == 0 and s % SW_BLOCK == 0
    cos, sin = _rope_tables(s)
    sw_bias = _sw_bias_table()
    qb0 = 3 * NA_WIDTH
    for l in range(depth):
        mod = _ada(c, w_ada[l], b_ada[l]).reshape(bsz, 6, 1, d)

        wi = w_in[l]
        wi = jnp.concatenate([wi[:, :qb0], wi[:, qb0:qb0 + SW_WIDTH][:, _SW_COL_ORDER],
                              wi[:, qb0 + SW_WIDTH:]], axis=1).astype(BF16)
        qa, ka, va, qb, kvb = _in_proj(x, g_attn[l].reshape(1, d), mod, wi, cos, sin)

        o_a = _na_attention(qa, ka, va, _na_bias_table(na_rpb[l], s // GRID_W))
        o_b = _sw_attention(sw_sink[l][_SW_HEAD_ORDER].astype(F32), qb, kvb, sw_bias)

        wo = w_out[l]
        wo = jnp.concatenate([wo[:NA_WIDTH], wo[NA_WIDTH:][_SW_COL_ORDER]], axis=0).astype(BF16)
        x = _out_proj(x, o_a, o_b, g_na_out[l].reshape(1, -1), g_sw_out[l][_SW_COL_ORDER].reshape(1, -1),
                      wo, mod)

        wu = w_up[l]
        wu = jnp.concatenate([wu[:, :d_ff].reshape(d, nf, FF_CHUNK), wu[:, d_ff:].reshape(d, nf, FF_CHUNK)],
                             axis=-1).transpose(1, 0, 2).astype(BF16)
        cw = conv_w[l].reshape(CONV_W, nf, FF_CHUNK).transpose(1, 0, 2)
        cb = conv_b[l].reshape(nf, 1, FF_CHUNK)
        x = _ffn(x, g_ffn[l].reshape(1, d), mod, wu, cw, cb, w_down[l].astype(BF16), g_final.reshape(1, d))
    return x
```

```python
import functools

import jax
import jax.numpy as jnp
import numpy as np
from jax import lax
from jax.experimental import pallas as pl
from jax.experimental.pallas import tpu as pltpu

F32 = jnp.float32
BF16 = jnp.bfloat16

HEAD_DIM = 64
LANES = 128
NA_HEADS = 8
NA_WIDTH = NA_HEADS * HEAD_DIM
GRID_W = 64
NA_ROWS_WIN = 8
NA_COLS = 16
SW_HEADS = 8
SW_KV_HEADS = 2
SW_GROUP = SW_HEADS // SW_KV_HEADS
SW_WIDTH = SW_HEADS * HEAD_DIM
SW_KV_WIDTH = SW_KV_HEADS * HEAD_DIM
SW_BLOCK = 128
CONV_W = 3
ROPE_THETA = 10000.0
EPS = 1e-6
NEG = -1e30
QK_SCALE = HEAD_DIM ** -0.5

VMEM_LIMIT = 56 * 1024 * 1024

IN_TM = 512
OUT_TM = 512
FF_CHUNK = 256
FF_ROWS = 512
NA_GROUP = 16
SW_STEP_BLOCKS = 4
SW_AHEAD = 2
NA_AHEAD = 3


def _rms(x):
    return x * lax.rsqrt(jnp.mean(x * x, axis=-1, keepdims=True) + EPS)


def _nt_dot(a, b):
    return lax.dot_general(a, b, (((1,), (1,)), ((), ())), preferred_element_type=F32)


def _lo_mask(rows):
    return lax.broadcasted_iota(jnp.int32, (rows, LANES), 1) < HEAD_DIM


def _ada_kernel(c_ref, w_ref, b_ref, o_ref):
    c = c_ref[...]
    sc = c * jax.nn.sigmoid(c)
    o_ref[...] = jnp.dot(sc, w_ref[...], preferred_element_type=F32,
                         precision=lax.Precision.HIGHEST) + b_ref[...]


def _ada(c, w, b):
    bsz, d = c.shape
    n = w.shape[1]
    tn = 1024
    return pl.pallas_call(
        _ada_kernel,
        grid=(n // tn,),
        in_specs=[pl.BlockSpec((bsz, d), lambda j: (0, 0)),
                  pl.BlockSpec((d, tn), lambda j: (0, j)),
                  pl.BlockSpec((1, tn), lambda j: (0, j))],
        out_specs=pl.BlockSpec((bsz, tn), lambda j: (0, j)),
        out_shape=jax.ShapeDtypeStruct((bsz, n), F32),
        compiler_params=pltpu.CompilerParams(dimension_semantics=("arbitrary",),
                                             vmem_limit_bytes=VMEM_LIMIT),
        name="ada",
    )(c, w, b.reshape(1, n))


MOD_SHIFT_A, MOD_SCALE_A, MOD_GATE_A, MOD_SHIFT_F, MOD_SCALE_F, MOD_GATE_F = range(6)


def _mod_spec(which, d):
    return pl.BlockSpec((1, 1, 1, d), lambda b, t: (b, which, 0, 0))


def _rope(t, cos, sin_signed, first_half):
    rot = jnp.where(first_half, pltpu.roll(t, LANES - HEAD_DIM // 2, 1), pltpu.roll(t, HEAD_DIM // 2, 1))
    return t * cos + rot * sin_signed


def _in_proj_kernel(x_ref, g_ref, scale_ref, shift_ref, w_ref, cos_ref, sin_ref,
                    qa_ref, ka_ref, va_ref, qb_ref, kvb_ref):
    x = x_ref[0]
    h = ((_rms(x) * g_ref[...]) * (1.0 + scale_ref[0, 0]) + shift_ref[0, 0]).astype(BF16)
    cos = cos_ref[...]
    sin = sin_ref[...]
    tm = x.shape[0]
    first_half = (lax.broadcasted_iota(jnp.int32, (tm, LANES), 1) % HEAD_DIM) < HEAD_DIM // 2

    def proj(lo, hi):
        return jnp.dot(h, w_ref[:, lo:hi], preferred_element_type=F32)

    qa_ref[0] = (proj(0, NA_WIDTH) * QK_SCALE).astype(BF16)
    ka_ref[0] = proj(NA_WIDTH, 2 * NA_WIDTH).astype(BF16)
    va_ref[0] = proj(2 * NA_WIDTH, 3 * NA_WIDTH).astype(BF16)
    base = 3 * NA_WIDTH
    qb = proj(base, base + SW_WIDTH)
    for j in range(SW_WIDTH // LANES):
        blk = _rope(qb[:, j * LANES:(j + 1) * LANES], cos, sin, first_half) * QK_SCALE
        qb_ref[0, :, j * LANES:(j + 1) * LANES] = blk.astype(BF16)
    kv = proj(base + SW_WIDTH, base + SW_WIDTH + 2 * SW_KV_WIDTH)
    kvb_ref[0, :, :LANES] = _rope(kv[:, :LANES], cos, sin, first_half).astype(BF16)
    kvb_ref[0, :, LANES:] = kv[:, LANES:].astype(BF16)


def _in_proj(x, g, mod, w, cos, sin):
    bsz, s, d = x.shape
    n = w.shape[1]
    tm = IN_TM
    row = lambda b, t: (b, t, 0)
    outs = [jax.ShapeDtypeStruct((bsz, s, NA_WIDTH), BF16)] * 3 + [
        jax.ShapeDtypeStruct((bsz, s, SW_WIDTH), BF16),
        jax.ShapeDtypeStruct((bsz, s, 2 * SW_KV_WIDTH), BF16)]
    return pl.pallas_call(
        _in_proj_kernel,
        grid=(bsz, s // tm),
        in_specs=[pl.BlockSpec((1, tm, d), row),
                  pl.BlockSpec((1, d), lambda b, t: (0, 0)),
                  _mod_spec(MOD_SCALE_A, d),
                  _mod_spec(MOD_SHIFT_A, d),
                  pl.BlockSpec((d, n), lambda b, t: (0, 0)),
                  pl.BlockSpec((tm, LANES), lambda b, t: (t, 0)),
                  pl.BlockSpec((tm, LANES), lambda b, t: (t, 0))],
        out_specs=[pl.BlockSpec((1, tm, NA_WIDTH), row)] * 3 + [
            pl.BlockSpec((1, tm, SW_WIDTH), row),
            pl.BlockSpec((1, tm, 2 * SW_KV_WIDTH), row)],
        out_shape=outs,
        compiler_params=pltpu.CompilerParams(dimension_semantics=("arbitrary", "arbitrary"),
                                             vmem_limit_bytes=VMEM_LIMIT),
        name="in_proj",
    )(x, g, mod, mod, w, cos, sin)


def _na_kernel(q_ref, k_ref, v_ref, bias_ref, o_ref, *, rows):
    lo = _lo_mask(GRID_W)
    win = NA_ROWS_WIN * GRID_W

    def scores(r):
        rs = jnp.clip(r - NA_ROWS_WIN // 2, 0, rows - NA_ROWS_WIN)
        q = q_ref[0, pl.ds(pl.multiple_of(r * GRID_W, GRID_W), GRID_W), :]
        k0 = pl.multiple_of(rs * GRID_W, GRID_W)
        zero = jnp.zeros_like(q)
        qs = jnp.concatenate([jnp.where(lo, q, zero), jnp.where(lo, zero, q)], axis=0)
        return _nt_dot(qs, k_ref[0, pl.ds(k0, win), :]) + bias_ref[0, r - rs], k0

    def finish(r, s, k0):
        m = jnp.max(s, axis=-1, keepdims=True)
        p = jnp.exp(s - m)
        den = jnp.sum(p, axis=-1, keepdims=True)
        o = jnp.dot(p.astype(BF16), v_ref[0, pl.ds(k0, win), :], preferred_element_type=F32) * (1.0 / den)
        out = jnp.where(lo, o[:GRID_W], o[GRID_W:])
        o_ref[0, pl.ds(pl.multiple_of(r * GRID_W, GRID_W), GRID_W), :] = out.astype(o_ref.dtype)

    def body(g, carry):
        base = g * NA_GROUP
        staged = [scores(base + i) for i in range(NA_AHEAD)]
        for i in range(NA_GROUP):
            if i + NA_AHEAD < NA_GROUP:
                staged.append(scores(base + i + NA_AHEAD))
            finish(base + i, *staged[i])
        return carry

    lax.fori_loop(0, rows // NA_GROUP, body, 0)


def _na_attention(q, k, v, bias):
    bsz, s, _ = q.shape
    rows = s // GRID_W
    pairs = NA_WIDTH // LANES
    blk = pl.BlockSpec((1, s, LANES), lambda hp, b: (b, 0, hp))
    return pl.pallas_call(
        functools.partial(_na_kernel, rows=rows),
        grid=(pairs, bsz),
        in_specs=[blk, blk, blk,
                  pl.BlockSpec((1,) + bias.shape[1:], lambda hp, b: (hp, 0, 0, 0))],
        out_specs=blk,
        out_shape=jax.ShapeDtypeStruct((bsz, s, NA_WIDTH), BF16),
        compiler_params=pltpu.CompilerParams(dimension_semantics=("arbitrary", "arbitrary"),
                                             vmem_limit_bytes=VMEM_LIMIT),
        name="na_attn",
    )(q, k, v, bias)


def _na_bias_table(rpb, rows):
    assert rows >= 2 * NA_ROWS_WIN
    col = np.arange(GRID_W)
    cs = np.clip(col - NA_COLS // 2, 0, GRID_W - NA_COLS)
    col_ok = (col[None, :] >= cs[:, None]) & (col[None, :] < cs[:, None] + NA_COLS)
    dc = np.clip(col[None, :] - col[:, None] + NA_COLS - 1, 0, 2 * NA_COLS - 2)
    onehot = (np.arange(2 * NA_COLS - 1)[:, None] == dc.reshape(1, -1)).astype(np.float32)
    t = jnp.dot(rpb.astype(F32).reshape(-1, 2 * NA_COLS - 1), onehot, precision=lax.Precision.HIGHEST)
    t = jnp.where(col_ok[None, None], t.reshape(NA_HEADS, 2 * NA_ROWS_WIN - 1, GRID_W, GRID_W), NEG)
    b = jnp.stack([t[:, NA_ROWS_WIN - 1 - c:2 * NA_ROWS_WIN - 1 - c] for c in range(NA_ROWS_WIN)], axis=1)
    b = b.reshape(NA_HEADS // 2, 2, NA_ROWS_WIN, NA_ROWS_WIN, GRID_W, GRID_W)
    b = b.transpose(0, 2, 1, 4, 3, 5)
    return b.reshape(NA_HEADS // 2, NA_ROWS_WIN, 2 * GRID_W, NA_ROWS_WIN * GRID_W)


def _sw_kernel(sink_ref, q_ref, k_ref, v_ref, bias_ref, o_ref, *, nblocks):
    lo = _lo_mask(SW_BLOCK)
    zero = jnp.zeros((SW_BLOCK, LANES), q_ref.dtype)
    npairs = SW_WIDTH // LANES

    def window(u):
        n = pl.program_id(1) * SW_STEP_BLOCKS + u
        start = pl.multiple_of(jnp.clip(n - 1, 0, nblocks - 3) * SW_BLOCK, SW_BLOCK)
        variant = jnp.where(n == 0, 0, jnp.where(n == nblocks - 1, 2, 1))
        return start, variant

    def scores(u, j):
        start, _ = window(u)
        qj = q_ref[0, u * SW_BLOCK:(u + 1) * SW_BLOCK, j * LANES:(j + 1) * LANES]
        qs = jnp.concatenate([jnp.where(lo, qj, zero), jnp.where(lo, zero, qj)], axis=0)
        return _nt_dot(qs, k_ref[0, pl.ds(start, 3 * SW_BLOCK), :])

    def finish(u, j, s):
        start, variant = window(u)
        bias = bias_ref[variant]
        ps, invs = [], []
        for half in range(2):
            sg = s[half * SW_BLOCK:(half + 1) * SW_BLOCK] + bias
            sk = sink_ref[2 * j + half]
            m = jnp.maximum(jnp.max(sg, axis=-1, keepdims=True), sk)
            p = jnp.exp(sg - m)
            den = jnp.sum(p, axis=-1, keepdims=True) + jnp.exp(sk - m)
            ps.append(p.astype(BF16))
            invs.append(1.0 / den)
        o = jnp.dot(jnp.concatenate(ps, axis=0), v_ref[0, pl.ds(start, 3 * SW_BLOCK), :],
                    preferred_element_type=F32)
        out = jnp.where(lo, o[:SW_BLOCK] * invs[0], o[SW_BLOCK:] * invs[1])
        o_ref[0, u * SW_BLOCK:(u + 1) * SW_BLOCK, j * LANES:(j + 1) * LANES] = out.astype(o_ref.dtype)

    units = [(u, j) for u in range(SW_STEP_BLOCKS) for j in range(npairs)]
    staged = [scores(*units[i]) for i in range(SW_AHEAD)]
    for i, unit in enumerate(units):
        if i + SW_AHEAD < len(units):
            staged.append(scores(*units[i + SW_AHEAD]))
        finish(*unit, staged[i])
        staged[i] = None


def _sw_attention(sink, q, kv, bias):
    bsz, s, _ = q.shape
    nblocks = s // SW_BLOCK
    assert nblocks >= 3 and nblocks % SW_STEP_BLOCKS == 0
    rows = SW_STEP_BLOCKS * SW_BLOCK
    grid_spec = pltpu.PrefetchScalarGridSpec(
        num_scalar_prefetch=1,
        grid=(bsz, nblocks // SW_STEP_BLOCKS),
        in_specs=[pl.BlockSpec((1, rows, SW_WIDTH), lambda b, n, sk: (b, n, 0)),
                  pl.BlockSpec((1, s, LANES), lambda b, n, sk: (b, 0, 0)),
                  pl.BlockSpec((1, s, LANES), lambda b, n, sk: (b, 0, 1)),
                  pl.BlockSpec(bias.shape, lambda b, n, sk: (0, 0, 0))],
        out_specs=pl.BlockSpec((1, rows, SW_WIDTH), lambda b, n, sk: (b, n, 0)),
    )
    return pl.pallas_call(
        functools.partial(_sw_kernel, nblocks=nblocks),
        grid_spec=grid_spec,
        out_shape=jax.ShapeDtypeStruct((bsz, s, SW_WIDTH), BF16),
        compiler_params=pltpu.CompilerParams(dimension_semantics=("arbitrary", "arbitrary"),
                                             vmem_limit_bytes=VMEM_LIMIT),
        name="sw_attn",
    )(sink, q, kv, kv, bias)


def _sw_bias_table():
    i = np.arange(SW_BLOCK)[:, None]
    j = np.arange(3 * SW_BLOCK)[None, :]
    ok = np.stack([np.abs(i - j) <= SW_BLOCK,
                   np.abs(i + SW_BLOCK - j) <= SW_BLOCK,
                   np.abs(i + 2 * SW_BLOCK - j) <= SW_BLOCK])
    return jnp.asarray(np.where(ok, 0.0, NEG), F32)


def _out_proj_kernel(x_ref, oa_ref, ob_ref, gna_ref, gsw_ref, w_ref, gate_ref, o_ref):
    oa = (_rms(oa_ref[0].astype(F32)) * gna_ref[...]).astype(BF16)
    ob = (_rms(ob_ref[0].astype(F32)) * gsw_ref[...]).astype(BF16)
    mix = (jnp.dot(oa, w_ref[:NA_WIDTH], preferred_element_type=F32)
           + jnp.dot(ob, w_ref[NA_WIDTH:], preferred_element_type=F32))
    o_ref[0] = x_ref[0] + gate_ref[0, 0] * mix


def _out_proj(x, oa, ob, gna, gsw, w, mod):
    bsz, s, d = x.shape
    tm = OUT_TM
    row = lambda b, t: (b, t, 0)
    return pl.pallas_call(
        _out_proj_kernel,
        grid=(bsz, s // tm),
        in_specs=[pl.BlockSpec((1, tm, d), row),
                  pl.BlockSpec((1, tm, NA_WIDTH), row),
                  pl.BlockSpec((1, tm, SW_WIDTH), row),
                  pl.BlockSpec((1, NA_WIDTH), lambda b, t: (0, 0)),
                  pl.BlockSpec((1, SW_WIDTH), lambda b, t: (0, 0)),
                  pl.BlockSpec(w.shape, lambda b, t: (0, 0)),
                  _mod_spec(MOD_GATE_A, d)],
        out_specs=pl.BlockSpec((1, tm, d), row),
        out_shape=jax.ShapeDtypeStruct((bsz, s, d), F32),
        compiler_params=pltpu.CompilerParams(dimension_semantics=("arbitrary", "arbitrary"),
                                             vmem_limit_bytes=VMEM_LIMIT),
        name="out_proj",
    )(x, oa, ob, gna, gsw, w, mod)


def _ffn_kernel(x_ref, g_ref, scale_ref, shift_ref, gate_ref, wup_ref, cw_ref, cb_ref, wdn_ref, gfin_ref,
                o_ref, h_ref, *, row_chunk):
    f = pl.program_id(1)
    s = x_ref.shape[1]
    nchunks = s // row_chunk

    @pl.when(f == 0)
    def _():
        gm = g_ref[...]
        sc = 1.0 + scale_ref[0, 0]
        sh = shift_ref[0, 0]

        def body(i, carry):
            r0 = pl.multiple_of(i * row_chunk, row_chunk)
            xr = x_ref[0, pl.ds(r0, row_chunk), :]
            h_ref[pl.ds(r0, row_chunk), :] = ((_rms(xr) * gm) * sc + sh).astype(BF16)
            o_ref[0, pl.ds(r0, row_chunk), :] = jnp.zeros((row_chunk, xr.shape[1]), F32)
            return carry

        lax.fori_loop(0, nchunks, body, 0)

    nblk = s // FF_ROWS
    cw = cw_ref[0]
    cb = cb_ref[0]
    ridx = lax.broadcasted_iota(jnp.int32, (FF_ROWS, FF_CHUNK), 0)
    edge = jnp.zeros((1, FF_CHUNK), F32)

    def up(i):
        return jnp.dot(h_ref[i * FF_ROWS:(i + 1) * FF_ROWS, :], wup_ref[0], preferred_element_type=F32)

    def act(i, ups):
        val = ups[i][:, :FF_CHUNK]
        gt = ups[i][:, FF_CHUNK:]
        before = ups[i - 1][FF_ROWS - 1:, FF_CHUNK:] if i > 0 else edge
        after = ups[i + 1][:1, FF_CHUNK:] if i < nblk - 1 else edge
        prev = jnp.where(ridx == 0, before, pltpu.roll(gt, 1, 0))
        nxt = jnp.where(ridx == FF_ROWS - 1, after, pltpu.roll(gt, FF_ROWS - 1, 0))
        gc = prev * cw[0:1] + gt * cw[1:2] + nxt * cw[2:3] + cb
        return (gc * jax.nn.sigmoid(gc) * val).astype(BF16)

    ups = {0: up(0), 1: up(1)}
    for i in range(nblk):
        a = act(i, ups)
        o_ref[0, i * FF_ROWS:(i + 1) * FF_ROWS, :] += jnp.dot(a, wdn_ref[...], preferred_element_type=F32)
        if i + 2 < nblk:
            ups[i + 2] = up(i + 2)

    @pl.when(f == pl.num_programs(1) - 1)
    def _():
        gate = gate_ref[0, 0]
        gfin = gfin_ref[...]

        def body(i, carry):
            r0 = pl.multiple_of(i * row_chunk, row_chunk)
            y = x_ref[0, pl.ds(r0, row_chunk), :] + gate * o_ref[0, pl.ds(r0, row_chunk), :]
            o_ref[0, pl.ds(r0, row_chunk), :] = _rms(y) * gfin
            return carry

        lax.fori_loop(0, nchunks, body, 0)


def _ffn(x1, g, mod, wup, cw, cb, wdn, gfin):
    bsz, s, d = x1.shape
    nf = wup.shape[0]
    return pl.pallas_call(
        functools.partial(_ffn_kernel, row_chunk=256),
        grid=(bsz, nf),
        in_specs=[pl.BlockSpec((1, s, d), lambda b, f: (b, 0, 0)),
                  pl.BlockSpec((1, d), lambda b, f: (0, 0)),
                  _mod_spec(MOD_SCALE_F, d),
                  _mod_spec(MOD_SHIFT_F, d),
                  _mod_spec(MOD_GATE_F, d),
                  pl.BlockSpec((1, d, 2 * FF_CHUNK), lambda b, f: (f, 0, 0)),
                  pl.BlockSpec((1, CONV_W, FF_CHUNK), lambda b, f: (f, 0, 0)),
                  pl.BlockSpec((1, 1, FF_CHUNK), lambda b, f: (f, 0, 0)),
                  pl.BlockSpec((FF_CHUNK, d), lambda b, f: (f, 0)),
                  pl.BlockSpec((1, d), lambda b, f: (0, 0))],
        out_specs=pl.BlockSpec((1, s, d), lambda b, f: (b, 0, 0)),
        out_shape=jax.ShapeDtypeStruct((bsz, s, d), F32),
        scratch_shapes=[pltpu.VMEM((s, d), BF16)],
        compiler_params=pltpu.CompilerParams(dimension_semantics=("arbitrary", "arbitrary"),
                                             vmem_limit_bytes=VMEM_LIMIT),
        name="ffn",
    )(x1, g, mod, mod, mod, wup, cw, cb, wdn, gfin)


def _rope_tables(s):
    half = HEAD_DIM // 2
    inv = ROPE_THETA ** (-jnp.arange(half, dtype=F32) / half)
    ang = jnp.arange(s).astype(F32)[:, None] * inv[None, :]
    cos = jnp.cos(ang)
    sin = jnp.sin(ang)
    reps = LANES // HEAD_DIM
    return (jnp.tile(jnp.concatenate([cos, cos], axis=-1), (1, reps)),
            jnp.tile(jnp.concatenate([-sin, sin], axis=-1), (1, reps)))


_SW_HEAD_ORDER = np.arange(SW_HEADS).reshape(SW_KV_HEADS, SW_GROUP).T.reshape(-1)
_SW_COL_ORDER = (_SW_HEAD_ORDER[:, None] * HEAD_DIM + np.arange(HEAD_DIM)[None, :]).reshape(-1)


def kernel(x, c, w_ada, b_ada, g_attn, w_in, na_rpb, sw_sink, g_na_out, g_sw_out, w_out, g_ffn, w_up,
           conv_w, conv_b, w_down, g_final):
    bsz, s, d = x.shape
    depth = w_ada.shape[0]
    d_ff = w_down.shape[1]
    nf = d_ff // FF_CHUNK
    assert depth == 1, "the final rmsnorm is fused into the (only) layer's ffn call"
    assert d_ff % FF_CHUNK == 0 and s % GRID_W == 0 and s % SW_BLOCK == 0
    cos, sin = _rope_tables(s)
    sw_bias = _sw_bias_table()
    qb0 = 3 * NA_WIDTH
    for l in range(depth):
        mod = _ada(c, w_ada[l], b_ada[l]).reshape(bsz, 6, 1, d)

        wi = w_in[l]
        wi = jnp.concatenate([wi[:, :qb0], wi[:, qb0:qb0 + SW_WIDTH][:, _SW_COL_ORDER],
                              wi[:, qb0 + SW_WIDTH:]], axis=1).astype(BF16)
        qa, ka, va, qb, kvb = _in_proj(x, g_attn[l].reshape(1, d), mod, wi, cos, sin)

        o_a = _na_attention(qa, ka, va, _na_bias_table(na_rpb[l], s // GRID_W))
        o_b = _sw_attention(sw_sink[l][_SW_HEAD_ORDER].astype(F32), qb, kvb, sw_bias)

        wo = w_out[l]
        wo = jnp.concatenate([wo[:NA_WIDTH], wo[NA_WIDTH:][_SW_COL_ORDER]], axis=0).astype(BF16)
        x = _out_proj(x, o_a, o_b, g_na_out[l].reshape(1, -1), g_sw_out[l][_SW_COL_ORDER].reshape(1, -1),
                      wo, mod)

        wu = w_up[l]
        wu = jnp.concatenate([wu[:, :d_ff].reshape(d, nf, FF_CHUNK), wu[:, d_ff:].reshape(d, nf, FF_CHUNK)],
                             axis=-1).transpose(1, 0, 2).astype(BF16)
        cw = conv_w[l].reshape(CONV_W, nf, FF_CHUNK).transpose(1, 0, 2)
        cb = conv_b[l].reshape(nf, 1, FF_CHUNK)
        x = _ffn(x, g_ffn[l].reshape(1, d), mod, wu, cw, cb, w_down[l].astype(BF16), g_final.reshape(1, d))
    return x
```

```python
import functools

import jax
import jax.numpy as jnp
import numpy as np
from jax import lax
from jax.experimental import pallas as pl
from jax.experimental.pallas import tpu as pltpu

F32 = jnp.float32
BF16 = jnp.bfloat16

HEAD_DIM = 64
LANES = 128
NA_HEADS = 8
NA_WIDTH = NA_HEADS * HEAD_DIM
GRID_W = 64
NA_ROWS_WIN = 8
NA_COLS = 16
SW_HEADS = 8
SW_KV_HEADS = 2
SW_GROUP = SW_HEADS // SW_KV_HEADS
SW_WIDTH = SW_HEADS * HEAD_DIM
SW_KV_WIDTH = SW_KV_HEADS * HEAD_DIM
SW_BLOCK = 128
CONV_W = 3
ROPE_THETA = 10000.0
EPS = 1e-6
NEG = -1e30
LOG2E = 1.4426950408889634
Q_SCALE = HEAD_DIM ** -0.5 * LOG2E

VMEM_LIMIT = 56 * 1024 * 1024

IN_TM = 1024
IN_ROWS = 256
OUT_TM = 512
FF_CHUNK = 256
FF_ROWS = 512
NA_GROUP = 16
SW_STEP_BLOCKS = 4
SW_AHEAD = 2
NA_AHEAD = 3


def _rms(x):
    return x * lax.rsqrt(jnp.mean(x * x, axis=-1, keepdims=True) + EPS)


def _nt_dot(a, b):
    return lax.dot_general(a, b, (((1,), (1,)), ((), ())), preferred_element_type=F32)


def _lo_mask(rows):
    return lax.broadcasted_iota(jnp.int32, (rows, LANES), 1) < HEAD_DIM


def _ada_kernel(c_ref, w_ref, b_ref, o_ref):
    c = c_ref[...]
    sc = c * jax.nn.sigmoid(c)
    o_ref[...] = jnp.dot(sc, w_ref[...], preferred_element_type=F32,
                         precision=lax.Precision.HIGHEST) + b_ref[...]


def _ada(c, w, b):
    bsz, d = c.shape
    n = w.shape[1]
    tn = 1024
    return pl.pallas_call(
        _ada_kernel,
        grid=(n // tn,),
        in_specs=[pl.BlockSpec((bsz, d), lambda j: (0, 0)),
                  pl.BlockSpec((d, tn), lambda j: (0, j)),
                  pl.BlockSpec((1, tn), lambda j: (0, j))],
        out_specs=pl.BlockSpec((bsz, tn), lambda j: (0, j)),
        out_shape=jax.ShapeDtypeStruct((bsz, n), F32),
        compiler_params=pltpu.CompilerParams(dimension_semantics=("arbitrary",),
                                             vmem_limit_bytes=VMEM_LIMIT),
        name="ada",
    )(c, w, b.reshape(1, n))


MOD_SHIFT_A, MOD_SCALE_A, MOD_GATE_A, MOD_SHIFT_F, MOD_SCALE_F, MOD_GATE_F = range(6)


def _mod_spec(which, d):
    return pl.BlockSpec((1, 1, 1, d), lambda b, t: (b, which, 0, 0))


def _rope(t, cos, sin_signed, first_half):
    rot = jnp.where(first_half, pltpu.roll(t, LANES - HEAD_DIM // 2, 1), pltpu.roll(t, HEAD_DIM // 2, 1))
    return t * cos + rot * sin_signed


def _in_proj_kernel(x_ref, g_ref, scale_ref, shift_ref, w_ref, cos_ref, sin_ref,
                    qa_ref, ka_ref, va_ref, qb_ref, kvb_ref):
    gain = g_ref[...]
    scale = 1.0 + scale_ref[0, 0]
    shift = shift_ref[0, 0]
    first_half = (lax.broadcasted_iota(jnp.int32, (IN_ROWS, LANES), 1) % HEAD_DIM) < HEAD_DIM // 2
    base = 3 * NA_WIDTH

    for i in range(x_ref.shape[1] // IN_ROWS):
        rows = slice(i * IN_ROWS, (i + 1) * IN_ROWS)
        h = ((_rms(x_ref[0, rows, :]) * gain) * scale + shift).astype(BF16)
        cos = cos_ref[rows, :]
        sin = sin_ref[rows, :]

        def proj(lo, hi):
            return jnp.dot(h, w_ref[:, lo:hi], preferred_element_type=F32)

        qa_ref[0, rows, :] = (proj(0, NA_WIDTH) * Q_SCALE).astype(BF16)
        ka_ref[0, rows, :] = proj(NA_WIDTH, 2 * NA_WIDTH).astype(BF16)
        va_ref[0, rows, :] = proj(2 * NA_WIDTH, 3 * NA_WIDTH).astype(BF16)
        qb = proj(base, base + SW_WIDTH)
        for j in range(SW_WIDTH // LANES):
            blk = _rope(qb[:, j * LANES:(j + 1) * LANES], cos, sin, first_half) * Q_SCALE
            qb_ref[0, rows, j * LANES:(j + 1) * LANES] = blk.astype(BF16)
        kv = proj(base + SW_WIDTH, base + SW_WIDTH + 2 * SW_KV_WIDTH)
        kvb_ref[0, rows, :LANES] = _rope(kv[:, :LANES], cos, sin, first_half).astype(BF16)
        kvb_ref[0, rows, LANES:] = kv[:, LANES:].astype(BF16)


def _in_proj(x, g, mod, w, cos, sin):
    bsz, s, d = x.shape
    n = w.shape[1]
    tm = IN_TM
    row = lambda b, t: (b, t, 0)
    outs = [jax.ShapeDtypeStruct((bsz, s, NA_WIDTH), BF16)] * 3 + [
        jax.ShapeDtypeStruct((bsz, s, SW_WIDTH), BF16),
        jax.ShapeDtypeStruct((bsz, s, 2 * SW_KV_WIDTH), BF16)]
    return pl.pallas_call(
        _in_proj_kernel,
        grid=(bsz, s // tm),
        in_specs=[pl.BlockSpec((1, tm, d), row),
                  pl.BlockSpec((1, d), lambda b, t: (0, 0)),
                  _mod_spec(MOD_SCALE_A, d),
                  _mod_spec(MOD_SHIFT_A, d),
                  pl.BlockSpec((d, n), lambda b, t: (0, 0)),
                  pl.BlockSpec((tm, LANES), lambda b, t: (t, 0)),
                  pl.BlockSpec((tm, LANES), lambda b, t: (t, 0))],
        out_specs=[pl.BlockSpec((1, tm, NA_WIDTH), row)] * 3 + [
            pl.BlockSpec((1, tm, SW_WIDTH), row),
            pl.BlockSpec((1, tm, 2 * SW_KV_WIDTH), row)],
        out_shape=outs,
        compiler_params=pltpu.CompilerParams(dimension_semantics=("arbitrary", "arbitrary"),
                                             vmem_limit_bytes=VMEM_LIMIT),
        name="in_proj",
    )(x, g, mod, mod, w, cos, sin)


def _na_kernel(q_ref, k_ref, v_ref, bias_ref, o_ref, *, rows):
    lo = _lo_mask(GRID_W)
    win = NA_ROWS_WIN * GRID_W
    ones = jnp.ones((win, LANES), BF16)

    def scores(r):
        rs = jnp.clip(r - NA_ROWS_WIN // 2, 0, rows - NA_ROWS_WIN)
        q = q_ref[0, pl.ds(pl.multiple_of(r * GRID_W, GRID_W), GRID_W), :]
        k0 = pl.multiple_of(rs * GRID_W, GRID_W)
        zero = jnp.zeros_like(q)
        qs = jnp.concatenate([jnp.where(lo, q, zero), jnp.where(lo, zero, q)], axis=0)
        return _nt_dot(qs, k_ref[0, pl.ds(k0, win), :]) + bias_ref[0, r - rs], k0

    def finish(r, s, k0):
        p = jnp.exp2(s - jnp.max(s, axis=-1, keepdims=True)).astype(BF16)
        oa = jnp.dot(p, jnp.concatenate([v_ref[0, pl.ds(k0, win), :], ones], axis=1), preferred_element_type=F32)
        o = oa[:, :LANES] * (1.0 / oa[:, LANES:])
        out = jnp.where(lo, o[:GRID_W], o[GRID_W:])
        o_ref[0, pl.ds(pl.multiple_of(r * GRID_W, GRID_W), GRID_W), :] = out.astype(o_ref.dtype)

    def body(g, carry):
        base = g * NA_GROUP
        staged = [scores(base + i) for i in range(NA_AHEAD)]
        for i in range(NA_GROUP):
            if i + NA_AHEAD < NA_GROUP:
                staged.append(scores(base + i + NA_AHEAD))
            finish(base + i, *staged[i])
        return carry

    lax.fori_loop(0, rows // NA_GROUP, body, 0)


def _na_attention(q, k, v, bias):
    bsz, s, _ = q.shape
    rows = s // GRID_W
    pairs = NA_WIDTH // LANES
    blk = pl.BlockSpec((1, s, LANES), lambda hp, b: (b, 0, hp))
    return pl.pallas_call(
        functools.partial(_na_kernel, rows=rows),
        grid=(pairs, bsz),
        in_specs=[blk, blk, blk,
                  pl.BlockSpec((1,) + bias.shape[1:], lambda hp, b: (hp, 0, 0, 0))],
        out_specs=blk,
        out_shape=jax.ShapeDtypeStruct((bsz, s, NA_WIDTH), BF16),
        compiler_params=pltpu.CompilerParams(dimension_semantics=("arbitrary", "arbitrary"),
                                             vmem_limit_bytes=VMEM_LIMIT),
        name="na_attn",
    )(q, k, v, bias)


def _na_bias_table(rpb, rows):
    assert rows >= 2 * NA_ROWS_WIN
    col = np.arange(GRID_W)
    cs = np.clip(col - NA_COLS // 2, 0, GRID_W - NA_COLS)
    col_ok = (col[None, :] >= cs[:, None]) & (col[None, :] < cs[:, None] + NA_COLS)
    dc = np.clip(col[None, :] - col[:, None] + NA_COLS - 1, 0, 2 * NA_COLS - 2)
    onehot = (np.arange(2 * NA_COLS - 1)[:, None] == dc.reshape(1, -1)).astype(np.float32)
    t = jnp.dot(rpb.reshape(-1, 2 * NA_COLS - 1), onehot, precision=lax.Precision.HIGHEST)
    t = jnp.where(col_ok[None, None], t.reshape(NA_HEADS, 2 * NA_ROWS_WIN - 1, GRID_W, GRID_W), NEG)
    b = jnp.stack([t[:, NA_ROWS_WIN - 1 - c:2 * NA_ROWS_WIN - 1 - c] for c in range(NA_ROWS_WIN)], axis=1)
    b = b.reshape(NA_HEADS // 2, 2, NA_ROWS_WIN, NA_ROWS_WIN, GRID_W, GRID_W)
    b = b.transpose(0, 2, 1, 4, 3, 5)
    return b.reshape(NA_HEADS // 2, NA_ROWS_WIN, 2 * GRID_W, NA_ROWS_WIN * GRID_W)


def _sw_kernel(sink_ref, q_ref, k_ref, v_ref, mask_ref, o_ref, *, nblocks):
    lo = _lo_mask(SW_BLOCK)
    zero = jnp.zeros((SW_BLOCK, LANES), q_ref.dtype)
    ones = jnp.ones((3 * SW_BLOCK, LANES), BF16)
    ri = lax.broadcasted_iota(jnp.int32, (2 * SW_BLOCK, LANES), 0)
    ci = lax.broadcasted_iota(jnp.int32, (2 * SW_BLOCK, LANES), 1)
    eye2 = jnp.where((ri == ci) | (ri == ci + SW_BLOCK), 1.0, 0.0).astype(BF16)
    npairs = SW_WIDTH // LANES

    def window(u):
        n = pl.program_id(1) * SW_STEP_BLOCKS + u
        start = pl.multiple_of(jnp.clip(n - 1, 0, nblocks - 3) * SW_BLOCK, SW_BLOCK)
        variant = jnp.where(n == 0, 0, jnp.where(n == nblocks - 1, 2, 1))
        return start, variant

    def scores(u, j):
        start, variant = window(u)
        qj = q_ref[0, u * SW_BLOCK:(u + 1) * SW_BLOCK, j * LANES:(j + 1) * LANES]
        qs = jnp.concatenate([jnp.where(lo, qj, zero), jnp.where(lo, zero, qj)], axis=0)
        return _nt_dot(jnp.concatenate([qs, eye2], axis=1),
                       jnp.concatenate([k_ref[0, pl.ds(start, 3 * SW_BLOCK), :], mask_ref[variant]], axis=1))

    def finish(u, j, s):
        start, _ = window(u)
        ps, sinks = [], []
        for half in range(2):
            sg = s[half * SW_BLOCK:(half + 1) * SW_BLOCK]
            sk = sink_ref[2 * j + half]
            m = jnp.maximum(jnp.max(sg, axis=-1, keepdims=True), sk)
            ps.append(jnp.exp2(sg - m).astype(BF16))
            sinks.append(jnp.exp2(sk - m))
        oa = jnp.dot(jnp.concatenate(ps, axis=0),
                     jnp.concatenate([v_ref[0, pl.ds(start, 3 * SW_BLOCK), :], ones], axis=1),
                     preferred_element_type=F32)
        o = oa[:, :LANES] * (1.0 / (oa[:, LANES:] + jnp.concatenate(sinks, axis=0)))
        out = jnp.where(lo, o[:SW_BLOCK], o[SW_BLOCK:])
        o_ref[0, u * SW_BLOCK:(u + 1) * SW_BLOCK, j * LANES:(j + 1) * LANES] = out.astype(o_ref.dtype)

    units = [(u, j) for u in range(SW_STEP_BLOCKS) for j in range(npairs)]
    staged = [scores(*units[i]) for i in range(SW_AHEAD)]
    for i, unit in enumerate(units):
        if i + SW_AHEAD < len(units):
            staged.append(scores(*units[i + SW_AHEAD]))
        finish(*unit, staged[i])
        staged[i] = None


def _sw_attention(sink, q, kv, mask):
    bsz, s, _ = q.shape
    nblocks = s // SW_BLOCK
    assert nblocks >= 3 and nblocks % SW_STEP_BLOCKS == 0
    rows = SW_STEP_BLOCKS * SW_BLOCK
    grid_spec = pltpu.PrefetchScalarGridSpec(
        num_scalar_prefetch=1,
        grid=(bsz, nblocks // SW_STEP_BLOCKS),
        in_specs=[pl.BlockSpec((1, rows, SW_WIDTH), lambda b, n, sk: (b, n, 0)),
                  pl.BlockSpec((1, s, LANES), lambda b, n, sk: (b, 0, 0)),
                  pl.BlockSpec((1, s, LANES), lambda b, n, sk: (b, 0, 1)),
                  pl.BlockSpec(mask.shape, lambda b, n, sk: (0, 0, 0))],
        out_specs=pl.BlockSpec((1, rows, SW_WIDTH), lambda b, n, sk: (b, n, 0)),
    )
    return pl.pallas_call(
        functools.partial(_sw_kernel, nblocks=nblocks),
        grid_spec=grid_spec,
        out_shape=jax.ShapeDtypeStruct((bsz, s, SW_WIDTH), BF16),
        compiler_params=pltpu.CompilerParams(dimension_semantics=("arbitrary", "arbitrary"),
                                             vmem_limit_bytes=VMEM_LIMIT),
        name="sw_attn",
    )(sink, q, kv, kv, mask)


def _sw_mask_table():
    i = np.arange(SW_BLOCK)[:, None]
    j = np.arange(3 * SW_BLOCK)[None, :]
    ok = np.stack([np.abs(i - j) <= SW_BLOCK,
                   np.abs(i + SW_BLOCK - j) <= SW_BLOCK,
                   np.abs(i + 2 * SW_BLOCK - j) <= SW_BLOCK])
    return jnp.asarray(np.where(ok, 0.0, NEG).transpose(0, 2, 1), BF16)


def _out_proj_kernel(x_ref, oa_ref, ob_ref, gna_ref, gsw_ref, w_ref, gate_ref, o_ref):
    oa = (_rms(oa_ref[0].astype(F32)) * gna_ref[...]).astype(BF16)
    ob = (_rms(ob_ref[0].astype(F32)) * gsw_ref[...]).astype(BF16)
    mix = (jnp.dot(oa, w_ref[:NA_WIDTH], preferred_element_type=F32)
           + jnp.dot(ob, w_ref[NA_WIDTH:], preferred_element_type=F32))
    o_ref[0] = x_ref[0] + gate_ref[0, 0] * mix


def _out_proj(x, oa, ob, gna, gsw, w, mod):
    bsz, s, d = x.shape
    tm = OUT_TM
    row = lambda b, t: (b, t, 0)
    return pl.pallas_call(
        _out_proj_kernel,
        grid=(bsz, s // tm),
        in_specs=[pl.BlockSpec((1, tm, d), row),
                  pl.BlockSpec((1, tm, NA_WIDTH), row),
                  pl.BlockSpec((1, tm, SW_WIDTH), row),
                  pl.BlockSpec((1, NA_WIDTH), lambda b, t: (0, 0)),
                  pl.BlockSpec((1, SW_WIDTH), lambda b, t: (0, 0)),
                  pl.BlockSpec(w.shape, lambda b, t: (0, 0)),
                  _mod_spec(MOD_GATE_A, d)],
        out_specs=pl.BlockSpec((1, tm, d), row),
        out_shape=jax.ShapeDtypeStruct((bsz, s, d), F32),
        compiler_params=pltpu.CompilerParams(dimension_semantics=("arbitrary", "arbitrary"),
                                             vmem_limit_bytes=VMEM_LIMIT),
        name="out_proj",
    )(x, oa, ob, gna, gsw, w, mod)


def _ffn_kernel(x_ref, g_ref, scale_ref, shift_ref, gate_ref, wval_ref, wgate_ref, cw_ref, cb_ref, wdn_ref,
                gfin_ref,
                o_ref, h_ref, *, row_chunk):
    f = pl.program_id(1)
    s = x_ref.shape[1]
    nchunks = s // row_chunk

    @pl.when(f == 0)
    def _():
        gm = g_ref[...]
        sc = 1.0 + scale_ref[0, 0]
        sh = shift_ref[0, 0]

        def body(i, carry):
            r0 = pl.multiple_of(i * row_chunk, row_chunk)
            xr = x_ref[0, pl.ds(r0, row_chunk), :]
            h_ref[pl.ds(r0, row_chunk), :] = ((_rms(xr) * gm) * sc + sh).astype(BF16)
            o_ref[0, pl.ds(r0, row_chunk), :] = jnp.zeros((row_chunk, xr.shape[1]), F32)
            return carry

        lax.fori_loop(0, nchunks, body, 0)

    nblk = s // FF_ROWS
    cw = cw_ref[...]
    cb = cb_ref[...]
    ridx = lax.broadcasted_iota(jnp.int32, (FF_ROWS, FF_CHUNK), 0)
    edge = jnp.zeros((1, FF_CHUNK), F32)

    def up(i):
        h = h_ref[i * FF_ROWS:(i + 1) * FF_ROWS, :]
        return (jnp.dot(h, wval_ref[...], preferred_element_type=F32),
                jnp.dot(h, wgate_ref[...], preferred_element_type=F32))

    def act(i, ups):
        val, gt = ups[i]
        before = ups[i - 1][1][FF_ROWS - 1:] if i > 0 else edge
        after = ups[i + 1][1][:1] if i < nblk - 1 else edge
        prev = jnp.where(ridx == 0, before, pltpu.roll(gt, 1, 0))
        nxt = jnp.where(ridx == FF_ROWS - 1, after, pltpu.roll(gt, FF_ROWS - 1, 0))
        gc = prev * cw[0:1] + gt * cw[1:2] + nxt * cw[2:3] + cb
        return (gc * jax.nn.sigmoid(gc) * val).astype(BF16)

    ups = {0: up(0), 1: up(1)}
    for i in range(nblk):
        a = act(i, ups)
        o_ref[0, i * FF_ROWS:(i + 1) * FF_ROWS, :] += jnp.dot(a, wdn_ref[...], preferred_element_type=F32)
        if i + 2 < nblk:
            ups[i + 2] = up(i + 2)

    @pl.when(f == pl.num_programs(1) - 1)
    def _():
        gate = gate_ref[0, 0]
        gfin = gfin_ref[...]

        def body(i, carry):
            r0 = pl.multiple_of(i * row_chunk, row_chunk)
            y = x_ref[0, pl.ds(r0, row_chunk), :] + gate * o_ref[0, pl.ds(r0, row_chunk), :]
            o_ref[0, pl.ds(r0, row_chunk), :] = _rms(y) * gfin
            return carry

        lax.fori_loop(0, nchunks, body, 0)


def _ffn(x1, g, mod, wup, cw, cb, wdn, gfin):
    bsz, s, d = x1.shape
    nf = wdn.shape[0] // FF_CHUNK
    return pl.pallas_call(
        functools.partial(_ffn_kernel, row_chunk=256),
        grid=(bsz, nf),
        in_specs=[pl.BlockSpec((1, s, d), lambda b, f: (b, 0, 0)),
                  pl.BlockSpec((1, d), lambda b, f: (0, 0)),
                  _mod_spec(MOD_SCALE_F, d),
                  _mod_spec(MOD_SHIFT_F, d),
                  _mod_spec(MOD_GATE_F, d),
                  pl.BlockSpec((d, FF_CHUNK), lambda b, f: (0, f)),
                  pl.BlockSpec((d, FF_CHUNK), lambda b, f: (0, nf + f)),
                  pl.BlockSpec((CONV_W, FF_CHUNK), lambda b, f: (0, f)),
                  pl.BlockSpec((1, FF_CHUNK), lambda b, f: (0, f)),
                  pl.BlockSpec((FF_CHUNK, d), lambda b, f: (f, 0)),
                  pl.BlockSpec((1, d), lambda b, f: (0, 0))],
        out_specs=pl.BlockSpec((1, s, d), lambda b, f: (b, 0, 0)),
        out_shape=jax.ShapeDtypeStruct((bsz, s, d), F32),
        scratch_shapes=[pltpu.VMEM((s, d), BF16)],
        compiler_params=pltpu.CompilerParams(dimension_semantics=("arbitrary", "arbitrary"),
                                             vmem_limit_bytes=VMEM_LIMIT),
        name="ffn",
    )(x1, g, mod, mod, mod, wup, wup, cw, cb, wdn, gfin)


def _rope_tables(s):
    half = HEAD_DIM // 2
    inv = ROPE_THETA ** (-jnp.arange(half, dtype=F32) / half)
    ang = jnp.arange(s).astype(F32)[:, None] * inv[None, :]
    cos = jnp.cos(ang)
    sin = jnp.sin(ang)
    reps = LANES // HEAD_DIM
    return (jnp.tile(jnp.concatenate([cos, cos], axis=-1), (1, reps)),
            jnp.tile(jnp.concatenate([-sin, sin], axis=-1), (1, reps)))


def _sw_head_order(t, axis):
    shape = t.shape
    t = t.reshape(shape[:axis] + (SW_KV_HEADS, SW_GROUP, -1) + shape[axis + 1:])
    return jnp.swapaxes(t, axis, axis + 1).reshape(shape)


def kernel(x, c, w_ada, b_ada, g_attn, w_in, na_rpb, sw_sink, g_na_out, g_sw_out, w_out, g_ffn, w_up,
           conv_w, conv_b, w_down, g_final):
    bsz, s, d = x.shape
    depth = w_ada.shape[0]
    d_ff = w_down.shape[1]
    assert depth == 1, "the final rmsnorm is fused into the (only) layer's ffn call"
    assert d_ff % FF_CHUNK == 0 and s % GRID_W == 0 and s % SW_BLOCK == 0
    cos, sin = _rope_tables(s)
    sw_mask = _sw_mask_table()
    qb0 = 3 * NA_WIDTH
    for l in range(depth):
        mod = _ada(c, w_ada[l], b_ada[l]).reshape(bsz, 6, 1, d)

        wi = w_in[l].astype(BF16)
        wi = jnp.concatenate([wi[:, :qb0], _sw_head_order(wi[:, qb0:qb0 + SW_WIDTH], 1),
                              wi[:, qb0 + SW_WIDTH:]], axis=1)
        qa, ka, va, qb, kvb = _in_proj(x, g_attn[l].reshape(1, d), mod, wi, cos, sin)

        o_a = _na_attention(qa, ka, va, _na_bias_table(na_rpb[l].astype(F32) * LOG2E, s // GRID_W))
        o_b = _sw_attention(_sw_head_order(sw_sink[l].astype(F32) * LOG2E, 0), qb, kvb, sw_mask)

        wo = w_out[l].astype(BF16)
        wo = jnp.concatenate([wo[:NA_WIDTH], _sw_head_order(wo[NA_WIDTH:], 0)], axis=0)
        x = _out_proj(x, o_a, o_b, g_na_out[l].reshape(1, -1), _sw_head_order(g_sw_out[l], 0).reshape(1, -1),
                      wo, mod)

        x = _ffn(x, g_ffn[l].reshape(1, d), mod, w_up[l].astype(BF16), conv_w[l], conv_b[l].reshape(1, d_ff),
                 w_down[l].astype(BF16), g_final.reshape(1, d))
    return x
```

```python
import functools

import jax
import jax.numpy as jnp
import numpy as np
from jax import lax
from jax.experimental import pallas as pl
from jax.experimental.pallas import tpu as pltpu

F32 = jnp.float32
BF16 = jnp.bfloat16

HEAD_DIM = 64
LANES = 128
NA_HEADS = 8
NA_WIDTH = NA_HEADS * HEAD_DIM
GRID_W = 64
NA_ROWS_WIN = 8
NA_COLS = 16
SW_HEADS = 8
SW_KV_HEADS = 2
SW_GROUP = SW_HEADS // SW_KV_HEADS
SW_WIDTH = SW_HEADS * HEAD_DIM
SW_KV_WIDTH = SW_KV_HEADS * HEAD_DIM
SW_BLOCK = 128
CONV_W = 3
ROPE_THETA = 10000.0
EPS = 1e-6
NEG = -1e30
LOG2E = 1.4426950408889634
Q_SCALE = HEAD_DIM ** -0.5 * LOG2E

VMEM_LIMIT = 56 * 1024 * 1024
FFN_VMEM_LIMIT = 62 * 1024 * 1024

IN_TM = 1024
IN_ROWS = 256
OUT_TM = 512
FF_CHUNK = 256
FF_ROWS = 512
NA_GROUP = 16
SW_STEP_BLOCKS = 4
SW_AHEAD = 2
NA_AHEAD = 3


def _rms(x):
    return x * lax.rsqrt(jnp.mean(x * x, axis=-1, keepdims=True) + EPS)


def _nt_dot(a, b):
    return lax.dot_general(a, b, (((1,), (1,)), ((), ())), preferred_element_type=F32)


def _lo_mask(rows):
    return lax.broadcasted_iota(jnp.int32, (rows, LANES), 1) < HEAD_DIM


def _ada_kernel(c_ref, w_ref, b_ref, o_ref):
    c = c_ref[...]
    sc = c * jax.nn.sigmoid(c)
    o_ref[...] = jnp.dot(sc, w_ref[...], preferred_element_type=F32,
                         precision=lax.Precision.HIGHEST) + b_ref[...]


def _ada(c, w, b):
    bsz, d = c.shape
    n = w.shape[1]
    tn = 1024
    return pl.pallas_call(
        _ada_kernel,
        grid=(n // tn,),
        in_specs=[pl.BlockSpec((bsz, d), lambda j: (0, 0)),
                  pl.BlockSpec((d, tn), lambda j: (0, j)),
                  pl.BlockSpec((1, tn), lambda j: (0, j))],
        out_specs=pl.BlockSpec((bsz, tn), lambda j: (0, j)),
        out_shape=jax.ShapeDtypeStruct((bsz, n), F32),
        compiler_params=pltpu.CompilerParams(dimension_semantics=("arbitrary",),
                                             vmem_limit_bytes=VMEM_LIMIT),
        name="ada",
    )(c, w, b.reshape(1, n))


MOD_SHIFT_A, MOD_SCALE_A, MOD_GATE_A, MOD_SHIFT_F, MOD_SCALE_F, MOD_GATE_F = range(6)


def _mod_spec(which, d):
    return pl.BlockSpec((1, 1, 1, d), lambda b, t: (b, which, 0, 0))


def _rope(t, cos, sin_signed, first_half):
    rot = jnp.where(first_half, pltpu.roll(t, LANES - HEAD_DIM // 2, 1), pltpu.roll(t, HEAD_DIM // 2, 1))
    return t * cos + rot * sin_signed


def _in_proj_kernel(x_ref, g_ref, scale_ref, shift_ref, w_ref, cos_ref, sin_ref,
                    qa_ref, ka_ref, va_ref, qb_ref, kvb_ref):
    gain = g_ref[...]
    scale = 1.0 + scale_ref[0, 0]
    shift = shift_ref[0, 0]
    first_half = (lax.broadcasted_iota(jnp.int32, (IN_ROWS, LANES), 1) % HEAD_DIM) < HEAD_DIM // 2
    base = 3 * NA_WIDTH

    for i in range(x_ref.shape[1] // IN_ROWS):
        rows = slice(i * IN_ROWS, (i + 1) * IN_ROWS)
        h = ((_rms(x_ref[0, rows, :]) * gain) * scale + shift).astype(BF16)
        cos = cos_ref[rows, :]
        sin = sin_ref[rows, :]

        def proj(lo, hi):
            return jnp.dot(h, w_ref[:, lo:hi], preferred_element_type=F32)

        qa_ref[0, rows, :] = (proj(0, NA_WIDTH) * Q_SCALE).astype(BF16)
        ka_ref[0, rows, :] = proj(NA_WIDTH, 2 * NA_WIDTH).astype(BF16)
        va_ref[0, rows, :] = proj(2 * NA_WIDTH, 3 * NA_WIDTH).astype(BF16)
        qb = proj(base, base + SW_WIDTH)
        for j in range(SW_WIDTH // LANES):
            blk = _rope(qb[:, j * LANES:(j + 1) * LANES], cos, sin, first_half) * Q_SCALE
            qb_ref[0, rows, j * LANES:(j + 1) * LANES] = blk.astype(BF16)
        kv = proj(base + SW_WIDTH, base + SW_WIDTH + 2 * SW_KV_WIDTH)
        kvb_ref[0, rows, :LANES] = _rope(kv[:, :LANES], cos, sin, first_half).astype(BF16)
        kvb_ref[0, rows, LANES:] = kv[:, LANES:].astype(BF16)


def _in_proj(x, g, mod, w, cos, sin):
    bsz, s, d = x.shape
    n = w.shape[1]
    tm = IN_TM
    row = lambda b, t: (b, t, 0)
    outs = [jax.ShapeDtypeStruct((bsz, s, NA_WIDTH), BF16)] * 3 + [
        jax.ShapeDtypeStruct((bsz, s, SW_WIDTH), BF16),
        jax.ShapeDtypeStruct((bsz, s, 2 * SW_KV_WIDTH), BF16)]
    return pl.pallas_call(
        _in_proj_kernel,
        grid=(bsz, s // tm),
        in_specs=[pl.BlockSpec((1, tm, d), row),
                  pl.BlockSpec((1, d), lambda b, t: (0, 0)),
                  _mod_spec(MOD_SCALE_A, d),
                  _mod_spec(MOD_SHIFT_A, d),
                  pl.BlockSpec((d, n), lambda b, t: (0, 0)),
                  pl.BlockSpec((tm, LANES), lambda b, t: (t, 0)),
                  pl.BlockSpec((tm, LANES), lambda b, t: (t, 0))],
        out_specs=[pl.BlockSpec((1, tm, NA_WIDTH), row)] * 3 + [
            pl.BlockSpec((1, tm, SW_WIDTH), row),
            pl.BlockSpec((1, tm, 2 * SW_KV_WIDTH), row)],
        out_shape=outs,
        compiler_params=pltpu.CompilerParams(dimension_semantics=("arbitrary", "arbitrary"),
                                             vmem_limit_bytes=VMEM_LIMIT),
        name="in_proj",
    )(x, g, mod, mod, w, cos, sin)


def _na_kernel(q_ref, k_ref, v_ref, bias_ref, o_ref, *, rows):
    lo = _lo_mask(GRID_W)
    win = NA_ROWS_WIN * GRID_W
    ones = jnp.ones((win, LANES), BF16)

    def scores(r):
        rs = jnp.clip(r - NA_ROWS_WIN // 2, 0, rows - NA_ROWS_WIN)
        q = q_ref[0, pl.ds(pl.multiple_of(r * GRID_W, GRID_W), GRID_W), :]
        k0 = pl.multiple_of(rs * GRID_W, GRID_W)
        zero = jnp.zeros_like(q)
        qs = jnp.concatenate([jnp.where(lo, q, zero), jnp.where(lo, zero, q)], axis=0)
        return _nt_dot(qs, k_ref[0, pl.ds(k0, win), :]) + bias_ref[0, r - rs], k0

    def finish(r, s, k0):
        p = jnp.exp2(s - jnp.max(s, axis=-1, keepdims=True)).astype(BF16)
        oa = jnp.dot(p, jnp.concatenate([v_ref[0, pl.ds(k0, win), :], ones], axis=1), preferred_element_type=F32)
        o = oa[:, :LANES] * (1.0 / oa[:, LANES:])
        out = jnp.where(lo, o[:GRID_W], o[GRID_W:])
        o_ref[0, pl.ds(pl.multiple_of(r * GRID_W, GRID_W), GRID_W), :] = out.astype(o_ref.dtype)

    def body(g, carry):
        base = g * NA_GROUP
        staged = [scores(base + i) for i in range(NA_AHEAD)]
        for i in range(NA_GROUP):
            if i + NA_AHEAD < NA_GROUP:
                staged.append(scores(base + i + NA_AHEAD))
            finish(base + i, *staged[i])
        return carry

    lax.fori_loop(0, rows // NA_GROUP, body, 0)


def _na_attention(q, k, v, bias):
    bsz, s, _ = q.shape
    rows = s // GRID_W
    pairs = NA_WIDTH // LANES
    blk = pl.BlockSpec((1, s, LANES), lambda hp, b: (b, 0, hp))
    return pl.pallas_call(
        functools.partial(_na_kernel, rows=rows),
        grid=(pairs, bsz),
        in_specs=[blk, blk, blk,
                  pl.BlockSpec((1,) + bias.shape[1:], lambda hp, b: (hp, 0, 0, 0))],
        out_specs=blk,
        out_shape=jax.ShapeDtypeStruct((bsz, s, NA_WIDTH), BF16),
        compiler_params=pltpu.CompilerParams(dimension_semantics=("arbitrary", "arbitrary"),
                                             vmem_limit_bytes=VMEM_LIMIT),
        name="na_attn",
    )(q, k, v, bias)


def _na_bias_table(rpb, rows):
    assert rows >= 2 * NA_ROWS_WIN
    col = np.arange(GRID_W)
    cs = np.clip(col - NA_COLS // 2, 0, GRID_W - NA_COLS)
    col_ok = (col[None, :] >= cs[:, None]) & (col[None, :] < cs[:, None] + NA_COLS)
    dc = np.clip(col[None, :] - col[:, None] + NA_COLS - 1, 0, 2 * NA_COLS - 2)
    onehot = (np.arange(2 * NA_COLS - 1)[:, None] == dc.reshape(1, -1)).astype(np.float32)
    t = jnp.dot(rpb.reshape(-1, 2 * NA_COLS - 1), onehot, precision=lax.Precision.HIGHEST)
    t = jnp.where(col_ok[None, None], t.reshape(NA_HEADS, 2 * NA_ROWS_WIN - 1, GRID_W, GRID_W), NEG)
    b = jnp.stack([t[:, NA_ROWS_WIN - 1 - c:2 * NA_ROWS_WIN - 1 - c] for c in range(NA_ROWS_WIN)], axis=1)
    b = b.reshape(NA_HEADS // 2, 2, NA_ROWS_WIN, NA_ROWS_WIN, GRID_W, GRID_W)
    b = b.transpose(0, 2, 1, 4, 3, 5)
    return b.reshape(NA_HEADS // 2, NA_ROWS_WIN, 2 * GRID_W, NA_ROWS_WIN * GRID_W)


def _sw_kernel(sink_ref, q_ref, k_ref, v_ref, mask_ref, o_ref, *, nblocks):
    lo = _lo_mask(SW_BLOCK)
    zero = jnp.zeros((SW_BLOCK, LANES), q_ref.dtype)
    ones = jnp.ones((3 * SW_BLOCK, LANES), BF16)
    ri = lax.broadcasted_iota(jnp.int32, (2 * SW_BLOCK, LANES), 0)
    ci = lax.broadcasted_iota(jnp.int32, (2 * SW_BLOCK, LANES), 1)
    eye2 = jnp.where((ri == ci) | (ri == ci + SW_BLOCK), 1.0, 0.0).astype(BF16)
    npairs = SW_WIDTH // LANES

    def window(u):
        n = pl.program_id(1) * SW_STEP_BLOCKS + u
        start = pl.multiple_of(jnp.clip(n - 1, 0, nblocks - 3) * SW_BLOCK, SW_BLOCK)
        variant = jnp.where(n == 0, 0, jnp.where(n == nblocks - 1, 2, 1))
        return start, variant

    def scores(u, j):
        start, variant = window(u)
        qj = q_ref[0, u * SW_BLOCK:(u + 1) * SW_BLOCK, j * LANES:(j + 1) * LANES]
        qs = jnp.concatenate([jnp.where(lo, qj, zero), jnp.where(lo, zero, qj)], axis=0)
        return _nt_dot(jnp.concatenate([qs, eye2], axis=1),
                       jnp.concatenate([k_ref[0, pl.ds(start, 3 * SW_BLOCK), :], mask_ref[variant]], axis=1))

    def finish(u, j, s):
        start, _ = window(u)
        ps, sinks = [], []
        for half in range(2):
            sg = s[half * SW_BLOCK:(half + 1) * SW_BLOCK]
            sk = sink_ref[2 * j + half]
            m = jnp.maximum(jnp.max(sg, axis=-1, keepdims=True), sk)
            ps.append(jnp.exp2(sg - m).astype(BF16))
            sinks.append(jnp.exp2(sk - m))
        oa = jnp.dot(jnp.concatenate(ps, axis=0),
                     jnp.concatenate([v_ref[0, pl.ds(start, 3 * SW_BLOCK), :], ones], axis=1),
                     preferred_element_type=F32)
        o = oa[:, :LANES] * (1.0 / (oa[:, LANES:] + jnp.concatenate(sinks, axis=0)))
        out = jnp.where(lo, o[:SW_BLOCK], o[SW_BLOCK:])
        o_ref[0, u * SW_BLOCK:(u + 1) * SW_BLOCK, j * LANES:(j + 1) * LANES] = out.astype(o_ref.dtype)

    units = [(u, j) for u in range(SW_STEP_BLOCKS) for j in range(npairs)]
    staged = [scores(*units[i]) for i in range(SW_AHEAD)]
    for i, unit in enumerate(units):
        if i + SW_AHEAD < len(units):
            staged.append(scores(*units[i + SW_AHEAD]))
        finish(*unit, staged[i])
        staged[i] = None


def _sw_attention(sink, q, kv, mask):
    bsz, s, _ = q.shape
    nblocks = s // SW_BLOCK
    assert nblocks >= 3 and nblocks % SW_STEP_BLOCKS == 0
    rows = SW_STEP_BLOCKS * SW_BLOCK
    grid_spec = pltpu.PrefetchScalarGridSpec(
        num_scalar_prefetch=1,
        grid=(bsz, nblocks // SW_STEP_BLOCKS),
        in_specs=[pl.BlockSpec((1, rows, SW_WIDTH), lambda b, n, sk: (b, n, 0)),
                  pl.BlockSpec((1, s, LANES), lambda b, n, sk: (b, 0, 0)),
                  pl.BlockSpec((1, s, LANES), lambda b, n, sk: (b, 0, 1)),
                  pl.BlockSpec(mask.shape, lambda b, n, sk: (0, 0, 0))],
        out_specs=pl.BlockSpec((1, rows, SW_WIDTH), lambda b, n, sk: (b, n, 0)),
    )
    return pl.pallas_call(
        functools.partial(_sw_kernel, nblocks=nblocks),
        grid_spec=grid_spec,
        out_shape=jax.ShapeDtypeStruct((bsz, s, SW_WIDTH), BF16),
        compiler_params=pltpu.CompilerParams(dimension_semantics=("arbitrary", "arbitrary"),
                                             vmem_limit_bytes=VMEM_LIMIT),
        name="sw_attn",
    )(sink, q, kv, kv, mask)


def _sw_mask_table():
    i = np.arange(SW_BLOCK)[:, None]
    j = np.arange(3 * SW_BLOCK)[None, :]
    ok = np.stack([np.abs(i - j) <= SW_BLOCK,
                   np.abs(i + SW_BLOCK - j) <= SW_BLOCK,
                   np.abs(i + 2 * SW_BLOCK - j) <= SW_BLOCK])
    return jnp.asarray(np.where(ok, 0.0, NEG).transpose(0, 2, 1), BF16)


def _out_proj_kernel(x_ref, oa_ref, ob_ref, gna_ref, gsw_ref, w_ref, gate_ref, o_ref):
    oa = (_rms(oa_ref[0].astype(F32)) * gna_ref[...]).astype(BF16)
    ob = (_rms(ob_ref[0].astype(F32)) * gsw_ref[...]).astype(BF16)
    mix = (jnp.dot(oa, w_ref[:NA_WIDTH], preferred_element_type=F32)
           + jnp.dot(ob, w_ref[NA_WIDTH:], preferred_element_type=F32))
    o_ref[0] = x_ref[0] + gate_ref[0, 0] * mix


def _out_proj(x, oa, ob, gna, gsw, w, mod):
    bsz, s, d = x.shape
    tm = OUT_TM
    row = lambda b, t: (b, t, 0)
    return pl.pallas_call(
        _out_proj_kernel,
        grid=(bsz, s // tm),
        in_specs=[pl.BlockSpec((1, tm, d), row),
                  pl.BlockSpec((1, tm, NA_WIDTH), row),
                  pl.BlockSpec((1, tm, SW_WIDTH), row),
                  pl.BlockSpec((1, NA_WIDTH), lambda b, t: (0, 0)),
                  pl.BlockSpec((1, SW_WIDTH), lambda b, t: (0, 0)),
                  pl.BlockSpec(w.shape, lambda b, t: (0, 0)),
                  _mod_spec(MOD_GATE_A, d)],
        out_specs=pl.BlockSpec((1, tm, d), row),
        out_shape=jax.ShapeDtypeStruct((bsz, s, d), F32),
        compiler_params=pltpu.CompilerParams(dimension_semantics=("arbitrary", "arbitrary"),
                                             vmem_limit_bytes=VMEM_LIMIT),
        name="out_proj",
    )(x, oa, ob, gna, gsw, w, mod)


def _ffn_kernel(x_ref, g_ref, scale_ref, shift_ref, gate_ref, wup_ref, cw_ref, cb_ref, wdn_ref, gfin_ref,
                o_ref, h_ref, *, row_chunk):
    s = x_ref.shape[1]
    nf = cw_ref.shape[0]
    nblk = s // FF_ROWS

    def prologue(i, carry):
        r0 = pl.multiple_of(i * row_chunk, row_chunk)
        xr = x_ref[0, pl.ds(r0, row_chunk), :]
        h = (_rms(xr) * g_ref[...]) * (1.0 + scale_ref[0, 0]) + shift_ref[0, 0]
        h_ref[pl.ds(r0, row_chunk), :] = h.astype(BF16)
        o_ref[0, pl.ds(r0, row_chunk), :] = jnp.zeros_like(xr)
        return carry

    lax.fori_loop(0, s // row_chunk, prologue, 0)

    ridx = lax.broadcasted_iota(jnp.int32, (FF_ROWS, FF_CHUNK), 0)
    edge = jnp.zeros((1, FF_CHUNK), F32)

    def rows(i):
        return slice(i * FF_ROWS, (i + 1) * FF_ROWS)

    def up(c, i):
        h = h_ref[rows(i), :]
        return (jnp.dot(h, wup_ref[c], preferred_element_type=F32),
                jnp.dot(h, wup_ref[nf + c], preferred_element_type=F32))

    def act(c, i, cur, before, after):
        val, gt = cur
        before = before[1][FF_ROWS - 1:] if i > 0 else edge
        after = after[1][:1] if i < nblk - 1 else edge
        prev = jnp.where(ridx == 0, before, pltpu.roll(gt, 1, 0))
        nxt = jnp.where(ridx == FF_ROWS - 1, after, pltpu.roll(gt, FF_ROWS - 1, 0))
        cw = cw_ref[c]
        gc = prev * cw[0:1] + gt * cw[1:2] + nxt * cw[2:3] + cb_ref[c]
        return (gc * jax.nn.sigmoid(gc) * val).astype(BF16)

    def run(chunks, last):
        units = [(c, i) for c in chunks for i in range(nblk)]
        ups = {0: up(*units[0]), 1: up(*units[1])}
        for u, (c, i) in enumerate(units):
            a = act(c, i, ups[u], ups.get(u - 1), ups.get(u + 1))
            w = wdn_ref[pl.ds(pl.multiple_of(c * FF_CHUNK, FF_CHUNK), FF_CHUNK), :]
            acc = o_ref[0, rows(i), :] + jnp.dot(a, w, preferred_element_type=F32)
            if last and u >= len(units) - nblk:
                y = x_ref[0, rows(i), :] + gate_ref[0, 0] * acc
                acc = _rms(y) * gfin_ref[...]
            o_ref[0, rows(i), :] = acc
            ups.pop(u - 1, None)
            if u + 2 < len(units):
                ups[u + 2] = up(*units[u + 2])

    def pair(p, carry):
        run([2 * p, 2 * p + 1], False)
        return carry

    lax.fori_loop(0, (nf - 1) // 2, pair, 0)
    run(list(range(2 * ((nf - 1) // 2), nf)), True)


def _ffn(x1, g, mod, wup, cw, cb, wdn, gfin):
    bsz, s, d = x1.shape
    whole = lambda a: pl.BlockSpec(a.shape, lambda b: (0,) * a.ndim, pipeline_mode=pl.Buffered(1))
    vec = lambda which: pl.BlockSpec((1, 1, 1, d), lambda b: (b, which, 0, 0))
    return pl.pallas_call(
        functools.partial(_ffn_kernel, row_chunk=256),
        grid=(bsz,),
        in_specs=[pl.BlockSpec((1, s, d), lambda b: (b, 0, 0)),
                  whole(g), vec(MOD_SCALE_F), vec(MOD_SHIFT_F), vec(MOD_GATE_F),
                  whole(wup), whole(cw), whole(cb), whole(wdn), whole(gfin)],
        out_specs=pl.BlockSpec((1, s, d), lambda b: (b, 0, 0)),
        out_shape=jax.ShapeDtypeStruct((bsz, s, d), F32),
        scratch_shapes=[pltpu.VMEM((s, d), BF16)],
        compiler_params=pltpu.CompilerParams(dimension_semantics=("arbitrary",),
                                             vmem_limit_bytes=FFN_VMEM_LIMIT),
        name="ffn",
    )(x1, g, mod, mod, mod, wup, cw, cb, wdn, gfin)


def _rope_tables(s):
    half = HEAD_DIM // 2
    inv = ROPE_THETA ** (-jnp.arange(half, dtype=F32) / half)
    ang = jnp.arange(s).astype(F32)[:, None] * inv[None, :]
    cos = jnp.cos(ang)
    sin = jnp.sin(ang)
    reps = LANES // HEAD_DIM
    return (jnp.tile(jnp.concatenate([cos, cos], axis=-1), (1, reps)),
            jnp.tile(jnp.concatenate([-sin, sin], axis=-1), (1, reps)))


def _sw_head_order(t, axis):
    shape = t.shape
    t = t.reshape(shape[:axis] + (SW_KV_HEADS, SW_GROUP, -1) + shape[axis + 1:])
    return jnp.swapaxes(t, axis, axis + 1).reshape(shape)


def kernel(x, c, w_ada, b_ada, g_attn, w_in, na_rpb, sw_sink, g_na_out, g_sw_out, w_out, g_ffn, w_up,
           conv_w, conv_b, w_down, g_final):
    bsz, s, d = x.shape
    depth = w_ada.shape[0]
    d_ff = w_down.shape[1]
    assert depth == 1, "the final rmsnorm is fused into the (only) layer's ffn call"
    assert d_ff % FF_CHUNK == 0 and s % GRID_W == 0 and s % SW_BLOCK == 0
    cos, sin = _rope_tables(s)
    sw_mask = _sw_mask_table()
    qb0 = 3 * NA_WIDTH
    for l in range(depth):
        mod = _ada(c, w_ada[l], b_ada[l]).reshape(bsz, 6, 1, d)

        wi = w_in[l].astype(BF16)
        wi = jnp.concatenate([wi[:, :qb0], _sw_head_order(wi[:, qb0:qb0 + SW_WIDTH], 1),
                              wi[:, qb0 + SW_WIDTH:]], axis=1)
        qa, ka, va, qb, kvb = _in_proj(x, g_attn[l].reshape(1, d), mod, wi, cos, sin)

        o_a = _na_attention(qa, ka, va, _na_bias_table(na_rpb[l].astype(F32) * LOG2E, s // GRID_W))
        o_b = _sw_attention(_sw_head_order(sw_sink[l].astype(F32) * LOG2E, 0), qb, kvb, sw_mask)

        wo = w_out[l].astype(BF16)
        wo = jnp.concatenate([wo[:NA_WIDTH], _sw_head_order(wo[NA_WIDTH:], 0)], axis=0)
        x = _out_proj(x, o_a, o_b, g_na_out[l].reshape(1, -1), _sw_head_order(g_sw_out[l], 0).reshape(1, -1),
                      wo, mod)

        nf = d_ff // FF_CHUNK
        wu = w_up[l].astype(BF16).reshape(d, 2 * nf, FF_CHUNK).transpose(1, 0, 2)
        cw = conv_w[l].reshape(CONV_W, nf, FF_CHUNK).transpose(1, 0, 2)
        x = _ffn(x, g_ffn[l].reshape(1, d), mod, wu, cw, conv_b[l].reshape(nf, 1, FF_CHUNK),
                 w_down[l].astype(BF16), g_final.reshape(1, d))
    return x
```

```python
import functools

import jax
import jax.numpy as jnp
import numpy as np
from jax import lax
from jax.experimental import pallas as pl
from jax.experimental.pallas import tpu as pltpu

F32 = jnp.float32
BF16 = jnp.bfloat16

HEAD_DIM = 64
LANES = 128
NA_HEADS = 8
NA_WIDTH = NA_HEADS * HEAD_DIM
GRID_W = 64
NA_ROWS_WIN = 8
NA_COLS = 16
SW_HEADS = 8
SW_KV_HEADS = 2
SW_GROUP = SW_HEADS // SW_KV_HEADS
SW_WIDTH = SW_HEADS * HEAD_DIM
SW_KV_WIDTH = SW_KV_HEADS * HEAD_DIM
SW_BLOCK = 128
CONV_W = 3
ROPE_THETA = 10000.0
EPS = 1e-6
NEG = -1e30
LOG2E = 1.4426950408889634
Q_SCALE = HEAD_DIM ** -0.5 * LOG2E

VMEM_LIMIT = 56 * 1024 * 1024
FFN_VMEM_LIMIT = 62 * 1024 * 1024

IN_TM = 1024
IN_ROWS = 256
OUT_TM = 1024
FF_CHUNK = 256
FF_ROWS = 512
NA_GROUP = 32
SW_STEP_BLOCKS = 8
SW_AHEAD = 2
NA_AHEAD = 3


def _rms(x):
    return x * lax.rsqrt(jnp.mean(x * x, axis=-1, keepdims=True) + EPS)


def _nt_dot(a, b):
    return lax.dot_general(a, b, (((1,), (1,)), ((), ())), preferred_element_type=F32)


def _lo_mask(rows):
    return lax.broadcasted_iota(jnp.int32, (rows, LANES), 1) < HEAD_DIM


def _ada_kernel(c_ref, w_ref, b_ref, o_ref):
    c = c_ref[...]
    sc = c * jax.nn.sigmoid(c)
    o_ref[...] = jnp.dot(sc, w_ref[...], preferred_element_type=F32,
                         precision=lax.Precision.HIGHEST) + b_ref[...]


def _ada(c, w, b):
    bsz, d = c.shape
    n = w.shape[1]
    tn = 1024
    return pl.pallas_call(
        _ada_kernel,
        grid=(n // tn,),
        in_specs=[pl.BlockSpec((bsz, d), lambda j: (0, 0)),
                  pl.BlockSpec((d, tn), lambda j: (0, j)),
                  pl.BlockSpec((1, tn), lambda j: (0, j))],
        out_specs=pl.BlockSpec((bsz, tn), lambda j: (0, j)),
        out_shape=jax.ShapeDtypeStruct((bsz, n), F32),
        compiler_params=pltpu.CompilerParams(dimension_semantics=("arbitrary",),
                                             vmem_limit_bytes=VMEM_LIMIT),
        name="ada",
    )(c, w, b.reshape(1, n))


MOD_SHIFT_A, MOD_SCALE_A, MOD_GATE_A, MOD_SHIFT_F, MOD_SCALE_F, MOD_GATE_F = range(6)


def _mod_spec(which, d):
    return pl.BlockSpec((1, 1, 1, d), lambda b, t: (b, which, 0, 0))


def _rope(t, cos, sin_signed, first_half):
    rot = jnp.where(first_half, pltpu.roll(t, LANES - HEAD_DIM // 2, 1), pltpu.roll(t, HEAD_DIM // 2, 1))
    return t * cos + rot * sin_signed


def _in_proj_kernel(x_ref, g_ref, scale_ref, shift_ref, w_ref, cos_ref, sin_ref,
                    qa_ref, ka_ref, va_ref, qb_ref, kvb_ref):
    gain = g_ref[...]
    scale = 1.0 + scale_ref[0, 0]
    shift = shift_ref[0, 0]
    first_half = (lax.broadcasted_iota(jnp.int32, (IN_ROWS, LANES), 1) % HEAD_DIM) < HEAD_DIM // 2
    base = 3 * NA_WIDTH

    for i in range(x_ref.shape[1] // IN_ROWS):
        rows = slice(i * IN_ROWS, (i + 1) * IN_ROWS)
        h = ((_rms(x_ref[0, rows, :]) * gain) * scale + shift).astype(BF16)
        cos = cos_ref[rows, :]
        sin = sin_ref[rows, :]

        def proj(lo, hi):
            return jnp.dot(h, w_ref[:, lo:hi], preferred_element_type=F32)

        qa_ref[0, rows, :] = (proj(0, NA_WIDTH) * Q_SCALE).astype(BF16)
        ka_ref[0, rows, :] = proj(NA_WIDTH, 2 * NA_WIDTH).astype(BF16)
        va_ref[0, rows, :] = proj(2 * NA_WIDTH, 3 * NA_WIDTH).astype(BF16)
        qb = proj(base, base + SW_WIDTH)
        for j in range(SW_WIDTH // LANES):
            blk = _rope(qb[:, j * LANES:(j + 1) * LANES], cos, sin, first_half) * Q_SCALE
            qb_ref[0, rows, j * LANES:(j + 1) * LANES] = blk.astype(BF16)
        kv = proj(base + SW_WIDTH, base + SW_WIDTH + 2 * SW_KV_WIDTH)
        kvb_ref[0, rows, :LANES] = _rope(kv[:, :LANES], cos, sin, first_half).astype(BF16)
        kvb_ref[0, rows, LANES:] = kv[:, LANES:].astype(BF16)


def _in_proj(x, g, mod, w, cos, sin):
    bsz, s, d = x.shape
    n = w.shape[1]
    tm = IN_TM
    row = lambda b, t: (b, t, 0)
    outs = [jax.ShapeDtypeStruct((bsz, s, NA_WIDTH), BF16)] * 3 + [
        jax.ShapeDtypeStruct((bsz, s, SW_WIDTH), BF16),
        jax.ShapeDtypeStruct((bsz, s, 2 * SW_KV_WIDTH), BF16)]
    return pl.pallas_call(
        _in_proj_kernel,
        grid=(bsz, s // tm),
        in_specs=[pl.BlockSpec((1, tm, d), row),
                  pl.BlockSpec((1, d), lambda b, t: (0, 0)),
                  _mod_spec(MOD_SCALE_A, d),
                  _mod_spec(MOD_SHIFT_A, d),
                  pl.BlockSpec((d, n), lambda b, t: (0, 0)),
                  pl.BlockSpec((tm, LANES), lambda b, t: (t, 0)),
                  pl.BlockSpec((tm, LANES), lambda b, t: (t, 0))],
        out_specs=[pl.BlockSpec((1, tm, NA_WIDTH), row)] * 3 + [
            pl.BlockSpec((1, tm, SW_WIDTH), row),
            pl.BlockSpec((1, tm, 2 * SW_KV_WIDTH), row)],
        out_shape=outs,
        compiler_params=pltpu.CompilerParams(dimension_semantics=("arbitrary", "arbitrary"),
                                             vmem_limit_bytes=VMEM_LIMIT),
        name="in_proj",
    )(x, g, mod, mod, w, cos, sin)


def _na_kernel(q_ref, k_ref, v_ref, bias_ref, o_ref, *, rows):
    lo = _lo_mask(GRID_W)
    win = NA_ROWS_WIN * GRID_W
    ones = jnp.ones((win, LANES), BF16)

    def scores(r):
        rs = jnp.clip(r - NA_ROWS_WIN // 2, 0, rows - NA_ROWS_WIN)
        q = q_ref[0, pl.ds(pl.multiple_of(r * GRID_W, GRID_W), GRID_W), :]
        k0 = pl.multiple_of(rs * GRID_W, GRID_W)
        zero = jnp.zeros_like(q)
        qs = jnp.concatenate([jnp.where(lo, q, zero), jnp.where(lo, zero, q)], axis=0)
        dr0 = NA_ROWS_WIN - 1 - (r - rs)
        bias = jnp.concatenate(
            [jnp.concatenate([bias_ref[head, dr0 + 2 * jj] for jj in range(NA_ROWS_WIN // 2)], axis=1)
             for head in range(2)], axis=0)
        return _nt_dot(qs, k_ref[0, pl.ds(k0, win), :]) + bias, k0

    def finish(r, s, k0):
        p = jnp.exp2(s - jnp.max(s, axis=-1, keepdims=True)).astype(BF16)
        oa = jnp.dot(p, jnp.concatenate([v_ref[0, pl.ds(k0, win), :], ones], axis=1), preferred_element_type=F32)
        o = oa[:, :LANES] * (1.0 / oa[:, LANES:])
        out = jnp.where(lo, o[:GRID_W], o[GRID_W:])
        o_ref[0, pl.ds(pl.multiple_of(r * GRID_W, GRID_W), GRID_W), :] = out.astype(o_ref.dtype)

    def body(g, carry):
        base = g * NA_GROUP
        staged = [scores(base + i) for i in range(NA_AHEAD)]
        for i in range(NA_GROUP):
            if i + NA_AHEAD < NA_GROUP:
                staged.append(scores(base + i + NA_AHEAD))
            finish(base + i, *staged[i])
        return carry

    lax.fori_loop(0, rows // NA_GROUP, body, 0)


def _na_attention(q, k, v, bias):
    bsz, s, _ = q.shape
    rows = s // GRID_W
    pairs = NA_WIDTH // LANES
    blk = pl.BlockSpec((1, s, LANES), lambda hp, b: (b, 0, hp))
    return pl.pallas_call(
        functools.partial(_na_kernel, rows=rows),
        grid=(pairs, bsz),
        in_specs=[blk, blk, blk,
                  pl.BlockSpec((2,) + bias.shape[1:], lambda hp, b: (hp, 0, 0, 0))],
        out_specs=blk,
        out_shape=jax.ShapeDtypeStruct((bsz, s, NA_WIDTH), BF16),
        compiler_params=pltpu.CompilerParams(dimension_semantics=("arbitrary", "arbitrary"),
                                             vmem_limit_bytes=VMEM_LIMIT),
        name="na_attn",
    )(q, k, v, bias)


def _na_bias_table(rpb, rows):
    assert rows >= 2 * NA_ROWS_WIN
    col = np.arange(GRID_W)
    cs = np.clip(col - NA_COLS // 2, 0, GRID_W - NA_COLS)
    col_ok = (col[None, :] >= cs[:, None]) & (col[None, :] < cs[:, None] + NA_COLS)
    dc = np.clip(col[None, :] - col[:, None] + NA_COLS - 1, 0, 2 * NA_COLS - 2)
    onehot = (np.arange(2 * NA_COLS - 1)[:, None] == dc.reshape(1, -1)).astype(np.float32)
    t = jnp.dot(rpb.reshape(-1, 2 * NA_COLS - 1), onehot, precision=lax.Precision.HIGHEST)
    t = jnp.where(col_ok[None, None], t.reshape(NA_HEADS, 2 * NA_ROWS_WIN - 1, GRID_W, GRID_W), NEG)
    return jnp.concatenate([t[:, :-1], t[:, 1:]], axis=-1)


def _sw_kernel(sink_ref, q_ref, k_ref, v_ref, mask_ref, o_ref, *, nblocks):
    lo = _lo_mask(SW_BLOCK)
    zero = jnp.zeros((SW_BLOCK, LANES), q_ref.dtype)
    ones = jnp.ones((3 * SW_BLOCK, LANES), BF16)
    ri = lax.broadcasted_iota(jnp.int32, (2 * SW_BLOCK, LANES), 0)
    ci = lax.broadcasted_iota(jnp.int32, (2 * SW_BLOCK, LANES), 1)
    eye2 = jnp.where((ri == ci) | (ri == ci + SW_BLOCK), 1.0, 0.0).astype(BF16)
    npairs = SW_WIDTH // LANES

    def window(u):
        n = pl.program_id(1) * SW_STEP_BLOCKS + u
        start = pl.multiple_of(jnp.clip(n - 1, 0, nblocks - 3) * SW_BLOCK, SW_BLOCK)
        variant = jnp.where(n == 0, 0, jnp.where(n == nblocks - 1, 2, 1))
        return start, variant

    def scores(u, j):
        start, variant = window(u)
        qj = q_ref[0, u * SW_BLOCK:(u + 1) * SW_BLOCK, j * LANES:(j + 1) * LANES]
        qs = jnp.concatenate([jnp.where(lo, qj, zero), jnp.where(lo, zero, qj)], axis=0)
        return _nt_dot(jnp.concatenate([qs, eye2], axis=1),
                       jnp.concatenate([k_ref[0, pl.ds(start, 3 * SW_BLOCK), :], mask_ref[variant]], axis=1))

    def finish(u, j, s):
        start, _ = window(u)
        ps, sinks = [], []
        for half in range(2):
            sg = s[half * SW_BLOCK:(half + 1) * SW_BLOCK]
            sk = sink_ref[2 * j + half]
            m = jnp.maximum(jnp.max(sg, axis=-1, keepdims=True), sk)
            ps.append(jnp.exp2(sg - m).astype(BF16))
            sinks.append(jnp.exp2(sk - m))
        oa = jnp.dot(jnp.concatenate(ps, axis=0),
                     jnp.concatenate([v_ref[0, pl.ds(start, 3 * SW_BLOCK), :], ones], axis=1),
                     preferred_element_type=F32)
        o = oa[:, :LANES] * (1.0 / (oa[:, LANES:] + jnp.concatenate(sinks, axis=0)))
        out = jnp.where(lo, o[:SW_BLOCK], o[SW_BLOCK:])
        o_ref[0, u * SW_BLOCK:(u + 1) * SW_BLOCK, j * LANES:(j + 1) * LANES] = out.astype(o_ref.dtype)

    units = [(u, j) for u in range(SW_STEP_BLOCKS) for j in range(npairs)]
    staged = [scores(*units[i]) for i in range(SW_AHEAD)]
    for i, unit in enumerate(units):
        if i + SW_AHEAD < len(units):
            staged.append(scores(*units[i + SW_AHEAD]))
        finish(*unit, staged[i])
        staged[i] = None


def _sw_attention(sink, q, kv, mask):
    bsz, s, _ = q.shape
    nblocks = s // SW_BLOCK
    assert nblocks >= 3 and nblocks % SW_STEP_BLOCKS == 0
    rows = SW_STEP_BLOCKS * SW_BLOCK
    grid_spec = pltpu.PrefetchScalarGridSpec(
        num_scalar_prefetch=1,
        grid=(bsz, nblocks // SW_STEP_BLOCKS),
        in_specs=[pl.BlockSpec((1, rows, SW_WIDTH), lambda b, n, sk: (b, n, 0)),
                  pl.BlockSpec((1, s, LANES), lambda b, n, sk: (b, 0, 0)),
                  pl.BlockSpec((1, s, LANES), lambda b, n, sk: (b, 0, 1)),
                  pl.BlockSpec(mask.shape, lambda b, n, sk: (0, 0, 0))],
        out_specs=pl.BlockSpec((1, rows, SW_WIDTH), lambda b, n, sk: (b, n, 0)),
    )
    return pl.pallas_call(
        functools.partial(_sw_kernel, nblocks=nblocks),
        grid_spec=grid_spec,
        out_shape=jax.ShapeDtypeStruct((bsz, s, SW_WIDTH), BF16),
        compiler_params=pltpu.CompilerParams(dimension_semantics=("arbitrary", "arbitrary"),
                                             vmem_limit_bytes=VMEM_LIMIT),
        name="sw_attn",
    )(sink, q, kv, kv, mask)


def _sw_mask_table():
    i = np.arange(SW_BLOCK)[:, None]
    j = np.arange(3 * SW_BLOCK)[None, :]
    ok = np.stack([np.abs(i - j) <= SW_BLOCK,
                   np.abs(i + SW_BLOCK - j) <= SW_BLOCK,
                   np.abs(i + 2 * SW_BLOCK - j) <= SW_BLOCK])
    return jnp.asarray(np.where(ok, 0.0, NEG).transpose(0, 2, 1), BF16)


def _out_proj_kernel(x_ref, oa_ref, ob_ref, gna_ref, gsw_ref, w_ref, gate_ref, o_ref):
    oa = (_rms(oa_ref[0].astype(F32)) * gna_ref[...]).astype(BF16)
    ob = (_rms(ob_ref[0].astype(F32)) * gsw_ref[...]).astype(BF16)
    mix = (jnp.dot(oa, w_ref[:NA_WIDTH], preferred_element_type=F32)
           + jnp.dot(ob, w_ref[NA_WIDTH:], preferred_element_type=F32))
    o_ref[0] = x_ref[0] + gate_ref[0, 0] * mix


def _out_proj(x, oa, ob, gna, gsw, w, mod):
    bsz, s, d = x.shape
    tm = OUT_TM
    row = lambda b, t: (b, t, 0)
    return pl.pallas_call(
        _out_proj_kernel,
        grid=(bsz, s // tm),
        in_specs=[pl.BlockSpec((1, tm, d), row),
                  pl.BlockSpec((1, tm, NA_WIDTH), row),
                  pl.BlockSpec((1, tm, SW_WIDTH), row),
                  pl.BlockSpec((1, NA_WIDTH), lambda b, t: (0, 0)),
                  pl.BlockSpec((1, SW_WIDTH), lambda b, t: (0, 0)),
                  pl.BlockSpec(w.shape, lambda b, t: (0, 0)),
                  _mod_spec(MOD_GATE_A, d)],
        out_specs=pl.BlockSpec((1, tm, d), row),
        out_shape=jax.ShapeDtypeStruct((bsz, s, d), F32),
        compiler_params=pltpu.CompilerParams(dimension_semantics=("arbitrary", "arbitrary"),
                                             vmem_limit_bytes=VMEM_LIMIT),
        name="out_proj",
    )(x, oa, ob, gna, gsw, w, mod)


def _ffn_kernel(x_ref, g_ref, scale_ref, shift_ref, gate_ref, wup_ref, cw_ref, cb_ref, wdn_ref, gfin_ref,
                o_ref, h_ref, *, row_chunk):
    s = x_ref.shape[1]
    nf = cw_ref.shape[0]
    nblk = s // FF_ROWS

    def prologue(i, carry):
        r0 = pl.multiple_of(i * row_chunk, row_chunk)
        xr = x_ref[0, pl.ds(r0, row_chunk), :]
        h = (_rms(xr) * g_ref[...]) * (1.0 + scale_ref[0, 0]) + shift_ref[0, 0]
        h_ref[pl.ds(r0, row_chunk), :] = h.astype(BF16)
        o_ref[0, pl.ds(r0, row_chunk), :] = jnp.zeros_like(xr)
        return carry

    lax.fori_loop(0, s // row_chunk, prologue, 0)

    ridx = lax.broadcasted_iota(jnp.int32, (FF_ROWS, FF_CHUNK), 0)
    edge = jnp.zeros((1, FF_CHUNK), F32)

    def rows(i):
        return slice(i * FF_ROWS, (i + 1) * FF_ROWS)

    def up(c, i):
        h = h_ref[rows(i), :]
        val_cols = pl.ds(pl.multiple_of(c * FF_CHUNK, FF_CHUNK), FF_CHUNK)
        gate_cols = pl.ds(pl.multiple_of((nf + c) * FF_CHUNK, FF_CHUNK), FF_CHUNK)
        return (jnp.dot(h, wup_ref[:, val_cols], preferred_element_type=F32),
                jnp.dot(h, wup_ref[:, gate_cols], preferred_element_type=F32))

    def act(c, i, cur, before, after):
        val, gt = cur
        before = before[1][FF_ROWS - 1:] if i > 0 else edge
        after = after[1][:1] if i < nblk - 1 else edge
        prev = jnp.where(ridx == 0, before, pltpu.roll(gt, 1, 0))
        nxt = jnp.where(ridx == FF_ROWS - 1, after, pltpu.roll(gt, FF_ROWS - 1, 0))
        cw = cw_ref[c]
        gc = prev * cw[0:1] + gt * cw[1:2] + nxt * cw[2:3] + cb_ref[c]
        return (gc * jax.nn.sigmoid(gc) * val).astype(BF16)

    def run(chunks, last):
        units = [(c, i) for c in chunks for i in range(nblk)]
        ups = {0: up(*units[0]), 1: up(*units[1])}
        for u, (c, i) in enumerate(units):
            a = act(c, i, ups[u], ups.get(u - 1), ups.get(u + 1))
            w = wdn_ref[pl.ds(pl.multiple_of(c * FF_CHUNK, FF_CHUNK), FF_CHUNK), :]
            acc = o_ref[0, rows(i), :] + jnp.dot(a, w, preferred_element_type=F32)
            if last and u >= len(units) - nblk:
                y = x_ref[0, rows(i), :] + gate_ref[0, 0] * acc
                acc = _rms(y) * gfin_ref[...]
            o_ref[0, rows(i), :] = acc
            ups.pop(u - 1, None)
            if u + 2 < len(units):
                ups[u + 2] = up(*units[u + 2])

    def pair(p, carry):
        run([2 * p, 2 * p + 1], False)
        return carry

    lax.fori_loop(0, (nf - 1) // 2, pair, 0)
    run(list(range(2 * ((nf - 1) // 2), nf)), True)


def _ffn(x1, g, mod, wup, cw, cb, wdn, gfin):
    bsz, s, d = x1.shape
    whole = lambda a: pl.BlockSpec(a.shape, lambda b: (0,) * a.ndim, pipeline_mode=pl.Buffered(1))
    vec = lambda which: pl.BlockSpec((1, 1, 1, d), lambda b: (b, which, 0, 0))
    return pl.pallas_call(
        functools.partial(_ffn_kernel, row_chunk=256),
        grid=(bsz,),
        in_specs=[pl.BlockSpec((1, s, d), lambda b: (b, 0, 0)),
                  whole(g), vec(MOD_SCALE_F), vec(MOD_SHIFT_F), vec(MOD_GATE_F),
                  whole(wup), whole(cw), whole(cb), whole(wdn), whole(gfin)],
        out_specs=pl.BlockSpec((1, s, d), lambda b: (b, 0, 0)),
        out_shape=jax.ShapeDtypeStruct((bsz, s, d), F32),
        scratch_shapes=[pltpu.VMEM((s, d), BF16)],
        compiler_params=pltpu.CompilerParams(dimension_semantics=("arbitrary",),
                                             vmem_limit_bytes=FFN_VMEM_LIMIT),
        name="ffn",
    )(x1, g, mod, mod, mod, wup, cw, cb, wdn, gfin)


def _rope_tables(s):
    half = HEAD_DIM // 2
    inv = ROPE_THETA ** (-np.arange(half, dtype=np.float64) / half)
    ang = np.arange(s, dtype=np.float64)[:, None] * inv[None, :]
    cos = np.cos(ang)
    sin = np.sin(ang)
    reps = LANES // HEAD_DIM
    return (jnp.asarray(np.tile(np.concatenate([cos, cos], axis=-1), (1, reps)), F32),
            jnp.asarray(np.tile(np.concatenate([-sin, sin], axis=-1), (1, reps)), F32))


def _sw_head_order(t, axis):
    shape = t.shape
    t = t.reshape(shape[:axis] + (SW_KV_HEADS, SW_GROUP, -1) + shape[axis + 1:])
    return jnp.swapaxes(t, axis, axis + 1).reshape(shape)


def kernel(x, c, w_ada, b_ada, g_attn, w_in, na_rpb, sw_sink, g_na_out, g_sw_out, w_out, g_ffn, w_up,
           conv_w, conv_b, w_down, g_final):
    bsz, s, d = x.shape
    depth = w_ada.shape[0]
    d_ff = w_down.shape[1]
    assert depth == 1, "the final rmsnorm is fused into the (only) layer's ffn call"
    assert d_ff % FF_CHUNK == 0 and s % GRID_W == 0 and s % SW_BLOCK == 0
    cos, sin = _rope_tables(s)
    sw_mask = _sw_mask_table()
    qb0 = 3 * NA_WIDTH
    for l in range(depth):
        mod = _ada(c, w_ada[l], b_ada[l]).reshape(bsz, 6, 1, d)

        wi = w_in[l].astype(BF16)
        wi = jnp.concatenate([wi[:, :qb0], _sw_head_order(wi[:, qb0:qb0 + SW_WIDTH], 1),
                              wi[:, qb0 + SW_WIDTH:]], axis=1)
        qa, ka, va, qb, kvb = _in_proj(x, g_attn[l].reshape(1, d), mod, wi, cos, sin)

        o_a = _na_attention(qa, ka, va, _na_bias_table(na_rpb[l].astype(F32) * LOG2E, s // GRID_W))
        o_b = _sw_attention(_sw_head_order(sw_sink[l].astype(F32) * LOG2E, 0), qb, kvb, sw_mask)

        wo = w_out[l].astype(BF16)
        wo = jnp.concatenate([wo[:NA_WIDTH], _sw_head_order(wo[NA_WIDTH:], 0)], axis=0)
        x = _out_proj(x, o_a, o_b, g_na_out[l].reshape(1, -1), _sw_head_order(g_sw_out[l], 0).reshape(1, -1),
                      wo, mod)

        nf = d_ff // FF_CHUNK
        cw = conv_w[l].reshape(CONV_W, nf, FF_CHUNK).transpose(1, 0, 2)
        x = _ffn(x, g_ffn[l].reshape(1, d), mod, w_up[l].astype(BF16), cw, conv_b[l].reshape(nf, 1, FF_CHUNK),
                 w_down[l].astype(BF16), g_final.reshape(1, d))
    return x
```

```python
import functools

import jax
import jax.numpy as jnp
import numpy as np
from jax import lax
from jax.experimental import pallas as pl
from jax.experimental.pallas import tpu as pltpu

F32 = jnp.float32
BF16 = jnp.bfloat16

HEAD_DIM = 64
LANES = 128
NA_HEADS = 8
NA_WIDTH = NA_HEADS * HEAD_DIM
GRID_W = 64
NA_ROWS_WIN = 8
NA_COLS = 16
SW_HEADS = 8
SW_KV_HEADS = 2
SW_GROUP = SW_HEADS // SW_KV_HEADS
SW_WIDTH = SW_HEADS * HEAD_DIM
SW_KV_WIDTH = SW_KV_HEADS * HEAD_DIM
SW_BLOCK = 128
CONV_W = 3
ROPE_THETA = 10000.0
EPS = 1e-6
NEG = -1e30
LOG2E = 1.4426950408889634
Q_SCALE = HEAD_DIM ** -0.5 * LOG2E

VMEM_LIMIT = 56 * 1024 * 1024
FFN_VMEM_LIMIT = 63 * 1024 * 1024

IN_TM = 1024
IN_ROWS = 256
OUT_TM = 1024
FF_CHUNK = 256
FF_BODY_CHUNKS = 3
FF_ROWS = 512
NA_GROUP = 32
SW_STEP_BLOCKS = 8
SW_AHEAD = 2
NA_AHEAD = 3


def _rms(x):
    return x * lax.rsqrt(jnp.mean(x * x, axis=-1, keepdims=True) + EPS)


def _nt_dot(a, b):
    return lax.dot_general(a, b, (((1,), (1,)), ((), ())), preferred_element_type=F32)


def _lo_mask(rows):
    return lax.broadcasted_iota(jnp.int32, (rows, LANES), 1) < HEAD_DIM


def _ada_kernel(c_ref, w_ref, b_ref, o_ref):
    c = c_ref[...]
    sc = c * jax.nn.sigmoid(c)
    o_ref[...] = jnp.dot(sc, w_ref[...], preferred_element_type=F32,
                         precision=lax.Precision.HIGHEST) + b_ref[...]


def _ada(c, w, b):
    bsz, d = c.shape
    n = w.shape[1]
    tn = 1024
    return pl.pallas_call(
        _ada_kernel,
        grid=(n // tn,),
        in_specs=[pl.BlockSpec((bsz, d), lambda j: (0, 0)),
                  pl.BlockSpec((d, tn), lambda j: (0, j)),
                  pl.BlockSpec((1, tn), lambda j: (0, j))],
        out_specs=pl.BlockSpec((bsz, tn), lambda j: (0, j)),
        out_shape=jax.ShapeDtypeStruct((bsz, n), F32),
        compiler_params=pltpu.CompilerParams(dimension_semantics=("arbitrary",),
                                             vmem_limit_bytes=VMEM_LIMIT),
        name="ada",
    )(c, w, b.reshape(1, n))


MOD_SHIFT_A, MOD_SCALE_A, MOD_GATE_A, MOD_SHIFT_F, MOD_SCALE_F, MOD_GATE_F = range(6)


def _mod_spec(which, d):
    return pl.BlockSpec((1, 1, 1, d), lambda b, t: (b, which, 0, 0))


def _rope(t, cos, sin_signed, first_half):
    rot = jnp.where(first_half, pltpu.roll(t, LANES - HEAD_DIM // 2, 1), pltpu.roll(t, HEAD_DIM // 2, 1))
    return t * cos + rot * sin_signed


def _in_proj_kernel(x_ref, g_ref, scale_ref, shift_ref, w_ref, cos_ref, sin_ref,
                    qa_ref, ka_ref, va_ref, qb_ref, kvb_ref):
    gain = g_ref[...]
    scale = 1.0 + scale_ref[0, 0]
    shift = shift_ref[0, 0]
    first_half = (lax.broadcasted_iota(jnp.int32, (IN_ROWS, LANES), 1) % HEAD_DIM) < HEAD_DIM // 2
    base = 3 * NA_WIDTH

    for i in range(x_ref.shape[1] // IN_ROWS):
        rows = slice(i * IN_ROWS, (i + 1) * IN_ROWS)
        h = ((_rms(x_ref[0, rows, :]) * gain) * scale + shift).astype(BF16)
        cos = cos_ref[rows, :]
        sin = sin_ref[rows, :]

        def proj(lo, hi):
            return jnp.dot(h, w_ref[:, lo:hi], preferred_element_type=F32)

        qa_ref[0, rows, :] = (proj(0, NA_WIDTH) * Q_SCALE).astype(BF16)
        ka_ref[0, rows, :] = proj(NA_WIDTH, 2 * NA_WIDTH).astype(BF16)
        va_ref[0, rows, :] = proj(2 * NA_WIDTH, 3 * NA_WIDTH).astype(BF16)
        qb = proj(base, base + SW_WIDTH)
        for j in range(SW_WIDTH // LANES):
            blk = _rope(qb[:, j * LANES:(j + 1) * LANES], cos, sin, first_half) * Q_SCALE
            qb_ref[0, rows, j * LANES:(j + 1) * LANES] = blk.astype(BF16)
        kv = proj(base + SW_WIDTH, base + SW_WIDTH + 2 * SW_KV_WIDTH)
        kvb_ref[0, rows, :LANES] = _rope(kv[:, :LANES], cos, sin, first_half).astype(BF16)
        kvb_ref[0, rows, LANES:] = kv[:, LANES:].astype(BF16)


def _in_proj(x, g, mod, w, cos, sin):
    bsz, s, d = x.shape
    n = w.shape[1]
    tm = IN_TM
    row = lambda b, t: (b, t, 0)
    outs = [jax.ShapeDtypeStruct((bsz, s, NA_WIDTH), BF16)] * 3 + [
        jax.ShapeDtypeStruct((bsz, s, SW_WIDTH), BF16),
        jax.ShapeDtypeStruct((bsz, s, 2 * SW_KV_WIDTH), BF16)]
    return pl.pallas_call(
        _in_proj_kernel,
        grid=(bsz, s // tm),
        in_specs=[pl.BlockSpec((1, tm, d), row),
                  pl.BlockSpec((1, d), lambda b, t: (0, 0)),
                  _mod_spec(MOD_SCALE_A, d),
                  _mod_spec(MOD_SHIFT_A, d),
                  pl.BlockSpec((d, n), lambda b, t: (0, 0)),
                  pl.BlockSpec((tm, LANES), lambda b, t: (t, 0)),
                  pl.BlockSpec((tm, LANES), lambda b, t: (t, 0))],
        out_specs=[pl.BlockSpec((1, tm, NA_WIDTH), row)] * 3 + [
            pl.BlockSpec((1, tm, SW_WIDTH), row),
            pl.BlockSpec((1, tm, 2 * SW_KV_WIDTH), row)],
        out_shape=outs,
        compiler_params=pltpu.CompilerParams(dimension_semantics=("arbitrary", "arbitrary"),
                                             vmem_limit_bytes=VMEM_LIMIT),
        name="in_proj",
    )(x, g, mod, mod, w, cos, sin)


def _na_kernel(q_ref, k_ref, v_ref, bias_ref, o_ref, *, rows):
    lo = _lo_mask(GRID_W)
    win = NA_ROWS_WIN * GRID_W
    ones = jnp.ones((win, LANES), BF16)

    def scores(r):
        rs = jnp.clip(r - NA_ROWS_WIN // 2, 0, rows - NA_ROWS_WIN)
        q = q_ref[0, pl.ds(pl.multiple_of(r * GRID_W, GRID_W), GRID_W), :]
        k0 = pl.multiple_of(rs * GRID_W, GRID_W)
        zero = jnp.zeros_like(q)
        qs = jnp.concatenate([jnp.where(lo, q, zero), jnp.where(lo, zero, q)], axis=0)
        dr0 = NA_ROWS_WIN - 1 - (r - rs)
        bias = jnp.concatenate(
            [jnp.concatenate([bias_ref[head, dr0 + 2 * jj] for jj in range(NA_ROWS_WIN // 2)], axis=1)
             for head in range(2)], axis=0)
        return _nt_dot(qs, k_ref[0, pl.ds(k0, win), :]) + bias, k0

    def finish(r, s, k0):
        p = jnp.exp2(s - jnp.max(s, axis=-1, keepdims=True)).astype(BF16)
        oa = jnp.dot(p, jnp.concatenate([v_ref[0, pl.ds(k0, win), :], ones], axis=1), preferred_element_type=F32)
        o = oa[:, :LANES] * (1.0 / oa[:, LANES:])
        out = jnp.where(lo, o[:GRID_W], o[GRID_W:])
        o_ref[0, pl.ds(pl.multiple_of(r * GRID_W, GRID_W), GRID_W), :] = out.astype(o_ref.dtype)

    def body(g, carry):
        base = g * NA_GROUP
        staged = [scores(base + i) for i in range(NA_AHEAD)]
        for i in range(NA_GROUP):
            if i + NA_AHEAD < NA_GROUP:
                staged.append(scores(base + i + NA_AHEAD))
            finish(base + i, *staged[i])
        return carry

    lax.fori_loop(0, rows // NA_GROUP, body, 0)


def _na_attention(q, k, v, bias):
    bsz, s, _ = q.shape
    rows = s // GRID_W
    pairs = NA_WIDTH // LANES
    blk = pl.BlockSpec((1, s, LANES), lambda hp, b: (b, 0, hp))
    return pl.pallas_call(
        functools.partial(_na_kernel, rows=rows),
        grid=(pairs, bsz),
        in_specs=[blk, blk, blk,
                  pl.BlockSpec((2,) + bias.shape[1:], lambda hp, b: (hp, 0, 0, 0))],
        out_specs=blk,
        out_shape=jax.ShapeDtypeStruct((bsz, s, NA_WIDTH), BF16),
        compiler_params=pltpu.CompilerParams(dimension_semantics=("arbitrary", "arbitrary"),
                                             vmem_limit_bytes=VMEM_LIMIT),
        name="na_attn",
    )(q, k, v, bias)


def _na_bias_table(rpb, rows):
    assert rows >= 2 * NA_ROWS_WIN
    col = np.arange(GRID_W)
    cs = np.clip(col - NA_COLS // 2, 0, GRID_W - NA_COLS)
    col_ok = (col[None, :] >= cs[:, None]) & (col[None, :] < cs[:, None] + NA_COLS)
    dc = np.clip(col[None, :] - col[:, None] + NA_COLS - 1, 0, 2 * NA_COLS - 2)
    onehot = (np.arange(2 * NA_COLS - 1)[:, None] == dc.reshape(1, -1)).astype(np.float32)
    t = jnp.dot(rpb.reshape(-1, 2 * NA_COLS - 1), onehot, precision=lax.Precision.HIGHEST)
    t = jnp.where(col_ok[None, None], t.reshape(NA_HEADS, 2 * NA_ROWS_WIN - 1, GRID_W, GRID_W), NEG)
    return jnp.concatenate([t[:, :-1], t[:, 1:]], axis=-1)


def _sw_kernel(sink_ref, q_ref, k_ref, v_ref, mask_ref, o_ref, *, nblocks):
    lo = _lo_mask(SW_BLOCK)
    zero = jnp.zeros((SW_BLOCK, LANES), q_ref.dtype)
    ones = jnp.ones((3 * SW_BLOCK, LANES), BF16)
    ri = lax.broadcasted_iota(jnp.int32, (2 * SW_BLOCK, LANES), 0)
    ci = lax.broadcasted_iota(jnp.int32, (2 * SW_BLOCK, LANES), 1)
    eye2 = jnp.where((ri == ci) | (ri == ci + SW_BLOCK), 1.0, 0.0).astype(BF16)
    npairs = SW_WIDTH // LANES

    def window(u):
        n = pl.program_id(1) * SW_STEP_BLOCKS + u
        start = pl.multiple_of(jnp.clip(n - 1, 0, nblocks - 3) * SW_BLOCK, SW_BLOCK)
        variant = jnp.where(n == 0, 0, jnp.where(n == nblocks - 1, 2, 1))
        return start, variant

    def scores(u, j):
        start, variant = window(u)
        qj = q_ref[0, u * SW_BLOCK:(u + 1) * SW_BLOCK, j * LANES:(j + 1) * LANES]
        qs = jnp.concatenate([jnp.where(lo, qj, zero), jnp.where(lo, zero, qj)], axis=0)
        return _nt_dot(jnp.concatenate([qs, eye2], axis=1),
                       jnp.concatenate([k_ref[0, pl.ds(start, 3 * SW_BLOCK), :], mask_ref[variant]], axis=1))

    def finish(u, j, s):
        start, _ = window(u)
        ps, sinks = [], []
        for half in range(2):
            sg = s[half * SW_BLOCK:(half + 1) * SW_BLOCK]
            sk = sink_ref[2 * j + half]
            m = jnp.maximum(jnp.max(sg, axis=-1, keepdims=True), sk)
            ps.append(jnp.exp2(sg - m).astype(BF16))
            sinks.append(jnp.exp2(sk - m))
        oa = jnp.dot(jnp.concatenate(ps, axis=0),
                     jnp.concatenate([v_ref[0, pl.ds(start, 3 * SW_BLOCK), :], ones], axis=1),
                     preferred_element_type=F32)
        o = oa[:, :LANES] * (1.0 / (oa[:, LANES:] + jnp.concatenate(sinks, axis=0)))
        out = jnp.where(lo, o[:SW_BLOCK], o[SW_BLOCK:])
        o_ref[0, u * SW_BLOCK:(u + 1) * SW_BLOCK, j * LANES:(j + 1) * LANES] = out.astype(o_ref.dtype)

    units = [(u, j) for u in range(SW_STEP_BLOCKS) for j in range(npairs)]
    staged = [scores(*units[i]) for i in range(SW_AHEAD)]
    for i, unit in enumerate(units):
        if i + SW_AHEAD < len(units):
            staged.append(scores(*units[i + SW_AHEAD]))
        finish(*unit, staged[i])
        staged[i] = None


def _sw_attention(sink, q, kv, mask):
    bsz, s, _ = q.shape
    nblocks = s // SW_BLOCK
    assert nblocks >= 3 and nblocks % SW_STEP_BLOCKS == 0
    rows = SW_STEP_BLOCKS * SW_BLOCK
    grid_spec = pltpu.PrefetchScalarGridSpec(
        num_scalar_prefetch=1,
        grid=(bsz, nblocks // SW_STEP_BLOCKS),
        in_specs=[pl.BlockSpec((1, rows, SW_WIDTH), lambda b, n, sk: (b, n, 0)),
                  pl.BlockSpec((1, s, LANES), lambda b, n, sk: (b, 0, 0)),
                  pl.BlockSpec((1, s, LANES), lambda b, n, sk: (b, 0, 1)),
                  pl.BlockSpec(mask.shape, lambda b, n, sk: (0, 0, 0))],
        out_specs=pl.BlockSpec((1, rows, SW_WIDTH), lambda b, n, sk: (b, n, 0)),
    )
    return pl.pallas_call(
        functools.partial(_sw_kernel, nblocks=nblocks),
        grid_spec=grid_spec,
        out_shape=jax.ShapeDtypeStruct((bsz, s, SW_WIDTH), BF16),
        compiler_params=pltpu.CompilerParams(dimension_semantics=("arbitrary", "arbitrary"),
                                             vmem_limit_bytes=VMEM_LIMIT),
        name="sw_attn",
    )(sink, q, kv, kv, mask)


def _sw_mask_table():
    i = np.arange(SW_BLOCK)[:, None]
    j = np.arange(3 * SW_BLOCK)[None, :]
    ok = np.stack([np.abs(i - j) <= SW_BLOCK,
                   np.abs(i + SW_BLOCK - j) <= SW_BLOCK,
                   np.abs(i + 2 * SW_BLOCK - j) <= SW_BLOCK])
    return jnp.asarray(np.where(ok, 0.0, NEG).transpose(0, 2, 1), BF16)


def _out_proj_kernel(x_ref, oa_ref, ob_ref, gna_ref, gsw_ref, w_ref, gate_ref, o_ref):
    oa = (_rms(oa_ref[0].astype(F32)) * gna_ref[...]).astype(BF16)
    ob = (_rms(ob_ref[0].astype(F32)) * gsw_ref[...]).astype(BF16)
    mix = (jnp.dot(oa, w_ref[:NA_WIDTH], preferred_element_type=F32)
           + jnp.dot(ob, w_ref[NA_WIDTH:], preferred_element_type=F32))
    o_ref[0] = x_ref[0] + gate_ref[0, 0] * mix


def _out_proj(x, oa, ob, gna, gsw, w, mod):
    bsz, s, d = x.shape
    tm = OUT_TM
    row = lambda b, t: (b, t, 0)
    return pl.pallas_call(
        _out_proj_kernel,
        grid=(bsz, s // tm),
        in_specs=[pl.BlockSpec((1, tm, d), row),
                  pl.BlockSpec((1, tm, NA_WIDTH), row),
                  pl.BlockSpec((1, tm, SW_WIDTH), row),
                  pl.BlockSpec((1, NA_WIDTH), lambda b, t: (0, 0)),
                  pl.BlockSpec((1, SW_WIDTH), lambda b, t: (0, 0)),
                  pl.BlockSpec(w.shape, lambda b, t: (0, 0)),
                  _mod_spec(MOD_GATE_A, d)],
        out_specs=pl.BlockSpec((1, tm, d), row),
        out_shape=jax.ShapeDtypeStruct((bsz, s, d), F32),
        compiler_params=pltpu.CompilerParams(dimension_semantics=("arbitrary", "arbitrary"),
                                             vmem_limit_bytes=VMEM_LIMIT),
        name="out_proj",
    )(x, oa, ob, gna, gsw, w, mod)


def _ffn_kernel(x_ref, g_ref, scale_ref, shift_ref, gate_ref, wup_ref, cw_ref, cb_ref, wdn_ref, gfin_ref,
                o_ref, h_ref, *, row_chunk):
    s = x_ref.shape[1]
    nf = cw_ref.shape[0]
    nblk = s // FF_ROWS

    def prologue(i, carry):
        r0 = pl.multiple_of(i * row_chunk, row_chunk)
        xr = x_ref[0, pl.ds(r0, row_chunk), :]
        h = (_rms(xr) * g_ref[...]) * (1.0 + scale_ref[0, 0]) + shift_ref[0, 0]
        h_ref[pl.ds(r0, row_chunk), :] = h.astype(BF16)
        o_ref[0, pl.ds(r0, row_chunk), :] = jnp.zeros_like(xr)
        return carry

    lax.fori_loop(0, s // row_chunk, prologue, 0)

    ridx = lax.broadcasted_iota(jnp.int32, (FF_ROWS, FF_CHUNK), 0)
    edge = jnp.zeros((1, FF_CHUNK), F32)

    def rows(i):
        return slice(i * FF_ROWS, (i + 1) * FF_ROWS)

    def up(c, i):
        h = h_ref[rows(i), :]
        val_cols = pl.ds(pl.multiple_of(c * FF_CHUNK, FF_CHUNK), FF_CHUNK)
        gate_cols = pl.ds(pl.multiple_of((nf + c) * FF_CHUNK, FF_CHUNK), FF_CHUNK)
        return (jnp.dot(h, wup_ref[:, val_cols], preferred_element_type=F32),
                jnp.dot(h, wup_ref[:, gate_cols], preferred_element_type=F32))

    def act(c, i, cur, before, after):
        val, gt = cur
        before = before[1][FF_ROWS - 1:] if i > 0 else edge
        after = after[1][:1] if i < nblk - 1 else edge
        prev = jnp.where(ridx == 0, before, pltpu.roll(gt, 1, 0))
        nxt = jnp.where(ridx == FF_ROWS - 1, after, pltpu.roll(gt, FF_ROWS - 1, 0))
        cw = cw_ref[c]
        gc = prev * cw[0:1] + gt * cw[1:2] + nxt * cw[2:3] + cb_ref[c]
        return (gc * jax.nn.sigmoid(gc) * val).astype(BF16)

    def run(chunks, last):
        units = [(c, i) for c in chunks for i in range(nblk)]
        ups = {0: up(*units[0]), 1: up(*units[1])}
        w_rows = pl.ds(pl.multiple_of(chunks[0] * FF_CHUNK, FF_CHUNK), len(chunks) * FF_CHUNK)
        gated = {i: [] for i in range(nblk)}
        for u, (c, i) in enumerate(units):
            gated[i].append(act(c, i, ups[u], ups.get(u - 1), ups.get(u + 1)))
            if u >= len(units) - nblk:
                a = jnp.concatenate(gated.pop(i), axis=1)
                acc = o_ref[0, rows(i), :] + jnp.dot(a, wdn_ref[w_rows, :], preferred_element_type=F32)
                if last:
                    y = x_ref[0, rows(i), :] + gate_ref[0, 0] * acc
                    acc = _rms(y) * gfin_ref[...]
                o_ref[0, rows(i), :] = acc
            ups.pop(u - 1, None)
            if u + 2 < len(units):
                ups[u + 2] = up(*units[u + 2])

    def body(p, carry):
        run([FF_BODY_CHUNKS * p + j for j in range(FF_BODY_CHUNKS)], False)
        return carry

    nloop = (nf - 1) // FF_BODY_CHUNKS
    lax.fori_loop(0, nloop, body, 0)
    run(list(range(FF_BODY_CHUNKS * nloop, nf)), True)


def _ffn(x1, g, mod, wup, cw, cb, wdn, gfin):
    bsz, s, d = x1.shape
    whole = lambda a: pl.BlockSpec(a.shape, lambda b: (0,) * a.ndim, pipeline_mode=pl.Buffered(1))
    vec = lambda which: pl.BlockSpec((1, 1, 1, d), lambda b: (b, which, 0, 0))
    return pl.pallas_call(
        functools.partial(_ffn_kernel, row_chunk=256),
        grid=(bsz,),
        in_specs=[pl.BlockSpec((1, s, d), lambda b: (b, 0, 0)),
                  whole(g), vec(MOD_SCALE_F), vec(MOD_SHIFT_F), vec(MOD_GATE_F),
                  whole(wup), whole(cw), whole(cb), whole(wdn), whole(gfin)],
        out_specs=pl.BlockSpec((1, s, d), lambda b: (b, 0, 0)),
        out_shape=jax.ShapeDtypeStruct((bsz, s, d), F32),
        scratch_shapes=[pltpu.VMEM((s, d), BF16)],
        compiler_params=pltpu.CompilerParams(dimension_semantics=("arbitrary",),
                                             vmem_limit_bytes=FFN_VMEM_LIMIT),
        name="ffn",
    )(x1, g, mod, mod, mod, wup, cw, cb, wdn, gfin)


def _rope_tables(s):
    half = HEAD_DIM // 2
    inv = ROPE_THETA ** (-np.arange(half, dtype=np.float64) / half)
    ang = np.arange(s, dtype=np.float64)[:, None] * inv[None, :]
    cos = np.cos(ang)
    sin = np.sin(ang)
    reps = LANES // HEAD_DIM
    return (jnp.asarray(np.tile(np.concatenate([cos, cos], axis=-1), (1, reps)), F32),
            jnp.asarray(np.tile(np.concatenate([-sin, sin], axis=-1), (1, reps)), F32))


def _sw_head_order(t, axis):
    shape = t.shape
    t = t.reshape(shape[:axis] + (SW_KV_HEADS, SW_GROUP, -1) + shape[axis + 1:])
    return jnp.swapaxes(t, axis, axis + 1).reshape(shape)


def kernel(x, c, w_ada, b_ada, g_attn, w_in, na_rpb, sw_sink, g_na_out, g_sw_out, w_out, g_ffn, w_up,
           conv_w, conv_b, w_down, g_final):
    bsz, s, d = x.shape
    depth = w_ada.shape[0]
    d_ff = w_down.shape[1]
    assert depth == 1, "the final rmsnorm is fused into the (only) layer's ffn call"
    assert d_ff % FF_CHUNK == 0 and s % GRID_W == 0 and s % SW_BLOCK == 0
    cos, sin = _rope_tables(s)
    sw_mask = _sw_mask_table()
    qb0 = 3 * NA_WIDTH
    for l in range(depth):
        mod = _ada(c, w_ada[l], b_ada[l]).reshape(bsz, 6, 1, d)

        wi = w_in[l].astype(BF16)
        wi = jnp.concatenate([wi[:, :qb0], _sw_head_order(wi[:, qb0:qb0 + SW_WIDTH], 1),
                              wi[:, qb0 + SW_WIDTH:]], axis=1)
        qa, ka, va, qb, kvb = _in_proj(x, g_attn[l].reshape(1, d), mod, wi, cos, sin)

        o_a = _na_attention(qa, ka, va, _na_bias_table(na_rpb[l].astype(F32) * LOG2E, s // GRID_W))
        o_b = _sw_attention(_sw_head_order(sw_sink[l].astype(F32) * LOG2E, 0), qb, kvb, sw_mask)

        wo = w_out[l].astype(BF16)
        wo = jnp.concatenate([wo[:NA_WIDTH], _sw_head_order(wo[NA_WIDTH:], 0)], axis=0)
        x = _out_proj(x, o_a, o_b, g_na_out[l].reshape(1, -1), _sw_head_order(g_sw_out[l], 0).reshape(1, -1),
                      wo, mod)

        nf = d_ff // FF_CHUNK
        cw = conv_w[l].reshape(CONV_W, nf, FF_CHUNK).transpose(1, 0, 2)
        x = _ffn(x, g_ffn[l].reshape(1, d), mod, w_up[l].astype(BF16), cw, conv_b[l].reshape(nf, 1, FF_CHUNK),
                 w_down[l].astype(BF16), g_final.reshape(1, d))
    return x
```

```python
import functools

import jax
import jax.numpy as jnp
import numpy as np
from jax import lax
from jax.experimental import pallas as pl
from jax.experimental.pallas import tpu as pltpu

F32 = jnp.float32
BF16 = jnp.bfloat16

HEAD_DIM = 64
LANES = 128
NA_HEADS = 8
NA_WIDTH = NA_HEADS * HEAD_DIM
GRID_W = 64
NA_ROWS_WIN = 8
NA_COLS = 16
SW_HEADS = 8
SW_KV_HEADS = 2
SW_GROUP = SW_HEADS // SW_KV_HEADS
SW_WIDTH = SW_HEADS * HEAD_DIM
SW_KV_WIDTH = SW_KV_HEADS * HEAD_DIM
SW_BLOCK = 128
CONV_W = 3
ROPE_THETA = 10000.0
EPS = 1e-6
NEG = -1e30
LOG2E = 1.4426950408889634
Q_SCALE = HEAD_DIM ** -0.5 * LOG2E

VMEM_LIMIT = 56 * 1024 * 1024
FFN_VMEM_LIMIT = 63 * 1024 * 1024

ADA_TN = 2048
IN_TM = 2048
IN_ROWS = 256
OUT_TM = 2048
FF_CHUNK = 256
FF_BODY_CHUNKS = 3
FF_ROWS = 512
NA_GROUP = 32
SW_STEP_BLOCKS = 8
SW_AHEAD = 2
NA_AHEAD = 3


def _rms(x):
    return x * lax.rsqrt(jnp.mean(x * x, axis=-1, keepdims=True) + EPS)


def _nt_dot(a, b):
    return lax.dot_general(a, b, (((1,), (1,)), ((), ())), preferred_element_type=F32)


def _lo_mask(rows):
    return lax.broadcasted_iota(jnp.int32, (rows, LANES), 1) < HEAD_DIM


def _split_bf16(a):
    hi = a.astype(BF16)
    return hi, (a - hi.astype(F32)).astype(BF16)


def _ada_kernel(c_ref, w_ref, b_ref, o_ref):
    c = c_ref[...]
    s_hi, s_lo = _split_bf16(c * jax.nn.sigmoid(c))
    w_hi, w_lo = _split_bf16(w_ref[...])
    dot = functools.partial(jnp.dot, preferred_element_type=F32)
    o_ref[...] = dot(s_hi, w_hi) + dot(s_hi, w_lo) + dot(s_lo, w_hi) + b_ref[...]


def _ada(c, w, b):
    bsz, d = c.shape
    n = w.shape[1]
    tn = ADA_TN
    return pl.pallas_call(
        _ada_kernel,
        grid=(n // tn,),
        in_specs=[pl.BlockSpec((bsz, d), lambda j: (0, 0)),
                  pl.BlockSpec((d, tn), lambda j: (0, j)),
                  pl.BlockSpec((1, tn), lambda j: (0, j))],
        out_specs=pl.BlockSpec((bsz, tn), lambda j: (0, j)),
        out_shape=jax.ShapeDtypeStruct((bsz, n), F32),
        compiler_params=pltpu.CompilerParams(dimension_semantics=("arbitrary",),
                                             vmem_limit_bytes=VMEM_LIMIT),
        name="ada",
    )(c, w, b.reshape(1, n))


MOD_SHIFT_A, MOD_SCALE_A, MOD_GATE_A, MOD_SHIFT_F, MOD_SCALE_F, MOD_GATE_F = range(6)


def _mod_spec(which, d):
    return pl.BlockSpec((1, 1, 1, d), lambda b, t: (b, which, 0, 0))


def _rope(t, cos, sin_signed, first_half):
    rot = jnp.where(first_half, pltpu.roll(t, LANES - HEAD_DIM // 2, 1), pltpu.roll(t, HEAD_DIM // 2, 1))
    return t * cos + rot * sin_signed


def _in_proj_kernel(x_ref, g_ref, scale_ref, shift_ref, w_ref, cos_ref, sin_ref,
                    qa_ref, ka_ref, va_ref, qb_ref, kvb_ref):
    gain = g_ref[...]
    scale = 1.0 + scale_ref[0, 0]
    shift = shift_ref[0, 0]
    first_half = (lax.broadcasted_iota(jnp.int32, (IN_ROWS, LANES), 1) % HEAD_DIM) < HEAD_DIM // 2
    base = 3 * NA_WIDTH

    for i in range(x_ref.shape[1] // IN_ROWS):
        rows = slice(i * IN_ROWS, (i + 1) * IN_ROWS)
        h = ((_rms(x_ref[0, rows, :]) * gain) * scale + shift).astype(BF16)
        cos = cos_ref[rows, :]
        sin = sin_ref[rows, :]

        def proj(lo, hi):
            return jnp.dot(h, w_ref[:, lo:hi], preferred_element_type=F32)

        qa_ref[0, rows, :] = (proj(0, NA_WIDTH) * Q_SCALE).astype(BF16)
        ka_ref[0, rows, :] = proj(NA_WIDTH, 2 * NA_WIDTH).astype(BF16)
        va_ref[0, rows, :] = proj(2 * NA_WIDTH, 3 * NA_WIDTH).astype(BF16)
        qb = proj(base, base + SW_WIDTH)
        for j in range(SW_WIDTH // LANES):
            blk = _rope(qb[:, j * LANES:(j + 1) * LANES], cos, sin, first_half) * Q_SCALE
            qb_ref[0, rows, j * LANES:(j + 1) * LANES] = blk.astype(BF16)
        kv = proj(base + SW_WIDTH, base + SW_WIDTH + 2 * SW_KV_WIDTH)
        kvb_ref[0, rows, :LANES] = _rope(kv[:, :LANES], cos, sin, first_half).astype(BF16)
        kvb_ref[0, rows, LANES:] = kv[:, LANES:].astype(BF16)


def _in_proj(x, g, mod, w, cos, sin):
    bsz, s, d = x.shape
    n = w.shape[1]
    tm = IN_TM
    row = lambda b, t: (b, t, 0)
    outs = [jax.ShapeDtypeStruct((bsz, s, NA_WIDTH), BF16)] * 3 + [
        jax.ShapeDtypeStruct((bsz, s, SW_WIDTH), BF16),
        jax.ShapeDtypeStruct((bsz, s, 2 * SW_KV_WIDTH), BF16)]
    return pl.pallas_call(
        _in_proj_kernel,
        grid=(bsz, s // tm),
        in_specs=[pl.BlockSpec((1, tm, d), row),
                  pl.BlockSpec((1, d), lambda b, t: (0, 0)),
                  _mod_spec(MOD_SCALE_A, d),
                  _mod_spec(MOD_SHIFT_A, d),
                  pl.BlockSpec((d, n), lambda b, t: (0, 0)),
                  pl.BlockSpec((tm, LANES), lambda b, t: (t, 0)),
                  pl.BlockSpec((tm, LANES), lambda b, t: (t, 0))],
        out_specs=[pl.BlockSpec((1, tm, NA_WIDTH), row)] * 3 + [
            pl.BlockSpec((1, tm, SW_WIDTH), row),
            pl.BlockSpec((1, tm, 2 * SW_KV_WIDTH), row)],
        out_shape=outs,
        compiler_params=pltpu.CompilerParams(dimension_semantics=("arbitrary", "arbitrary"),
                                             vmem_limit_bytes=VMEM_LIMIT),
        name="in_proj",
    )(x, g, mod, mod, w, cos, sin)


def _na_kernel(q_ref, k_ref, v_ref, bias_ref, o_ref, *, rows):
    lo = _lo_mask(GRID_W)
    win = NA_ROWS_WIN * GRID_W
    ones = jnp.ones((win, LANES), BF16)

    def scores(r):
        rs = jnp.clip(r - NA_ROWS_WIN // 2, 0, rows - NA_ROWS_WIN)
        q = q_ref[0, pl.ds(pl.multiple_of(r * GRID_W, GRID_W), GRID_W), :]
        k0 = pl.multiple_of(rs * GRID_W, GRID_W)
        zero = jnp.zeros_like(q)
        qs = jnp.concatenate([jnp.where(lo, q, zero), jnp.where(lo, zero, q)], axis=0)
        dr0 = NA_ROWS_WIN - 1 - (r - rs)
        bias = jnp.concatenate(
            [jnp.concatenate([bias_ref[head, dr0 + 2 * jj] for jj in range(NA_ROWS_WIN // 2)], axis=1)
             for head in range(2)], axis=0)
        return _nt_dot(qs, k_ref[0, pl.ds(k0, win), :]) + bias, k0

    def finish(r, s, k0):
        p = jnp.exp2(s - jnp.max(s, axis=-1, keepdims=True)).astype(BF16)
        oa = jnp.dot(p, jnp.concatenate([v_ref[0, pl.ds(k0, win), :], ones], axis=1), preferred_element_type=F32)
        o = oa[:, :LANES] * (1.0 / oa[:, LANES:])
        out = jnp.where(lo, o[:GRID_W], o[GRID_W:])
        o_ref[0, pl.ds(pl.multiple_of(r * GRID_W, GRID_W), GRID_W), :] = out.astype(o_ref.dtype)

    def body(g, carry):
        base = g * NA_GROUP
        staged = [scores(base + i) for i in range(NA_AHEAD)]
        for i in range(NA_GROUP):
            if i + NA_AHEAD < NA_GROUP:
                staged.append(scores(base + i + NA_AHEAD))
            finish(base + i, *staged[i])
        return carry

    lax.fori_loop(0, rows // NA_GROUP, body, 0)


def _na_attention(q, k, v, bias):
    bsz, s, _ = q.shape
    rows = s // GRID_W
    pairs = NA_WIDTH // LANES
    blk = pl.BlockSpec((1, s, LANES), lambda hp, b: (b, 0, hp))
    return pl.pallas_call(
        functools.partial(_na_kernel, rows=rows),
        grid=(pairs, bsz),
        in_specs=[blk, blk, blk,
                  pl.BlockSpec((2,) + bias.shape[1:], lambda hp, b: (hp, 0, 0, 0))],
        out_specs=blk,
        out_shape=jax.ShapeDtypeStruct((bsz, s, NA_WIDTH), BF16),
        compiler_params=pltpu.CompilerParams(dimension_semantics=("arbitrary", "arbitrary"),
                                             vmem_limit_bytes=VMEM_LIMIT),
        name="na_attn",
    )(q, k, v, bias)


def _na_bias_table(rpb, rows):
    assert rows >= 2 * NA_ROWS_WIN
    col = np.arange(GRID_W)
    cs = np.clip(col - NA_COLS // 2, 0, GRID_W - NA_COLS)
    col_ok = (col[None, :] >= cs[:, None]) & (col[None, :] < cs[:, None] + NA_COLS)
    dc = np.clip(col[None, :] - col[:, None] + NA_COLS - 1, 0, 2 * NA_COLS - 2)
    onehot = (np.arange(2 * NA_COLS - 1)[:, None] == dc.reshape(1, -1)).astype(np.float32)
    t = jnp.dot(rpb.reshape(-1, 2 * NA_COLS - 1), onehot, precision=lax.Precision.HIGHEST)
    t = jnp.where(col_ok[None, None], t.reshape(NA_HEADS, 2 * NA_ROWS_WIN - 1, GRID_W, GRID_W), NEG)
    return jnp.concatenate([t[:, :-1], t[:, 1:]], axis=-1)


def _sw_kernel(sink_ref, q_ref, k_ref, v_ref, mask_ref, o_ref, *, nblocks):
    lo = _lo_mask(SW_BLOCK)
    zero = jnp.zeros((SW_BLOCK, LANES), q_ref.dtype)
    ones = jnp.ones((3 * SW_BLOCK, LANES), BF16)
    ri = lax.broadcasted_iota(jnp.int32, (2 * SW_BLOCK, LANES), 0)
    ci = lax.broadcasted_iota(jnp.int32, (2 * SW_BLOCK, LANES), 1)
    eye2 = jnp.where((ri == ci) | (ri == ci + SW_BLOCK), 1.0, 0.0).astype(BF16)
    npairs = SW_WIDTH // LANES

    def window(u):
        n = pl.program_id(1) * SW_STEP_BLOCKS + u
        start = pl.multiple_of(jnp.clip(n - 1, 0, nblocks - 3) * SW_BLOCK, SW_BLOCK)
        variant = jnp.where(n == 0, 0, jnp.where(n == nblocks - 1, 2, 1))
        return start, variant

    def scores(u, j):
        start, variant = window(u)
        qj = q_ref[0, u * SW_BLOCK:(u + 1) * SW_BLOCK, j * LANES:(j + 1) * LANES]
        qs = jnp.concatenate([jnp.where(lo, qj, zero), jnp.where(lo, zero, qj)], axis=0)
        return _nt_dot(jnp.concatenate([qs, eye2], axis=1),
                       jnp.concatenate([k_ref[0, pl.ds(start, 3 * SW_BLOCK), :], mask_ref[variant]], axis=1))

    def finish(u, j, s):
        start, _ = window(u)
        ps, sinks = [], []
        for half in range(2):
            sg = s[half * SW_BLOCK:(half + 1) * SW_BLOCK]
            sk = sink_ref[2 * j + half]
            m = jnp.maximum(jnp.max(sg, axis=-1, keepdims=True), sk)
            ps.append(jnp.exp2(sg - m).astype(BF16))
            sinks.append(jnp.exp2(sk - m))
        oa = jnp.dot(jnp.concatenate(ps, axis=0),
                     jnp.concatenate([v_ref[0, pl.ds(start, 3 * SW_BLOCK), :], ones], axis=1),
                     preferred_element_type=F32)
        o = oa[:, :LANES] * (1.0 / (oa[:, LANES:] + jnp.concatenate(sinks, axis=0)))
        out = jnp.where(lo, o[:SW_BLOCK], o[SW_BLOCK:])
        o_ref[0, u * SW_BLOCK:(u + 1) * SW_BLOCK, j * LANES:(j + 1) * LANES] = out.astype(o_ref.dtype)

    units = [(u, j) for u in range(SW_STEP_BLOCKS) for j in range(npairs)]
    staged = [scores(*units[i]) for i in range(SW_AHEAD)]
    for i, unit in enumerate(units):
        if i + SW_AHEAD < len(units):
            staged.append(scores(*units[i + SW_AHEAD]))
        finish(*unit, staged[i])
        staged[i] = None


def _sw_attention(sink, q, kv, mask):
    bsz, s, _ = q.shape
    nblocks = s // SW_BLOCK
    assert nblocks >= 3 and nblocks % SW_STEP_BLOCKS == 0
    rows = SW_STEP_BLOCKS * SW_BLOCK
    grid_spec = pltpu.PrefetchScalarGridSpec(
        num_scalar_prefetch=1,
        grid=(bsz, nblocks // SW_STEP_BLOCKS),
        in_specs=[pl.BlockSpec((1, rows, SW_WIDTH), lambda b, n, sk: (b, n, 0)),
                  pl.BlockSpec((1, s, LANES), lambda b, n, sk: (b, 0, 0)),
                  pl.BlockSpec((1, s, LANES), lambda b, n, sk: (b, 0, 1)),
                  pl.BlockSpec(mask.shape, lambda b, n, sk: (0, 0, 0))],
        out_specs=pl.BlockSpec((1, rows, SW_WIDTH), lambda b, n, sk: (b, n, 0)),
    )
    return pl.pallas_call(
        functools.partial(_sw_kernel, nblocks=nblocks),
        grid_spec=grid_spec,
        out_shape=jax.ShapeDtypeStruct((bsz, s, SW_WIDTH), BF16),
        compiler_params=pltpu.CompilerParams(dimension_semantics=("arbitrary", "arbitrary"),
                                             vmem_limit_bytes=VMEM_LIMIT),
        name="sw_attn",
    )(sink, q, kv, kv, mask)


def _sw_mask_table():
    i = np.arange(SW_BLOCK)[:, None]
    j = np.arange(3 * SW_BLOCK)[None, :]
    ok = np.stack([np.abs(i - j) <= SW_BLOCK,
                   np.abs(i + SW_BLOCK - j) <= SW_BLOCK,
                   np.abs(i + 2 * SW_BLOCK - j) <= SW_BLOCK])
    return jnp.asarray(np.where(ok, 0.0, NEG).transpose(0, 2, 1), BF16)


def _out_proj_kernel(x_ref, oa_ref, ob_ref, gna_ref, gsw_ref, w_ref, gate_ref, o_ref):
    oa = (_rms(oa_ref[0].astype(F32)) * gna_ref[...]).astype(BF16)
    ob = (_rms(ob_ref[0].astype(F32)) * gsw_ref[...]).astype(BF16)
    mix = (jnp.dot(oa, w_ref[:NA_WIDTH], preferred_element_type=F32)
           + jnp.dot(ob, w_ref[NA_WIDTH:], preferred_element_type=F32))
    o_ref[0] = x_ref[0] + gate_ref[0, 0] * mix


def _out_proj(x, oa, ob, gna, gsw, w, mod):
    bsz, s, d = x.shape
    tm = OUT_TM
    row = lambda b, t: (b, t, 0)
    return pl.pallas_call(
        _out_proj_kernel,
        grid=(bsz, s // tm),
        in_specs=[pl.BlockSpec((1, tm, d), row),
                  pl.BlockSpec((1, tm, NA_WIDTH), row),
                  pl.BlockSpec((1, tm, SW_WIDTH), row),
                  pl.BlockSpec((1, NA_WIDTH), lambda b, t: (0, 0)),
                  pl.BlockSpec((1, SW_WIDTH), lambda b, t: (0, 0)),
                  pl.BlockSpec(w.shape, lambda b, t: (0, 0)),
                  _mod_spec(MOD_GATE_A, d)],
        out_specs=pl.BlockSpec((1, tm, d), row),
        out_shape=jax.ShapeDtypeStruct((bsz, s, d), F32),
        compiler_params=pltpu.CompilerParams(dimension_semantics=("arbitrary", "arbitrary"),
                                             vmem_limit_bytes=VMEM_LIMIT),
        name="out_proj",
    )(x, oa, ob, gna, gsw, w, mod)


def _ffn_kernel(x_ref, g_ref, scale_ref, shift_ref, gate_ref, wup_ref, cw_ref, cb_ref, wdn_ref, gfin_ref,
                o_ref, h_ref, *, row_chunk):
    s = x_ref.shape[1]
    nf = cw_ref.shape[0]
    nblk = s // FF_ROWS

    gain = g_ref[...] * (1.0 + scale_ref[0, 0])

    def prologue(i, carry):
        r0 = pl.multiple_of(i * row_chunk, row_chunk)
        xr = x_ref[0, pl.ds(r0, row_chunk), :]
        h_ref[pl.ds(r0, row_chunk), :] = (_rms(xr) * gain + shift_ref[0, 0]).astype(BF16)
        o_ref[0, pl.ds(r0, row_chunk), :] = jnp.zeros_like(xr)
        return carry

    lax.fori_loop(0, s // row_chunk, prologue, 0)

    ridx = lax.broadcasted_iota(jnp.int32, (FF_ROWS, FF_CHUNK), 0)
    edge = jnp.zeros((1, FF_CHUNK), F32)

    def rows(i):
        return slice(i * FF_ROWS, (i + 1) * FF_ROWS)

    def up(c, i):
        h = h_ref[rows(i), :]
        val_cols = pl.ds(pl.multiple_of(c * FF_CHUNK, FF_CHUNK), FF_CHUNK)
        gate_cols = pl.ds(pl.multiple_of((nf + c) * FF_CHUNK, FF_CHUNK), FF_CHUNK)
        return (jnp.dot(h, wup_ref[:, val_cols], preferred_element_type=F32),
                jnp.dot(h, wup_ref[:, gate_cols], preferred_element_type=F32))

    def act(c, i, cur, before, after):
        val, gt = cur
        before = before[1][FF_ROWS - 1:] if i > 0 else edge
        after = after[1][:1] if i < nblk - 1 else edge
        prev = jnp.where(ridx == 0, before, pltpu.roll(gt, 1, 0))
        nxt = jnp.where(ridx == FF_ROWS - 1, after, pltpu.roll(gt, FF_ROWS - 1, 0))
        cw = cw_ref[c]
        gc = prev * cw[0:1] + gt * cw[1:2] + nxt * cw[2:3] + cb_ref[c]
        return (gc * jax.nn.sigmoid(gc) * val).astype(BF16)

    def run(chunks, last):
        units = [(c, i) for c in chunks for i in range(nblk)]
        ups = {0: up(*units[0]), 1: up(*units[1])}
        w_rows = pl.ds(pl.multiple_of(chunks[0] * FF_CHUNK, FF_CHUNK), len(chunks) * FF_CHUNK)
        gated = {i: [] for i in range(nblk)}
        for u, (c, i) in enumerate(units):
            gated[i].append(act(c, i, ups[u], ups.get(u - 1), ups.get(u + 1)))
            if u >= len(units) - nblk:
                a = jnp.concatenate(gated.pop(i), axis=1)
                acc = o_ref[0, rows(i), :] + jnp.dot(a, wdn_ref[w_rows, :], preferred_element_type=F32)
                if last:
                    y = x_ref[0, rows(i), :] + gate_ref[0, 0] * acc
                    acc = _rms(y) * gfin_ref[...]
                o_ref[0, rows(i), :] = acc
            ups.pop(u - 1, None)
            if u + 2 < len(units):
                ups[u + 2] = up(*units[u + 2])

    def body(p, carry):
        run([FF_BODY_CHUNKS * p + j for j in range(FF_BODY_CHUNKS)], False)
        return carry

    nloop = (nf - 1) // FF_BODY_CHUNKS
    lax.fori_loop(0, nloop, body, 0)
    run(list(range(FF_BODY_CHUNKS * nloop, nf)), True)


def _ffn(x1, g, mod, wup, cw, cb, wdn, gfin):
    bsz, s, d = x1.shape
    whole = lambda a: pl.BlockSpec(a.shape, lambda b: (0,) * a.ndim, pipeline_mode=pl.Buffered(1))
    vec = lambda which: pl.BlockSpec((1, 1, 1, d), lambda b: (b, which, 0, 0))
    return pl.pallas_call(
        functools.partial(_ffn_kernel, row_chunk=256),
        grid=(bsz,),
        in_specs=[pl.BlockSpec((1, s, d), lambda b: (b, 0, 0)),
                  whole(g), vec(MOD_SCALE_F), vec(MOD_SHIFT_F), vec(MOD_GATE_F),
                  whole(wup), whole(cw), whole(cb), whole(wdn), whole(gfin)],
        out_specs=pl.BlockSpec((1, s, d), lambda b: (b, 0, 0)),
        out_shape=jax.ShapeDtypeStruct((bsz, s, d), F32),
        scratch_shapes=[pltpu.VMEM((s, d), BF16)],
        compiler_params=pltpu.CompilerParams(dimension_semantics=("arbitrary",),
                                             vmem_limit_bytes=FFN_VMEM_LIMIT),
        name="ffn",
    )(x1, g, mod, mod, mod, wup, cw, cb, wdn, gfin)


def _rope_tables(s):
    half = HEAD_DIM // 2
    inv = ROPE_THETA ** (-np.arange(half, dtype=np.float64) / half)
    ang = np.arange(s, dtype=np.float64)[:, None] * inv[None, :]
    cos = np.cos(ang)
    sin = np.sin(ang)
    reps = LANES // HEAD_DIM
    return (jnp.asarray(np.tile(np.concatenate([cos, cos], axis=-1), (1, reps)), F32),
            jnp.asarray(np.tile(np.concatenate([-sin, sin], axis=-1), (1, reps)), F32))


def _sw_head_order(t, axis):
    shape = t.shape
    t = t.reshape(shape[:axis] + (SW_KV_HEADS, SW_GROUP, -1) + shape[axis + 1:])
    return jnp.swapaxes(t, axis, axis + 1).reshape(shape)


def kernel(x, c, w_ada, b_ada, g_attn, w_in, na_rpb, sw_sink, g_na_out, g_sw_out, w_out, g_ffn, w_up,
           conv_w, conv_b, w_down, g_final):
    bsz, s, d = x.shape
    depth = w_ada.shape[0]
    d_ff = w_down.shape[1]
    assert depth == 1, "the final rmsnorm is fused into the (only) layer's ffn call"
    assert d_ff % FF_CHUNK == 0 and s % GRID_W == 0 and s % SW_BLOCK == 0
    cos, sin = _rope_tables(s)
    sw_mask = _sw_mask_table()
    qb0 = 3 * NA_WIDTH
    for l in range(depth):
        mod = _ada(c, w_ada[l], b_ada[l]).reshape(bsz, 6, 1, d)

        wi = w_in[l].astype(BF16)
        wi = jnp.concatenate([wi[:, :qb0], _sw_head_order(wi[:, qb0:qb0 + SW_WIDTH], 1),
                              wi[:, qb0 + SW_WIDTH:]], axis=1)
        qa, ka, va, qb, kvb = _in_proj(x, g_attn[l].reshape(1, d), mod, wi, cos, sin)

        o_a = _na_attention(qa, ka, va, _na_bias_table(na_rpb[l].astype(F32) * LOG2E, s // GRID_W))
        o_b = _sw_attention(_sw_head_order(sw_sink[l].astype(F32) * LOG2E, 0), qb, kvb, sw_mask)

        wo = w_out[l].astype(BF16)
        wo = jnp.concatenate([wo[:NA_WIDTH], _sw_head_order(wo[NA_WIDTH:], 0)], axis=0)
        x = _out_proj(x, o_a, o_b, g_na_out[l].reshape(1, -1), _sw_head_order(g_sw_out[l], 0).reshape(1, -1),
                      wo, mod)

        nf = d_ff // FF_CHUNK
        cw = conv_w[l].reshape(CONV_W, nf, FF_CHUNK).transpose(1, 0, 2)
        x = _ffn(x, g_ffn[l].reshape(1, d), mod, w_up[l].astype(BF16), cw, conv_b[l].reshape(nf, 1, FF_CHUNK),
                 w_down[l].astype(BF16), g_final.reshape(1, d))
    return x
```

```python
import functools

import jax
import jax.numpy as jnp
import numpy as np
from jax import lax
from jax.experimental import pallas as pl
from jax.experimental.pallas import tpu as pltpu

F32 = jnp.float32
BF16 = jnp.bfloat16

HEAD_DIM = 64
LANES = 128
NA_HEADS = 8
NA_WIDTH = NA_HEADS * HEAD_DIM
GRID_W = 64
NA_ROWS_WIN = 8
NA_COLS = 16
SW_HEADS = 8
SW_KV_HEADS = 2
SW_GROUP = SW_HEADS // SW_KV_HEADS
SW_WIDTH = SW_HEADS * HEAD_DIM
SW_KV_WIDTH = SW_KV_HEADS * HEAD_DIM
SW_BLOCK = 128
CONV_W = 3
ROPE_THETA = 10000.0
EPS = 1e-6
NEG = -1e30
LOG2E = 1.4426950408889634
Q_SCALE = HEAD_DIM ** -0.5 * LOG2E

VMEM_LIMIT = 56 * 1024 * 1024
FFN_VMEM_LIMIT = 63 * 1024 * 1024

ADA_TN = 2048
IN_TM = 2048
IN_ROWS = 256
FF_CHUNK = 256
FF_BODY_CHUNKS = 3
FF_ROWS = 512
NA_GROUP = 32
OUT_LAG = 3
OUT_ROWS = 256
SW_STEP_BLOCKS = 8
SW_AHEAD = 2
NA_AHEAD = 3


def _rms(x):
    return x * lax.rsqrt(jnp.mean(x * x, axis=-1, keepdims=True) + EPS)


def _nt_dot(a, b):
    return lax.dot_general(a, b, (((1,), (1,)), ((), ())), preferred_element_type=F32)


def _lo_mask(rows):
    return lax.broadcasted_iota(jnp.int32, (rows, LANES), 1) < HEAD_DIM


def _split_bf16(a):
    hi = a.astype(BF16)
    return hi, (a - hi.astype(F32)).astype(BF16)


def _ada_kernel(c_ref, w_ref, b_ref, o_ref):
    c = c_ref[...]
    s_hi, s_lo = _split_bf16(c * jax.nn.sigmoid(c))
    w_hi, w_lo = _split_bf16(w_ref[...])
    dot = functools.partial(jnp.dot, preferred_element_type=F32)
    o_ref[...] = dot(s_hi, w_hi) + dot(s_hi, w_lo) + dot(s_lo, w_hi) + b_ref[...]


def _ada(c, w, b):
    bsz, d = c.shape
    n = w.shape[1]
    tn = ADA_TN
    return pl.pallas_call(
        _ada_kernel,
        grid=(n // tn,),
        in_specs=[pl.BlockSpec((bsz, d), lambda j: (0, 0)),
                  pl.BlockSpec((d, tn), lambda j: (0, j)),
                  pl.BlockSpec((1, tn), lambda j: (0, j))],
        out_specs=pl.BlockSpec((bsz, tn), lambda j: (0, j)),
        out_shape=jax.ShapeDtypeStruct((bsz, n), F32),
        compiler_params=pltpu.CompilerParams(dimension_semantics=("arbitrary",),
                                             vmem_limit_bytes=VMEM_LIMIT),
        name="ada",
    )(c, w, b.reshape(1, n))


MOD_SHIFT_A, MOD_SCALE_A, MOD_GATE_A, MOD_SHIFT_F, MOD_SCALE_F, MOD_GATE_F = range(6)


def _mod_spec(which, d):
    return pl.BlockSpec((1, 1, 1, d), lambda b, t: (b, which, 0, 0))


def _rope(t, cos, sin_signed, first_half):
    rot = jnp.where(first_half, pltpu.roll(t, LANES - HEAD_DIM // 2, 1), pltpu.roll(t, HEAD_DIM // 2, 1))
    return t * cos + rot * sin_signed


def _in_proj_kernel(x_ref, g_ref, scale_ref, shift_ref, w_ref, cos_ref, sin_ref,
                    qa_ref, ka_ref, va_ref, qb_ref, kvb_ref):
    gain = g_ref[...]
    scale = 1.0 + scale_ref[0, 0]
    shift = shift_ref[0, 0]
    first_half = (lax.broadcasted_iota(jnp.int32, (IN_ROWS, LANES), 1) % HEAD_DIM) < HEAD_DIM // 2
    base = 3 * NA_WIDTH

    for i in range(x_ref.shape[1] // IN_ROWS):
        rows = slice(i * IN_ROWS, (i + 1) * IN_ROWS)
        h = ((_rms(x_ref[0, rows, :]) * gain) * scale + shift).astype(BF16)
        cos = cos_ref[rows, :]
        sin = sin_ref[rows, :]

        def proj(lo, hi):
            return jnp.dot(h, w_ref[:, lo:hi], preferred_element_type=F32)

        qa_ref[0, rows, :] = (proj(0, NA_WIDTH) * Q_SCALE).astype(BF16)
        ka_ref[0, rows, :] = proj(NA_WIDTH, 2 * NA_WIDTH).astype(BF16)
        va_ref[0, rows, :] = proj(2 * NA_WIDTH, 3 * NA_WIDTH).astype(BF16)
        qb = proj(base, base + SW_WIDTH)
        for j in range(SW_WIDTH // LANES):
            blk = _rope(qb[:, j * LANES:(j + 1) * LANES], cos, sin, first_half) * Q_SCALE
            qb_ref[0, rows, j * LANES:(j + 1) * LANES] = blk.astype(BF16)
        kv = proj(base + SW_WIDTH, base + SW_WIDTH + 2 * SW_KV_WIDTH)
        kvb_ref[0, rows, :LANES] = _rope(kv[:, :LANES], cos, sin, first_half).astype(BF16)
        kvb_ref[0, rows, LANES:] = kv[:, LANES:].astype(BF16)


def _in_proj(x, g, mod, w, cos, sin):
    bsz, s, d = x.shape
    n = w.shape[1]
    tm = IN_TM
    row = lambda b, t: (b, t, 0)
    outs = [jax.ShapeDtypeStruct((bsz, s, NA_WIDTH), BF16)] * 3 + [
        jax.ShapeDtypeStruct((bsz, s, SW_WIDTH), BF16),
        jax.ShapeDtypeStruct((bsz, s, 2 * SW_KV_WIDTH), BF16)]
    return pl.pallas_call(
        _in_proj_kernel,
        grid=(bsz, s // tm),
        in_specs=[pl.BlockSpec((1, tm, d), row),
                  pl.BlockSpec((1, d), lambda b, t: (0, 0)),
                  _mod_spec(MOD_SCALE_A, d),
                  _mod_spec(MOD_SHIFT_A, d),
                  pl.BlockSpec((d, n), lambda b, t: (0, 0)),
                  pl.BlockSpec((tm, LANES), lambda b, t: (t, 0)),
                  pl.BlockSpec((tm, LANES), lambda b, t: (t, 0))],
        out_specs=[pl.BlockSpec((1, tm, NA_WIDTH), row)] * 3 + [
            pl.BlockSpec((1, tm, SW_WIDTH), row),
            pl.BlockSpec((1, tm, 2 * SW_KV_WIDTH), row)],
        out_shape=outs,
        compiler_params=pltpu.CompilerParams(dimension_semantics=("arbitrary", "arbitrary"),
                                             vmem_limit_bytes=VMEM_LIMIT),
        name="in_proj",
    )(x, g, mod, mod, w, cos, sin)


def _na_kernel(q_ref, k_ref, v_ref, bias_ref, o_ref, *, rows):
    lo = _lo_mask(GRID_W)
    win = NA_ROWS_WIN * GRID_W
    ones = jnp.ones((win, LANES), BF16)

    def scores(r):
        rs = jnp.clip(r - NA_ROWS_WIN // 2, 0, rows - NA_ROWS_WIN)
        q = q_ref[0, pl.ds(pl.multiple_of(r * GRID_W, GRID_W), GRID_W), :]
        k0 = pl.multiple_of(rs * GRID_W, GRID_W)
        zero = jnp.zeros_like(q)
        qs = jnp.concatenate([jnp.where(lo, q, zero), jnp.where(lo, zero, q)], axis=0)
        dr0 = NA_ROWS_WIN - 1 - (r - rs)
        bias = jnp.concatenate(
            [jnp.concatenate([bias_ref[head, dr0 + 2 * jj] for jj in range(NA_ROWS_WIN // 2)], axis=1)
             for head in range(2)], axis=0)
        return _nt_dot(qs, k_ref[0, pl.ds(k0, win), :]) + bias, k0

    def finish(r, s, k0):
        p = jnp.exp2(s - jnp.max(s, axis=-1, keepdims=True)).astype(BF16)
        oa = jnp.dot(p, jnp.concatenate([v_ref[0, pl.ds(k0, win), :], ones], axis=1), preferred_element_type=F32)
        o = oa[:, :LANES] * (1.0 / oa[:, LANES:])
        out = jnp.where(lo, o[:GRID_W], o[GRID_W:])
        o_ref[0, pl.ds(pl.multiple_of(r * GRID_W, GRID_W), GRID_W), :] = out.astype(o_ref.dtype)

    def body(g, carry):
        base = g * NA_GROUP
        staged = [scores(base + i) for i in range(NA_AHEAD)]
        for i in range(NA_GROUP):
            if i + NA_AHEAD < NA_GROUP:
                staged.append(scores(base + i + NA_AHEAD))
            finish(base + i, *staged[i])
        return carry

    lax.fori_loop(0, rows // NA_GROUP, body, 0)


def _na_attention(q, k, v, bias):
    bsz, s, _ = q.shape
    rows = s // GRID_W
    pairs = NA_WIDTH // LANES
    blk = pl.BlockSpec((1, s, LANES), lambda hp, b: (b, 0, hp))
    return pl.pallas_call(
        functools.partial(_na_kernel, rows=rows),
        grid=(pairs, bsz),
        in_specs=[blk, blk, blk,
                  pl.BlockSpec((2,) + bias.shape[1:], lambda hp, b: (hp, 0, 0, 0))],
        out_specs=blk,
        out_shape=jax.ShapeDtypeStruct((bsz, s, NA_WIDTH), BF16),
        compiler_params=pltpu.CompilerParams(dimension_semantics=("arbitrary", "arbitrary"),
                                             vmem_limit_bytes=VMEM_LIMIT),
        name="na_attn",
    )(q, k, v, bias)


def _na_bias_table(rpb, rows):
    assert rows >= 2 * NA_ROWS_WIN
    col = np.arange(GRID_W)
    cs = np.clip(col - NA_COLS // 2, 0, GRID_W - NA_COLS)
    col_ok = (col[None, :] >= cs[:, None]) & (col[None, :] < cs[:, None] + NA_COLS)
    dc = np.clip(col[None, :] - col[:, None] + NA_COLS - 1, 0, 2 * NA_COLS - 2)
    onehot = (np.arange(2 * NA_COLS - 1)[:, None] == dc.reshape(1, -1)).astype(np.float32)
    t = jnp.dot(rpb.reshape(-1, 2 * NA_COLS - 1), onehot, precision=lax.Precision.HIGHEST)
    t = jnp.where(col_ok[None, None], t.reshape(NA_HEADS, 2 * NA_ROWS_WIN - 1, GRID_W, GRID_W), NEG)
    return jnp.concatenate([t[:, :-1], t[:, 1:]], axis=-1)


def _sw_out_kernel(sink_ref, q_ref, k_ref, v_ref, mask_ref, x_ref, oa_ref, gna_ref, gsw_ref, w_ref, gate_ref,
                   o_ref, ob_ref, *, nblocks):
    lo = _lo_mask(SW_BLOCK)
    zero = jnp.zeros((SW_BLOCK, LANES), q_ref.dtype)
    ones = jnp.ones((3 * SW_BLOCK, LANES), BF16)
    ri = lax.broadcasted_iota(jnp.int32, (2 * SW_BLOCK, LANES), 0)
    ci = lax.broadcasted_iota(jnp.int32, (2 * SW_BLOCK, LANES), 1)
    eye2 = jnp.where((ri == ci) | (ri == ci + SW_BLOCK), 1.0, 0.0).astype(BF16)
    npairs = SW_WIDTH // LANES

    def window(u):
        n = pl.program_id(1) * SW_STEP_BLOCKS + u
        start = pl.multiple_of(jnp.clip(n - 1, 0, nblocks - 3) * SW_BLOCK, SW_BLOCK)
        variant = jnp.where(n == 0, 0, jnp.where(n == nblocks - 1, 2, 1))
        return start, variant

    def scores(u, j):
        start, variant = window(u)
        qj = q_ref[0, u * SW_BLOCK:(u + 1) * SW_BLOCK, j * LANES:(j + 1) * LANES]
        qs = jnp.concatenate([jnp.where(lo, qj, zero), jnp.where(lo, zero, qj)], axis=0)
        return _nt_dot(jnp.concatenate([qs, eye2], axis=1),
                       jnp.concatenate([k_ref[0, pl.ds(start, 3 * SW_BLOCK), :], mask_ref[variant]], axis=1))

    def finish(u, j, s):
        start, _ = window(u)
        ps, sinks = [], []
        for half in range(2):
            sg = s[half * SW_BLOCK:(half + 1) * SW_BLOCK]
            sk = sink_ref[2 * j + half]
            m = jnp.maximum(jnp.max(sg, axis=-1, keepdims=True), sk)
            ps.append(jnp.exp2(sg - m).astype(BF16))
            sinks.append(jnp.exp2(sk - m))
        oa = jnp.dot(jnp.concatenate(ps, axis=0),
                     jnp.concatenate([v_ref[0, pl.ds(start, 3 * SW_BLOCK), :], ones], axis=1),
                     preferred_element_type=F32)
        o = oa[:, :LANES] * (1.0 / (oa[:, LANES:] + jnp.concatenate(sinks, axis=0)))
        ob_ref[u * SW_BLOCK:(u + 1) * SW_BLOCK, j * LANES:(j + 1) * LANES] = jnp.where(lo, o[:SW_BLOCK], o[SW_BLOCK:])

    def project(t):
        rows = slice(t * OUT_ROWS, (t + 1) * OUT_ROWS)
        oa = (_rms(oa_ref[0, rows, :].astype(F32)) * gna_ref[...]).astype(BF16)
        ob = (_rms(ob_ref[rows, :]) * gsw_ref[...]).astype(BF16)
        mix = (jnp.dot(oa, w_ref[:NA_WIDTH], preferred_element_type=F32)
               + jnp.dot(ob, w_ref[NA_WIDTH:], preferred_element_type=F32))
        o_ref[0, rows, :] = x_ref[0, rows, :] + gate_ref[0, 0] * mix

    per_tile = npairs * OUT_ROWS // SW_BLOCK
    units = [(u, j) for u in range(SW_STEP_BLOCKS) for j in range(npairs)]
    staged = [scores(*units[i]) for i in range(SW_AHEAD)]
    for i, unit in enumerate(units):
        if i + SW_AHEAD < len(units):
            staged.append(scores(*units[i + SW_AHEAD]))
        finish(*unit, staged[i])
        staged[i] = None
        if i % per_tile == OUT_LAG and i > per_tile:
            project(i // per_tile - 1)
    project(len(units) // per_tile - 1)


def _sw_attention_out_proj(sink, q, kv, mask, x, oa, gna, gsw, w, mod):
    bsz, s, d = x.shape
    nblocks = s // SW_BLOCK
    assert nblocks >= 3 and nblocks % SW_STEP_BLOCKS == 0
    rows = SW_STEP_BLOCKS * SW_BLOCK
    tile = lambda b, n, sk: (b, n, 0)
    const = lambda a: pl.BlockSpec(a.shape, lambda b, n, sk: (0,) * a.ndim)
    grid_spec = pltpu.PrefetchScalarGridSpec(
        num_scalar_prefetch=1,
        grid=(bsz, nblocks // SW_STEP_BLOCKS),
        in_specs=[pl.BlockSpec((1, rows, SW_WIDTH), tile),
                  pl.BlockSpec((1, s, LANES), lambda b, n, sk: (b, 0, 0)),
                  pl.BlockSpec((1, s, LANES), lambda b, n, sk: (b, 0, 1)),
                  const(mask),
                  pl.BlockSpec((1, rows, d), tile),
                  pl.BlockSpec((1, rows, NA_WIDTH), tile),
                  const(gna), const(gsw), const(w),
                  pl.BlockSpec((1, 1, 1, d), lambda b, n, sk: (b, MOD_GATE_A, 0, 0))],
        out_specs=pl.BlockSpec((1, rows, d), tile),
        scratch_shapes=[pltpu.VMEM((rows, SW_WIDTH), F32)],
    )
    return pl.pallas_call(
        functools.partial(_sw_out_kernel, nblocks=nblocks),
        grid_spec=grid_spec,
        out_shape=jax.ShapeDtypeStruct((bsz, s, d), F32),
        compiler_params=pltpu.CompilerParams(dimension_semantics=("arbitrary", "arbitrary"),
                                             vmem_limit_bytes=VMEM_LIMIT),
        name="sw_attn_out_proj",
    )(sink, q, kv, kv, mask, x, oa, gna, gsw, w, mod)


def _sw_mask_table():
    i = np.arange(SW_BLOCK)[:, None]
    j = np.arange(3 * SW_BLOCK)[None, :]
    ok = np.stack([np.abs(i - j) <= SW_BLOCK,
                   np.abs(i + SW_BLOCK - j) <= SW_BLOCK,
                   np.abs(i + 2 * SW_BLOCK - j) <= SW_BLOCK])
    return jnp.asarray(np.where(ok, 0.0, NEG).transpose(0, 2, 1), BF16)


def _ffn_kernel(x_ref, g_ref, scale_ref, shift_ref, gate_ref, wup_ref, cw_ref, cb_ref, wdn_ref, gfin_ref,
                o_ref, h_ref, *, row_chunk):
    s = x_ref.shape[1]
    nf = cw_ref.shape[0]
    nblk = s // FF_ROWS

    gain = g_ref[...] * (1.0 + scale_ref[0, 0])

    def prologue(i, carry):
        r0 = pl.multiple_of(i * row_chunk, row_chunk)
        xr = x_ref[0, pl.ds(r0, row_chunk), :]
        h_ref[pl.ds(r0, row_chunk), :] = (_rms(xr) * gain + shift_ref[0, 0]).astype(BF16)
        o_ref[0, pl.ds(r0, row_chunk), :] = jnp.zeros_like(xr)
        return carry

    lax.fori_loop(0, s // row_chunk, prologue, 0)

    ridx = lax.broadcasted_iota(jnp.int32, (FF_ROWS, FF_CHUNK), 0)
    edge = jnp.zeros((1, FF_CHUNK), F32)

    def rows(i):
        return slice(i * FF_ROWS, (i + 1) * FF_ROWS)

    def up(c, i):
        h = h_ref[rows(i), :]
        val_cols = pl.ds(pl.multiple_of(c * FF_CHUNK, FF_CHUNK), FF_CHUNK)
        gate_cols = pl.ds(pl.multiple_of((nf + c) * FF_CHUNK, FF_CHUNK), FF_CHUNK)
        return (jnp.dot(h, wup_ref[:, val_cols], preferred_element_type=F32),
                jnp.dot(h, wup_ref[:, gate_cols], preferred_element_type=F32))

    def act(c, i, cur, before, after):
        val, gt = cur
        before = before[1][FF_ROWS - 1:] if i > 0 else edge
        after = after[1][:1] if i < nblk - 1 else edge
        prev = jnp.where(ridx == 0, before, pltpu.roll(gt, 1, 0))
        nxt = jnp.where(ridx == FF_ROWS - 1, after, pltpu.roll(gt, FF_ROWS - 1, 0))
        cw = cw_ref[c]
        gc = prev * cw[0:1] + gt * cw[1:2] + nxt * cw[2:3] + cb_ref[c]
        return (gc * jax.nn.sigmoid(gc) * val).astype(BF16)

    def run(chunks, last):
        units = [(c, i) for c in chunks for i in range(nblk)]
        ups = {0: up(*units[0]), 1: up(*units[1])}
        w_rows = pl.ds(pl.multiple_of(chunks[0] * FF_CHUNK, FF_CHUNK), len(chunks) * FF_CHUNK)
        gated = {i: [] for i in range(nblk)}
        for u, (c, i) in enumerate(units):
            gated[i].append(act(c, i, ups[u], ups.get(u - 1), ups.get(u + 1)))
            if u >= len(units) - nblk:
                a = jnp.concatenate(gated.pop(i), axis=1)
                acc = o_ref[0, rows(i), :] + jnp.dot(a, wdn_ref[w_rows, :], preferred_element_type=F32)
                if last:
                    y = x_ref[0, rows(i), :] + gate_ref[0, 0] * acc
                    acc = _rms(y) * gfin_ref[...]
                o_ref[0, rows(i), :] = acc
            ups.pop(u - 1, None)
            if u + 2 < len(units):
                ups[u + 2] = up(*units[u + 2])

    def body(p, carry):
        run([FF_BODY_CHUNKS * p + j for j in range(FF_BODY_CHUNKS)], False)
        return carry

    nloop = (nf - 1) // FF_BODY_CHUNKS
    lax.fori_loop(0, nloop, body, 0)
    run(list(range(FF_BODY_CHUNKS * nloop, nf)), True)


def _ffn(x1, g, mod, wup, cw, cb, wdn, gfin):
    bsz, s, d = x1.shape
    whole = lambda a: pl.BlockSpec(a.shape, lambda b: (0,) * a.ndim, pipeline_mode=pl.Buffered(1))
    vec = lambda which: pl.BlockSpec((1, 1, 1, d), lambda b: (b, which, 0, 0))
    return pl.pallas_call(
        functools.partial(_ffn_kernel, row_chunk=256),
        grid=(bsz,),
        in_specs=[pl.BlockSpec((1, s, d), lambda b: (b, 0, 0)),
                  whole(g), vec(MOD_SCALE_F), vec(MOD_SHIFT_F), vec(MOD_GATE_F),
                  whole(wup), whole(cw), whole(cb), whole(wdn), whole(gfin)],
        out_specs=pl.BlockSpec((1, s, d), lambda b: (b, 0, 0)),
        out_shape=jax.ShapeDtypeStruct((bsz, s, d), F32),
        scratch_shapes=[pltpu.VMEM((s, d), BF16)],
        compiler_params=pltpu.CompilerParams(dimension_semantics=("arbitrary",),
                                             vmem_limit_bytes=FFN_VMEM_LIMIT),
        name="ffn",
    )(x1, g, mod, mod, mod, wup, cw, cb, wdn, gfin)


def _rope_tables(s):
    half = HEAD_DIM // 2
    inv = ROPE_THETA ** (-np.arange(half, dtype=np.float64) / half)
    ang = np.arange(s, dtype=np.float64)[:, None] * inv[None, :]
    cos = np.cos(ang)
    sin = np.sin(ang)
    reps = LANES // HEAD_DIM
    return (jnp.asarray(np.tile(np.concatenate([cos, cos], axis=-1), (1, reps)), F32),
            jnp.asarray(np.tile(np.concatenate([-sin, sin], axis=-1), (1, reps)), F32))


def _sw_head_order(t, axis):
    shape = t.shape
    t = t.reshape(shape[:axis] + (SW_KV_HEADS, SW_GROUP, -1) + shape[axis + 1:])
    return jnp.swapaxes(t, axis, axis + 1).reshape(shape)


def kernel(x, c, w_ada, b_ada, g_attn, w_in, na_rpb, sw_sink, g_na_out, g_sw_out, w_out, g_ffn, w_up,
           conv_w, conv_b, w_down, g_final):
    bsz, s, d = x.shape
    depth = w_ada.shape[0]
    d_ff = w_down.shape[1]
    assert depth == 1, "the final rmsnorm is fused into the (only) layer's ffn call"
    assert d_ff % FF_CHUNK == 0 and s % GRID_W == 0 and s % SW_BLOCK == 0
    cos, sin = _rope_tables(s)
    sw_mask = _sw_mask_table()
    qb0 = 3 * NA_WIDTH
    for l in range(depth):
        mod = _ada(c, w_ada[l], b_ada[l]).reshape(bsz, 6, 1, d)

        wi = w_in[l].astype(BF16)
        wi = jnp.concatenate([wi[:, :qb0], _sw_head_order(wi[:, qb0:qb0 + SW_WIDTH], 1),
                              wi[:, qb0 + SW_WIDTH:]], axis=1)
        qa, ka, va, qb, kvb = _in_proj(x, g_attn[l].reshape(1, d), mod, wi, cos, sin)

        o_a = _na_attention(qa, ka, va, _na_bias_table(na_rpb[l].astype(F32) * LOG2E, s // GRID_W))
        wo = w_out[l].astype(BF16)
        wo = jnp.concatenate([wo[:NA_WIDTH], _sw_head_order(wo[NA_WIDTH:], 0)], axis=0)
        x = _sw_attention_out_proj(_sw_head_order(sw_sink[l].astype(F32) * LOG2E, 0), qb, kvb, sw_mask, x, o_a,
                                   g_na_out[l].reshape(1, -1), _sw_head_order(g_sw_out[l], 0).reshape(1, -1), wo, mod)

        nf = d_ff // FF_CHUNK
        cw = conv_w[l].reshape(CONV_W, nf, FF_CHUNK).transpose(1, 0, 2)
        x = _ffn(x, g_ffn[l].reshape(1, d), mod, w_up[l].astype(BF16), cw, conv_b[l].reshape(nf, 1, FF_CHUNK),
                 w_down[l].astype(BF16), g_final.reshape(1, d))
    return x
```

```python
import functools

import jax
import jax.numpy as jnp
import numpy as np
from jax import lax
from jax.experimental import pallas as pl
from jax.experimental.pallas import tpu as pltpu

F32 = jnp.float32
BF16 = jnp.bfloat16

HEAD_DIM = 64
LANES = 128
NA_HEADS = 8
NA_WIDTH = NA_HEADS * HEAD_DIM
GRID_W = 64
NA_ROWS_WIN = 8
NA_COLS = 16
SW_HEADS = 8
SW_KV_HEADS = 2
SW_GROUP = SW_HEADS // SW_KV_HEADS
SW_WIDTH = SW_HEADS * HEAD_DIM
SW_KV_WIDTH = SW_KV_HEADS * HEAD_DIM
SW_BLOCK = 128
CONV_W = 3
ROPE_THETA = 10000.0
EPS = 1e-6
NEG = -1e30
LOG2E = 1.4426950408889634
Q_SCALE = HEAD_DIM ** -0.5 * LOG2E

VMEM_LIMIT = 56 * 1024 * 1024
FFN_VMEM_LIMIT = 63 * 1024 * 1024

ADA_TN = 2048
IN_TM = 2048
IN_ROWS = 256
FF_CHUNK = 256
FF_BODY_CHUNKS = 3
FF_ROWS = 512
NA_GROUP = 32
NA_AHEAD = 4
SW_STEP_BLOCKS = 8
SW_AHEAD = 3
OUT_ROWS = 256
OUT_LAG = 3


def _rms(x):
    return x * lax.rsqrt(jnp.mean(x * x, axis=-1, keepdims=True) + EPS)


def _nt_dot(a, b):
    return lax.dot_general(a, b, (((1,), (1,)), ((), ())), preferred_element_type=F32)


def _lo_mask(rows):
    return lax.broadcasted_iota(jnp.int32, (rows, LANES), 1) < HEAD_DIM


def _split_bf16(a):
    hi = a.astype(BF16)
    return hi, (a - hi.astype(F32)).astype(BF16)


def _ada_kernel(c_ref, w_ref, b_ref, o_ref):
    c = c_ref[...]
    s_hi, s_lo = _split_bf16(c * jax.nn.sigmoid(c))
    w_hi, w_lo = _split_bf16(w_ref[...])
    dot = functools.partial(jnp.dot, preferred_element_type=F32)
    o_ref[...] = dot(s_hi, w_hi) + dot(s_hi, w_lo) + dot(s_lo, w_hi) + b_ref[...]


def _ada(c, w, b):
    bsz, d = c.shape
    n = w.shape[1]
    tn = ADA_TN
    return pl.pallas_call(
        _ada_kernel,
        grid=(n // tn,),
        in_specs=[pl.BlockSpec((bsz, d), lambda j: (0, 0)),
                  pl.BlockSpec((d, tn), lambda j: (0, j)),
                  pl.BlockSpec((1, tn), lambda j: (0, j))],
        out_specs=pl.BlockSpec((bsz, tn), lambda j: (0, j)),
        out_shape=jax.ShapeDtypeStruct((bsz, n), F32),
        compiler_params=pltpu.CompilerParams(dimension_semantics=("arbitrary",),
                                             vmem_limit_bytes=VMEM_LIMIT),
        name="ada",
    )(c, w, b.reshape(1, n))


MOD_SHIFT_A, MOD_SCALE_A, MOD_GATE_A, MOD_SHIFT_F, MOD_SCALE_F, MOD_GATE_F = range(6)


def _mod_spec(which, d):
    return pl.BlockSpec((1, 1, 1, d), lambda b, t: (b, which, 0, 0))


def _rope(t, cos, sin_signed, first_half):
    rot = jnp.where(first_half, pltpu.roll(t, LANES - HEAD_DIM // 2, 1), pltpu.roll(t, HEAD_DIM // 2, 1))
    return t * cos + rot * sin_signed


def _in_proj_kernel(x_ref, g_ref, scale_ref, shift_ref, w_ref, cos_ref, sin_ref,
                    qa_ref, ka_ref, va_ref, qb_ref, kvb_ref):
    gain = g_ref[...]
    scale = 1.0 + scale_ref[0, 0]
    shift = shift_ref[0, 0]
    first_half = (lax.broadcasted_iota(jnp.int32, (IN_ROWS, LANES), 1) % HEAD_DIM) < HEAD_DIM // 2
    base = 3 * NA_WIDTH

    for i in range(x_ref.shape[1] // IN_ROWS):
        rows = slice(i * IN_ROWS, (i + 1) * IN_ROWS)
        h = ((_rms(x_ref[0, rows, :]) * gain) * scale + shift).astype(BF16)
        cos = cos_ref[rows, :]
        sin = sin_ref[rows, :]

        def proj(lo, hi):
            return jnp.dot(h, w_ref[:, lo:hi], preferred_element_type=F32)

        for ref, col0, mult in ((qa_ref, 0, Q_SCALE), (ka_ref, NA_WIDTH, None), (va_ref, 2 * NA_WIDTH, None)):
            t = proj(col0, col0 + NA_WIDTH)
            t = t if mult is None else t * mult
            for hp in range(NA_WIDTH // LANES):
                ref[0, hp, rows, :] = t[:, hp * LANES:(hp + 1) * LANES].astype(BF16)
        qb = proj(base, base + SW_WIDTH)
        for j in range(SW_WIDTH // LANES):
            blk = _rope(qb[:, j * LANES:(j + 1) * LANES], cos, sin, first_half) * Q_SCALE
            qb_ref[0, rows, j * LANES:(j + 1) * LANES] = blk.astype(BF16)
        kv = proj(base + SW_WIDTH, base + SW_WIDTH + 2 * SW_KV_WIDTH)
        kvb_ref[0, rows, :LANES] = _rope(kv[:, :LANES], cos, sin, first_half).astype(BF16)
        kvb_ref[0, rows, LANES:] = kv[:, LANES:].astype(BF16)


def _in_proj(x, g, mod, w, cos, sin):
    bsz, s, d = x.shape
    n = w.shape[1]
    tm = IN_TM
    row = lambda b, t: (b, t, 0)
    pairs = NA_WIDTH // LANES
    outs = [jax.ShapeDtypeStruct((bsz, pairs, s, LANES), BF16)] * 3 + [
        jax.ShapeDtypeStruct((bsz, s, SW_WIDTH), BF16),
        jax.ShapeDtypeStruct((bsz, s, 2 * SW_KV_WIDTH), BF16)]
    return pl.pallas_call(
        _in_proj_kernel,
        grid=(bsz, s // tm),
        in_specs=[pl.BlockSpec((1, tm, d), row),
                  pl.BlockSpec((1, d), lambda b, t: (0, 0)),
                  _mod_spec(MOD_SCALE_A, d),
                  _mod_spec(MOD_SHIFT_A, d),
                  pl.BlockSpec((d, n), lambda b, t: (0, 0)),
                  pl.BlockSpec((tm, LANES), lambda b, t: (t, 0)),
                  pl.BlockSpec((tm, LANES), lambda b, t: (t, 0))],
        out_specs=[pl.BlockSpec((1, pairs, tm, LANES), lambda b, t: (b, 0, t, 0))] * 3 + [
            pl.BlockSpec((1, tm, SW_WIDTH), row),
            pl.BlockSpec((1, tm, 2 * SW_KV_WIDTH), row)],
        out_shape=outs,
        compiler_params=pltpu.CompilerParams(dimension_semantics=("arbitrary", "arbitrary"),
                                             vmem_limit_bytes=VMEM_LIMIT),
        name="in_proj",
    )(x, g, mod, mod, w, cos, sin)


def _na_kernel(q_ref, k_ref, v_ref, bias_ref, o_ref, *, rows):
    lo = _lo_mask(GRID_W)
    win = NA_ROWS_WIN * GRID_W
    ones = jnp.ones((win, LANES), BF16)

    def scores(hp, r):
        rs = jnp.clip(r - NA_ROWS_WIN // 2, 0, rows - NA_ROWS_WIN)
        q = q_ref[0, hp, pl.ds(pl.multiple_of(r * GRID_W, GRID_W), GRID_W), :]
        k0 = pl.multiple_of(rs * GRID_W, GRID_W)
        zero = jnp.zeros_like(q)
        qs = jnp.concatenate([jnp.where(lo, q, zero), jnp.where(lo, zero, q)], axis=0)
        dr0 = NA_ROWS_WIN - 1 - (r - rs)
        bias = jnp.concatenate(
            [jnp.concatenate([bias_ref[2 * hp + head, dr0 + 2 * jj] for jj in range(NA_ROWS_WIN // 2)], axis=1)
             for head in range(2)], axis=0)
        return _nt_dot(qs, k_ref[0, hp, pl.ds(k0, win), :]) + bias, k0

    def finish(hp, r, s, k0):
        p = jnp.exp2(s - jnp.max(s, axis=-1, keepdims=True)).astype(BF16)
        oa = jnp.dot(p, jnp.concatenate([v_ref[0, hp, pl.ds(k0, win), :], ones], axis=1),
                     preferred_element_type=F32)
        o = oa[:, :LANES] * (1.0 / oa[:, LANES:])
        out = jnp.where(lo, o[:GRID_W], o[GRID_W:])
        o_ref[0, hp, pl.ds(pl.multiple_of(r * GRID_W, GRID_W), GRID_W), :] = out.astype(o_ref.dtype)

    def body(it, carry):
        hp = it // (rows // NA_GROUP)
        base = (it % (rows // NA_GROUP)) * NA_GROUP
        staged = [scores(hp, base + i) for i in range(NA_AHEAD)]
        for i in range(NA_GROUP):
            if i + NA_AHEAD < NA_GROUP:
                staged.append(scores(hp, base + i + NA_AHEAD))
            finish(hp, base + i, *staged[i])
        return carry

    lax.fori_loop(0, q_ref.shape[1] * (rows // NA_GROUP), body, 0)


def _na_attention(q, k, v, bias):
    bsz, pairs, s, _ = q.shape
    rows = s // GRID_W
    blk = pl.BlockSpec((1, pairs, s, LANES), lambda b: (b, 0, 0, 0))
    return pl.pallas_call(
        functools.partial(_na_kernel, rows=rows),
        grid=(bsz,),
        in_specs=[blk, blk, blk, pl.BlockSpec(bias.shape, lambda b: (0, 0, 0, 0))],
        out_specs=blk,
        out_shape=jax.ShapeDtypeStruct(q.shape, BF16),
        compiler_params=pltpu.CompilerParams(dimension_semantics=("arbitrary",),
                                             vmem_limit_bytes=VMEM_LIMIT),
        name="na_attn",
    )(q, k, v, bias)


def _na_bias_table(rpb, rows):
    assert rows >= 2 * NA_ROWS_WIN
    col = np.arange(GRID_W)
    cs = np.clip(col - NA_COLS // 2, 0, GRID_W - NA_COLS)
    col_ok = (col[None, :] >= cs[:, None]) & (col[None, :] < cs[:, None] + NA_COLS)
    dc = np.clip(col[None, :] - col[:, None] + NA_COLS - 1, 0, 2 * NA_COLS - 2)
    onehot = (np.arange(2 * NA_COLS - 1)[:, None] == dc.reshape(1, -1)).astype(np.float32)
    t = jnp.dot(rpb.reshape(-1, 2 * NA_COLS - 1), onehot, precision=lax.Precision.HIGHEST)
    t = jnp.where(col_ok[None, None], t.reshape(NA_HEADS, 2 * NA_ROWS_WIN - 1, GRID_W, GRID_W), NEG)
    return jnp.concatenate([t[:, :-1], t[:, 1:]], axis=-1)


def _sw_out_kernel(sink_ref, q_ref, k_ref, v_ref, mask_ref, x_ref, oa_ref, gna_ref, gsw_ref, w_ref, gate_ref,
                   o_ref, ob_ref, *, nblocks):
    lo = _lo_mask(SW_BLOCK)
    zero = jnp.zeros((SW_BLOCK, LANES), q_ref.dtype)
    ones = jnp.ones((3 * SW_BLOCK, LANES), BF16)
    ri = lax.broadcasted_iota(jnp.int32, (2 * SW_BLOCK, LANES), 0)
    ci = lax.broadcasted_iota(jnp.int32, (2 * SW_BLOCK, LANES), 1)
    eye2 = jnp.where((ri == ci) | (ri == ci + SW_BLOCK), 1.0, 0.0).astype(BF16)
    npairs = SW_WIDTH // LANES

    def window(u):
        n = pl.program_id(1) * SW_STEP_BLOCKS + u
        start = pl.multiple_of(jnp.clip(n - 1, 0, nblocks - 3) * SW_BLOCK, SW_BLOCK)
        variant = jnp.where(n == 0, 0, jnp.where(n == nblocks - 1, 2, 1))
        return start, variant

    def scores(u, j):
        start, variant = window(u)
        qj = q_ref[0, u * SW_BLOCK:(u + 1) * SW_BLOCK, j * LANES:(j + 1) * LANES]
        qs = jnp.concatenate([jnp.where(lo, qj, zero), jnp.where(lo, zero, qj)], axis=0)
        return _nt_dot(jnp.concatenate([qs, eye2], axis=1),
                       jnp.concatenate([k_ref[0, pl.ds(start, 3 * SW_BLOCK), :], mask_ref[variant]], axis=1))

    def finish(u, j, s):
        start, _ = window(u)
        ps, sinks = [], []
        for half in range(2):
            sg = s[half * SW_BLOCK:(half + 1) * SW_BLOCK]
            sk = sink_ref[2 * j + half]
            m = jnp.maximum(jnp.max(sg, axis=-1, keepdims=True), sk)
            ps.append(jnp.exp2(sg - m).astype(BF16))
            sinks.append(jnp.exp2(sk - m))
        oa = jnp.dot(jnp.concatenate(ps, axis=0),
                     jnp.concatenate([v_ref[0, pl.ds(start, 3 * SW_BLOCK), :], ones], axis=1),
                     preferred_element_type=F32)
        o = oa[:, :LANES] * (1.0 / (oa[:, LANES:] + jnp.concatenate(sinks, axis=0)))
        ob_ref[u * SW_BLOCK:(u + 1) * SW_BLOCK, j * LANES:(j + 1) * LANES] = jnp.where(lo, o[:SW_BLOCK], o[SW_BLOCK:])

    def project(t):
        rows = slice(t * OUT_ROWS, (t + 1) * OUT_ROWS)
        oa = jnp.concatenate([oa_ref[0, hp, rows, :] for hp in range(NA_WIDTH // LANES)], axis=1)
        oa = (_rms(oa.astype(F32)) * gna_ref[...]).astype(BF16)
        ob = (_rms(ob_ref[rows, :]) * gsw_ref[...]).astype(BF16)
        mix = (jnp.dot(oa, w_ref[:NA_WIDTH], preferred_element_type=F32)
               + jnp.dot(ob, w_ref[NA_WIDTH:], preferred_element_type=F32))
        o_ref[0, rows, :] = x_ref[0, rows, :] + gate_ref[0, 0] * mix

    per_tile = npairs * OUT_ROWS // SW_BLOCK
    units = [(u, j) for u in range(SW_STEP_BLOCKS) for j in range(npairs)]
    staged = [scores(*units[i]) for i in range(SW_AHEAD)]
    for i, unit in enumerate(units):
        if i + SW_AHEAD < len(units):
            staged.append(scores(*units[i + SW_AHEAD]))
        finish(*unit, staged[i])
        staged[i] = None
        if i % per_tile == OUT_LAG and i > per_tile:
            project(i // per_tile - 1)
    project(len(units) // per_tile - 1)


def _sw_attention_out_proj(sink, q, kv, mask, x, oa, gna, gsw, w, mod):
    bsz, s, d = x.shape
    nblocks = s // SW_BLOCK
    assert nblocks >= 3 and nblocks % SW_STEP_BLOCKS == 0
    rows = SW_STEP_BLOCKS * SW_BLOCK
    tile = lambda b, n, sk: (b, n, 0)
    const = lambda a: pl.BlockSpec(a.shape, lambda b, n, sk: (0,) * a.ndim)
    grid_spec = pltpu.PrefetchScalarGridSpec(
        num_scalar_prefetch=1,
        grid=(bsz, nblocks // SW_STEP_BLOCKS),
        in_specs=[pl.BlockSpec((1, rows, SW_WIDTH), tile),
                  pl.BlockSpec((1, s, LANES), lambda b, n, sk: (b, 0, 0)),
                  pl.BlockSpec((1, s, LANES), lambda b, n, sk: (b, 0, 1)),
                  const(mask),
                  pl.BlockSpec((1, rows, d), tile),
                  pl.BlockSpec((1, NA_WIDTH // LANES, rows, LANES), lambda b, n, sk: (b, 0, n, 0)),
                  const(gna), const(gsw), const(w),
                  pl.BlockSpec((1, 1, 1, d), lambda b, n, sk: (b, MOD_GATE_A, 0, 0))],
        out_specs=pl.BlockSpec((1, rows, d), tile),
        scratch_shapes=[pltpu.VMEM((rows, SW_WIDTH), F32)],
    )
    return pl.pallas_call(
        functools.partial(_sw_out_kernel, nblocks=nblocks),
        grid_spec=grid_spec,
        out_shape=jax.ShapeDtypeStruct((bsz, s, d), F32),
        compiler_params=pltpu.CompilerParams(dimension_semantics=("arbitrary", "arbitrary"),
                                             vmem_limit_bytes=VMEM_LIMIT),
        name="sw_attn_out_proj",
    )(sink, q, kv, kv, mask, x, oa, gna, gsw, w, mod)


def _sw_mask_table():
    i = np.arange(SW_BLOCK)[:, None]
    j = np.arange(3 * SW_BLOCK)[None, :]
    ok = np.stack([np.abs(i - j) <= SW_BLOCK,
                   np.abs(i + SW_BLOCK - j) <= SW_BLOCK,
                   np.abs(i + 2 * SW_BLOCK - j) <= SW_BLOCK])
    return jnp.asarray(np.where(ok, 0.0, NEG).transpose(0, 2, 1), BF16)


def _ffn_kernel(x_ref, g_ref, scale_ref, shift_ref, gate_ref, wup_ref, cw_ref, cb_ref, wdn_ref, gfin_ref,
                o_ref, h_ref, *, row_chunk):
    s = x_ref.shape[1]
    nf = cw_ref.shape[0]
    nblk = s // FF_ROWS

    gain = g_ref[...] * (1.0 + scale_ref[0, 0])

    def prologue(i, carry):
        r0 = pl.multiple_of(i * row_chunk, row_chunk)
        xr = x_ref[0, pl.ds(r0, row_chunk), :]
        h_ref[pl.ds(r0, row_chunk), :] = (_rms(xr) * gain + shift_ref[0, 0]).astype(BF16)
        o_ref[0, pl.ds(r0, row_chunk), :] = jnp.zeros_like(xr)
        return carry

    lax.fori_loop(0, s // row_chunk, prologue, 0)

    ridx = lax.broadcasted_iota(jnp.int32, (FF_ROWS, FF_CHUNK), 0)
    edge = jnp.zeros((1, FF_CHUNK), F32)

    def rows(i):
        return slice(i * FF_ROWS, (i + 1) * FF_ROWS)

    def up(c, i):
        h = h_ref[rows(i), :]
        val_cols = pl.ds(pl.multiple_of(c * FF_CHUNK, FF_CHUNK), FF_CHUNK)
        gate_cols = pl.ds(pl.multiple_of((nf + c) * FF_CHUNK, FF_CHUNK), FF_CHUNK)
        return (jnp.dot(h, wup_ref[:, val_cols], preferred_element_type=F32),
                jnp.dot(h, wup_ref[:, gate_cols], preferred_element_type=F32))

    def act(c, i, cur, before, after):
        val, gt = cur
        before = before[1][FF_ROWS - 1:] if i > 0 else edge
        after = after[1][:1] if i < nblk - 1 else edge
        prev = jnp.where(ridx == 0, before, pltpu.roll(gt, 1, 0))
        nxt = jnp.where(ridx == FF_ROWS - 1, after, pltpu.roll(gt, FF_ROWS - 1, 0))
        cw = cw_ref[c]
        gc = prev * cw[0:1] + gt * cw[1:2] + nxt * cw[2:3] + cb_ref[c]
        return (gc * jax.nn.sigmoid(gc) * val).astype(BF16)

    def run(chunks, last):
        units = [(c, i) for c in chunks for i in range(nblk)]
        ups = {0: up(*units[0]), 1: up(*units[1])}
        w_rows = pl.ds(pl.multiple_of(chunks[0] * FF_CHUNK, FF_CHUNK), len(chunks) * FF_CHUNK)
        gated = {i: [] for i in range(nblk)}
        for u, (c, i) in enumerate(units):
            gated[i].append(act(c, i, ups[u], ups.get(u - 1), ups.get(u + 1)))
            if u >= len(units) - nblk:
                a = jnp.concatenate(gated.pop(i), axis=1)
                acc = o_ref[0, rows(i), :] + jnp.dot(a, wdn_ref[w_rows, :], preferred_element_type=F32)
                if last:
                    y = x_ref[0, rows(i), :] + gate_ref[0, 0] * acc
                    acc = _rms(y) * gfin_ref[...]
                o_ref[0, rows(i), :] = acc
            ups.pop(u - 1, None)
            if u + 2 < len(units):
                ups[u + 2] = up(*units[u + 2])

    def body(p, carry):
        run([FF_BODY_CHUNKS * p + j for j in range(FF_BODY_CHUNKS)], False)
        return carry

    nloop = (nf - 1) // FF_BODY_CHUNKS
    lax.fori_loop(0, nloop, body, 0)
    run(list(range(FF_BODY_CHUNKS * nloop, nf)), True)


def _ffn(x1, g, mod, wup, cw, cb, wdn, gfin):
    bsz, s, d = x1.shape
    whole = lambda a: pl.BlockSpec(a.shape, lambda b: (0,) * a.ndim, pipeline_mode=pl.Buffered(1))
    vec = lambda which: pl.BlockSpec((1, 1, 1, d), lambda b: (b, which, 0, 0))
    return pl.pallas_call(
        functools.partial(_ffn_kernel, row_chunk=256),
        grid=(bsz,),
        in_specs=[pl.BlockSpec((1, s, d), lambda b: (b, 0, 0)),
                  whole(g), vec(MOD_SCALE_F), vec(MOD_SHIFT_F), vec(MOD_GATE_F),
                  whole(wup), whole(cw), whole(cb), whole(wdn), whole(gfin)],
        out_specs=pl.BlockSpec((1, s, d), lambda b: (b, 0, 0)),
        out_shape=jax.ShapeDtypeStruct((bsz, s, d), F32),
        scratch_shapes=[pltpu.VMEM((s, d), BF16)],
        compiler_params=pltpu.CompilerParams(dimension_semantics=("arbitrary",),
                                             vmem_limit_bytes=FFN_VMEM_LIMIT),
        name="ffn",
    )(x1, g, mod, mod, mod, wup, cw, cb, wdn, gfin)


def _rope_tables(s):
    half = HEAD_DIM // 2
    inv = ROPE_THETA ** (-np.arange(half, dtype=np.float64) / half)
    ang = np.arange(s, dtype=np.float64)[:, None] * inv[None, :]
    cos = np.cos(ang)
    sin = np.sin(ang)
    reps = LANES // HEAD_DIM
    return (jnp.asarray(np.tile(np.concatenate([cos, cos], axis=-1), (1, reps)), F32),
            jnp.asarray(np.tile(np.concatenate([-sin, sin], axis=-1), (1, reps)), F32))


def _sw_head_order(t, axis):
    shape = t.shape
    t = t.reshape(shape[:axis] + (SW_KV_HEADS, SW_GROUP, -1) + shape[axis + 1:])
    return jnp.swapaxes(t, axis, axis + 1).reshape(shape)


def kernel(x, c, w_ada, b_ada, g_attn, w_in, na_rpb, sw_sink, g_na_out, g_sw_out, w_out, g_ffn, w_up,
           conv_w, conv_b, w_down, g_final):
    bsz, s, d = x.shape
    depth = w_ada.shape[0]
    d_ff = w_down.shape[1]
    assert depth == 1, "the final rmsnorm is fused into the (only) layer's ffn call"
    assert d_ff % FF_CHUNK == 0 and s % GRID_W == 0 and s % SW_BLOCK == 0
    cos, sin = _rope_tables(s)
    sw_mask = _sw_mask_table()
    qb0 = 3 * NA_WIDTH
    for l in range(depth):
        mod = _ada(c, w_ada[l], b_ada[l]).reshape(bsz, 6, 1, d)

        wi = w_in[l].astype(BF16)
        wi = jnp.concatenate([wi[:, :qb0], _sw_head_order(wi[:, qb0:qb0 + SW_WIDTH], 1),
                              wi[:, qb0 + SW_WIDTH:]], axis=1)
        qa, ka, va, qb, kvb = _in_proj(x, g_attn[l].reshape(1, d), mod, wi, cos, sin)

        o_a = _na_attention(qa, ka, va, _na_bias_table(na_rpb[l].astype(F32) * LOG2E, s // GRID_W))
        wo = w_out[l].astype(BF16)
        wo = jnp.concatenate([wo[:NA_WIDTH], _sw_head_order(wo[NA_WIDTH:], 0)], axis=0)
        x = _sw_attention_out_proj(_sw_head_order(sw_sink[l].astype(F32) * LOG2E, 0), qb, kvb, sw_mask, x, o_a,
                                   g_na_out[l].reshape(1, -1), _sw_head_order(g_sw_out[l], 0).reshape(1, -1), wo, mod)

        nf = d_ff // FF_CHUNK
        cw = conv_w[l].reshape(CONV_W, nf, FF_CHUNK).transpose(1, 0, 2)
        x = _ffn(x, g_ffn[l].reshape(1, d), mod, w_up[l].astype(BF16), cw, conv_b[l].reshape(nf, 1, FF_CHUNK),
                 w_down[l].astype(BF16), g_final.reshape(1, d))
    return x
```

```python
import functools

import jax
import jax.numpy as jnp
import numpy as np
from jax import lax
from jax.experimental import pallas as pl
from jax.experimental.pallas import tpu as pltpu

F32 = jnp.float32
BF16 = jnp.bfloat16

HEAD_DIM = 64
LANES = 128
NA_HEADS = 8
NA_WIDTH = NA_HEADS * HEAD_DIM
GRID_W = 64
NA_ROWS_WIN = 8
NA_COLS = 16
SW_HEADS = 8
SW_KV_HEADS = 2
SW_GROUP = SW_HEADS // SW_KV_HEADS
SW_WIDTH = SW_HEADS * HEAD_DIM
SW_KV_WIDTH = SW_KV_HEADS * HEAD_DIM
SW_BLOCK = 128
CONV_W = 3
ROPE_THETA = 10000.0
EPS = 1e-6
NEG = -1e30
LOG2E = 1.4426950408889634
Q_SCALE = HEAD_DIM ** -0.5 * LOG2E

VMEM_LIMIT = 56 * 1024 * 1024
FFN_VMEM_LIMIT = 63 * 1024 * 1024

ADA_TN = 1024
IN_TM = 2048
IN_ROWS = 256
FF_CHUNK = 256
FF_BODY_CHUNKS = 3
FF_ROWS = 512
NA_BODY_PAIRS = 4
NA_AHEAD = 4
SW_STEP_BLOCKS = 8
SW_AHEAD = 3
OUT_ROWS = 512
OUT_LAG = 3


def _rms(x):
    return x * lax.rsqrt(jnp.mean(x * x, axis=-1, keepdims=True) + EPS)


def _nt_dot(a, b):
    return lax.dot_general(a, b, (((1,), (1,)), ((), ())), preferred_element_type=F32)


def _lo_mask(rows):
    return lax.broadcasted_iota(jnp.int32, (rows, LANES), 1) < HEAD_DIM


def _split_bf16(a):
    hi = a.astype(BF16)
    return hi, (a - hi.astype(F32)).astype(BF16)


def _ada_kernel(c_ref, w_ref, b_ref, o_ref):
    c = c_ref[...]
    s_hi, s_lo = _split_bf16(c * jax.nn.sigmoid(c))
    w_hi, w_lo = _split_bf16(w_ref[...])
    dot = functools.partial(jnp.dot, preferred_element_type=F32)
    o_ref[...] = dot(s_hi, w_hi) + dot(s_hi, w_lo) + dot(s_lo, w_hi) + b_ref[...]


def _ada(c, w, b):
    bsz, d = c.shape
    n = w.shape[1]
    tn = ADA_TN
    return pl.pallas_call(
        _ada_kernel,
        grid=(n // tn,),
        in_specs=[pl.BlockSpec((bsz, d), lambda j: (0, 0)),
                  pl.BlockSpec((d, tn), lambda j: (0, j)),
                  pl.BlockSpec((1, tn), lambda j: (0, j))],
        out_specs=pl.BlockSpec((bsz, tn), lambda j: (0, j)),
        out_shape=jax.ShapeDtypeStruct((bsz, n), F32),
        compiler_params=pltpu.CompilerParams(dimension_semantics=("arbitrary",),
                                             vmem_limit_bytes=VMEM_LIMIT),
        name="ada",
    )(c, w, b.reshape(1, n))


MOD_SHIFT_A, MOD_SCALE_A, MOD_GATE_A, MOD_SHIFT_F, MOD_SCALE_F, MOD_GATE_F = range(6)


def _mod_spec(which, d):
    return pl.BlockSpec((1, 1, 1, d), lambda b, t: (b, which, 0, 0))


def _rope(t, cos, sin_signed, first_half):
    rot = jnp.where(first_half, pltpu.roll(t, LANES - HEAD_DIM // 2, 1), pltpu.roll(t, HEAD_DIM // 2, 1))
    return t * cos + rot * sin_signed


def _in_proj_kernel(x_ref, g_ref, scale_ref, shift_ref, w_ref, cos_ref, sin_ref,
                    qa_ref, ka_ref, va_ref, qb_ref, kvb_ref):
    gain = g_ref[...]
    scale = 1.0 + scale_ref[0, 0]
    shift = shift_ref[0, 0]
    first_half = (lax.broadcasted_iota(jnp.int32, (IN_ROWS, LANES), 1) % HEAD_DIM) < HEAD_DIM // 2
    base = 3 * NA_WIDTH

    for i in range(x_ref.shape[1] // IN_ROWS):
        rows = slice(i * IN_ROWS, (i + 1) * IN_ROWS)
        h = ((_rms(x_ref[0, rows, :]) * gain) * scale + shift).astype(BF16)
        cos = cos_ref[rows, :]
        sin = sin_ref[rows, :]

        def proj(lo, hi):
            return jnp.dot(h, w_ref[:, lo:hi], preferred_element_type=F32)

        for ref, col0, mult in ((qa_ref, 0, Q_SCALE), (ka_ref, NA_WIDTH, None), (va_ref, 2 * NA_WIDTH, None)):
            t = proj(col0, col0 + NA_WIDTH)
            t = t if mult is None else t * mult
            for hp in range(NA_WIDTH // LANES):
                ref[0, hp, rows, :] = t[:, hp * LANES:(hp + 1) * LANES].astype(BF16)
        qb = proj(base, base + SW_WIDTH)
        for j in range(SW_WIDTH // LANES):
            blk = _rope(qb[:, j * LANES:(j + 1) * LANES], cos, sin, first_half) * Q_SCALE
            qb_ref[0, rows, j * LANES:(j + 1) * LANES] = blk.astype(BF16)
        kv = proj(base + SW_WIDTH, base + SW_WIDTH + 2 * SW_KV_WIDTH)
        kvb_ref[0, rows, :LANES] = _rope(kv[:, :LANES], cos, sin, first_half).astype(BF16)
        kvb_ref[0, rows, LANES:] = kv[:, LANES:].astype(BF16)


def _in_proj(x, g, mod, w, cos, sin):
    bsz, s, d = x.shape
    n = w.shape[1]
    tm = IN_TM
    row = lambda b, t: (b, t, 0)
    pairs = NA_WIDTH // LANES
    outs = [jax.ShapeDtypeStruct((bsz, pairs, s, LANES), BF16)] * 3 + [
        jax.ShapeDtypeStruct((bsz, s, SW_WIDTH), BF16),
        jax.ShapeDtypeStruct((bsz, s, 2 * SW_KV_WIDTH), BF16)]
    return pl.pallas_call(
        _in_proj_kernel,
        grid=(bsz, s // tm),
        in_specs=[pl.BlockSpec((1, tm, d), row),
                  pl.BlockSpec((1, d), lambda b, t: (0, 0)),
                  _mod_spec(MOD_SCALE_A, d),
                  _mod_spec(MOD_SHIFT_A, d),
                  pl.BlockSpec((d, n), lambda b, t: (0, 0)),
                  pl.BlockSpec((tm, LANES), lambda b, t: (t, 0)),
                  pl.BlockSpec((tm, LANES), lambda b, t: (t, 0))],
        out_specs=[pl.BlockSpec((1, pairs, tm, LANES), lambda b, t: (b, 0, t, 0))] * 3 + [
            pl.BlockSpec((1, tm, SW_WIDTH), row),
            pl.BlockSpec((1, tm, 2 * SW_KV_WIDTH), row)],
        out_shape=outs,
        compiler_params=pltpu.CompilerParams(dimension_semantics=("arbitrary", "arbitrary"),
                                             vmem_limit_bytes=VMEM_LIMIT),
        name="in_proj",
    )(x, g, mod, mod, w, cos, sin)


def _na_kernel(q_ref, k_ref, v_ref, bias_ref, o_ref, *, rows):
    lo = _lo_mask(GRID_W)
    win = NA_ROWS_WIN * GRID_W
    ones = jnp.ones((win, LANES), BF16)

    def scores(hp, r):
        rs = jnp.clip(r - NA_ROWS_WIN // 2, 0, rows - NA_ROWS_WIN)
        q = q_ref[0, hp, pl.ds(pl.multiple_of(r * GRID_W, GRID_W), GRID_W), :]
        k0 = pl.multiple_of(rs * GRID_W, GRID_W)
        zero = jnp.zeros_like(q)
        qs = jnp.concatenate([jnp.where(lo, q, zero), jnp.where(lo, zero, q)], axis=0)
        dr0 = NA_ROWS_WIN - 1 - (r - rs)
        bias = jnp.concatenate(
            [jnp.concatenate([bias_ref[2 * hp + head, dr0 + 2 * jj] for jj in range(NA_ROWS_WIN // 2)], axis=1)
             for head in range(2)], axis=0)
        return _nt_dot(qs, k_ref[0, hp, pl.ds(k0, win), :]) + bias, k0

    def finish(hp, r, s, k0):
        p = jnp.exp2(s - jnp.max(s, axis=-1, keepdims=True)).astype(BF16)
        oa = jnp.dot(p, jnp.concatenate([v_ref[0, hp, pl.ds(k0, win), :], ones], axis=1),
                     preferred_element_type=F32)
        o = oa[:, :LANES] * (1.0 / oa[:, LANES:])
        out = jnp.where(lo, o[:GRID_W], o[GRID_W:])
        o_ref[0, hp, pl.ds(pl.multiple_of(r * GRID_W, GRID_W), GRID_W), :] = out.astype(o_ref.dtype)

    def body(it, carry):
        units = [(NA_BODY_PAIRS * it + j, r) for j in range(NA_BODY_PAIRS) for r in range(rows)]
        staged = [scores(*units[i]) for i in range(NA_AHEAD)]
        for i, unit in enumerate(units):
            if i + NA_AHEAD < len(units):
                staged.append(scores(*units[i + NA_AHEAD]))
            finish(*unit, *staged[i])
            staged[i] = None
        return carry

    lax.fori_loop(0, q_ref.shape[1] // NA_BODY_PAIRS, body, 0)


def _na_attention(q, k, v, bias):
    bsz, pairs, s, _ = q.shape
    rows = s // GRID_W
    blk = pl.BlockSpec((1, pairs, s, LANES), lambda b: (b, 0, 0, 0))
    return pl.pallas_call(
        functools.partial(_na_kernel, rows=rows),
        grid=(bsz,),
        in_specs=[blk, blk, blk, pl.BlockSpec(bias.shape, lambda b: (0, 0, 0, 0))],
        out_specs=blk,
        out_shape=jax.ShapeDtypeStruct(q.shape, BF16),
        compiler_params=pltpu.CompilerParams(dimension_semantics=("arbitrary",),
                                             vmem_limit_bytes=VMEM_LIMIT),
        name="na_attn",
    )(q, k, v, bias)


def _na_bias_table(rpb, rows):
    assert rows >= 2 * NA_ROWS_WIN
    col = np.arange(GRID_W)
    cs = np.clip(col - NA_COLS // 2, 0, GRID_W - NA_COLS)
    col_ok = (col[None, :] >= cs[:, None]) & (col[None, :] < cs[:, None] + NA_COLS)
    dc = np.clip(col[None, :] - col[:, None] + NA_COLS - 1, 0, 2 * NA_COLS - 2)
    onehot = (np.arange(2 * NA_COLS - 1)[:, None] == dc.reshape(1, -1)).astype(np.float32)
    t = jnp.dot(rpb.reshape(-1, 2 * NA_COLS - 1), onehot, precision=lax.Precision.HIGHEST)
    t = jnp.where(col_ok[None, None], t.reshape(NA_HEADS, 2 * NA_ROWS_WIN - 1, GRID_W, GRID_W), NEG)
    return jnp.concatenate([t[:, :-1], t[:, 1:]], axis=-1)


def _sw_out_kernel(sink_ref, q_ref, k_ref, v_ref, mask_ref, x_ref, oa_ref, gna_ref, gsw_ref, w_ref, gate_ref,
                   o_ref, ob_ref, *, nblocks):
    lo = _lo_mask(SW_BLOCK)
    zero = jnp.zeros((SW_BLOCK, LANES), q_ref.dtype)
    ones = jnp.ones((3 * SW_BLOCK, LANES), BF16)
    ri = lax.broadcasted_iota(jnp.int32, (2 * SW_BLOCK, LANES), 0)
    ci = lax.broadcasted_iota(jnp.int32, (2 * SW_BLOCK, LANES), 1)
    eye2 = jnp.where((ri == ci) | (ri == ci + SW_BLOCK), 1.0, 0.0).astype(BF16)
    npairs = SW_WIDTH // LANES

    def window(u):
        n = pl.program_id(1) * SW_STEP_BLOCKS + u
        start = pl.multiple_of(jnp.clip(n - 1, 0, nblocks - 3) * SW_BLOCK, SW_BLOCK)
        variant = jnp.where(n == 0, 0, jnp.where(n == nblocks - 1, 2, 1))
        return start, variant

    def scores(u, j):
        start, variant = window(u)
        qj = q_ref[0, u * SW_BLOCK:(u + 1) * SW_BLOCK, j * LANES:(j + 1) * LANES]
        qs = jnp.concatenate([jnp.where(lo, qj, zero), jnp.where(lo, zero, qj)], axis=0)
        return _nt_dot(jnp.concatenate([qs, eye2], axis=1),
                       jnp.concatenate([k_ref[0, pl.ds(start, 3 * SW_BLOCK), :], mask_ref[variant]], axis=1))

    def finish(u, j, s):
        start, _ = window(u)
        ps, sinks = [], []
        for half in range(2):
            sg = s[half * SW_BLOCK:(half + 1) * SW_BLOCK]
            sk = sink_ref[2 * j + half]
            m = jnp.maximum(jnp.max(sg, axis=-1, keepdims=True), sk)
            ps.append(jnp.exp2(sg - m).astype(BF16))
            sinks.append(jnp.exp2(sk - m))
        oa = jnp.dot(jnp.concatenate(ps, axis=0),
                     jnp.concatenate([v_ref[0, pl.ds(start, 3 * SW_BLOCK), :], ones], axis=1),
                     preferred_element_type=F32)
        o = oa[:, :LANES] * (1.0 / (oa[:, LANES:] + jnp.concatenate(sinks, axis=0)))
        ob_ref[u * SW_BLOCK:(u + 1) * SW_BLOCK, j * LANES:(j + 1) * LANES] = jnp.where(lo, o[:SW_BLOCK], o[SW_BLOCK:])

    def project(t):
        rows = slice(t * OUT_ROWS, (t + 1) * OUT_ROWS)
        oa = jnp.concatenate([oa_ref[0, hp, rows, :] for hp in range(NA_WIDTH // LANES)], axis=1)
        oa = (_rms(oa.astype(F32)) * gna_ref[...]).astype(BF16)
        ob = (_rms(ob_ref[rows, :]) * gsw_ref[...]).astype(BF16)
        mix = (jnp.dot(oa, w_ref[:NA_WIDTH], preferred_element_type=F32)
               + jnp.dot(ob, w_ref[NA_WIDTH:], preferred_element_type=F32))
        o_ref[0, rows, :] = x_ref[0, rows, :] + gate_ref[0, 0] * mix

    per_tile = npairs * OUT_ROWS // SW_BLOCK
    units = [(u, j) for u in range(SW_STEP_BLOCKS) for j in range(npairs)]
    staged = [scores(*units[i]) for i in range(SW_AHEAD)]
    for i, unit in enumerate(units):
        if i + SW_AHEAD < len(units):
            staged.append(scores(*units[i + SW_AHEAD]))
        finish(*unit, staged[i])
        staged[i] = None
        if i % per_tile == OUT_LAG and i > per_tile:
            project(i // per_tile - 1)
    project(len(units) // per_tile - 1)


def _sw_attention_out_proj(sink, q, kv, mask, x, oa, gna, gsw, w, mod):
    bsz, s, d = x.shape
    nblocks = s // SW_BLOCK
    assert nblocks >= 3 and nblocks % SW_STEP_BLOCKS == 0
    rows = SW_STEP_BLOCKS * SW_BLOCK
    tile = lambda b, n, sk: (b, n, 0)
    const = lambda a: pl.BlockSpec(a.shape, lambda b, n, sk: (0,) * a.ndim)
    grid_spec = pltpu.PrefetchScalarGridSpec(
        num_scalar_prefetch=1,
        grid=(bsz, nblocks // SW_STEP_BLOCKS),
        in_specs=[pl.BlockSpec((1, rows, SW_WIDTH), tile),
                  pl.BlockSpec((1, s, LANES), lambda b, n, sk: (b, 0, 0)),
                  pl.BlockSpec((1, s, LANES), lambda b, n, sk: (b, 0, 1)),
                  const(mask),
                  pl.BlockSpec((1, rows, d), tile),
                  pl.BlockSpec((1, NA_WIDTH // LANES, rows, LANES), lambda b, n, sk: (b, 0, n, 0)),
                  const(gna), const(gsw), const(w),
                  pl.BlockSpec((1, 1, 1, d), lambda b, n, sk: (b, MOD_GATE_A, 0, 0))],
        out_specs=pl.BlockSpec((1, rows, d), tile),
        scratch_shapes=[pltpu.VMEM((rows, SW_WIDTH), F32)],
    )
    return pl.pallas_call(
        functools.partial(_sw_out_kernel, nblocks=nblocks),
        grid_spec=grid_spec,
        out_shape=jax.ShapeDtypeStruct((bsz, s, d), F32),
        compiler_params=pltpu.CompilerParams(dimension_semantics=("arbitrary", "arbitrary"),
                                             vmem_limit_bytes=VMEM_LIMIT),
        name="sw_attn_out_proj",
    )(sink, q, kv, kv, mask, x, oa, gna, gsw, w, mod)


def _sw_mask_table():
    i = np.arange(SW_BLOCK)[:, None]
    j = np.arange(3 * SW_BLOCK)[None, :]
    ok = np.stack([np.abs(i - j) <= SW_BLOCK,
                   np.abs(i + SW_BLOCK - j) <= SW_BLOCK,
                   np.abs(i + 2 * SW_BLOCK - j) <= SW_BLOCK])
    return jnp.asarray(np.where(ok, 0.0, NEG).transpose(0, 2, 1), BF16)


def _ffn_kernel(x_ref, g_ref, scale_ref, shift_ref, gate_ref, wup_ref, cw_ref, cb_ref, wdn_ref, gfin_ref,
                o_ref, h_ref, *, row_chunk):
    s = x_ref.shape[1]
    nf = cw_ref.shape[0]
    nblk = s // FF_ROWS

    gain = g_ref[...] * (1.0 + scale_ref[0, 0])

    def prologue(i, carry):
        r0 = pl.multiple_of(i * row_chunk, row_chunk)
        xr = x_ref[0, pl.ds(r0, row_chunk), :]
        h_ref[pl.ds(r0, row_chunk), :] = (_rms(xr) * gain + shift_ref[0, 0]).astype(BF16)
        o_ref[0, pl.ds(r0, row_chunk), :] = jnp.zeros_like(xr)
        return carry

    lax.fori_loop(0, s // row_chunk, prologue, 0)

    ridx = lax.broadcasted_iota(jnp.int32, (FF_ROWS, FF_CHUNK), 0)
    edge = jnp.zeros((1, FF_CHUNK), F32)

    def rows(i):
        return slice(i * FF_ROWS, (i + 1) * FF_ROWS)

    def up(c, i):
        h = h_ref[rows(i), :]
        val_cols = pl.ds(pl.multiple_of(c * FF_CHUNK, FF_CHUNK), FF_CHUNK)
        gate_cols = pl.ds(pl.multiple_of((nf + c) * FF_CHUNK, FF_CHUNK), FF_CHUNK)
        return (jnp.dot(h, wup_ref[:, val_cols], preferred_element_type=F32),
                jnp.dot(h, wup_ref[:, gate_cols], preferred_element_type=F32))

    def act(c, i, cur, before, after):
        val, gt = cur
        before = before[1][FF_ROWS - 1:] if i > 0 else edge
        after = after[1][:1] if i < nblk - 1 else edge
        prev = jnp.where(ridx == 0, before, pltpu.roll(gt, 1, 0))
        nxt = jnp.where(ridx == FF_ROWS - 1, after, pltpu.roll(gt, FF_ROWS - 1, 0))
        cw = cw_ref[c]
        gc = prev * cw[0:1] + gt * cw[1:2] + nxt * cw[2:3] + cb_ref[c]
        return (gc * jax.nn.sigmoid(gc) * val).astype(BF16)

    def run(chunks, last):
        units = [(c, i) for c in chunks for i in range(nblk)]
        ups = {0: up(*units[0]), 1: up(*units[1])}
        w_rows = pl.ds(pl.multiple_of(chunks[0] * FF_CHUNK, FF_CHUNK), len(chunks) * FF_CHUNK)
        gated = {i: [] for i in range(nblk)}
        for u, (c, i) in enumerate(units):
            gated[i].append(act(c, i, ups[u], ups.get(u - 1), ups.get(u + 1)))
            if u >= len(units) - nblk:
                a = jnp.concatenate(gated.pop(i), axis=1)
                acc = o_ref[0, rows(i), :] + jnp.dot(a, wdn_ref[w_rows, :], preferred_element_type=F32)
                if last:
                    y = x_ref[0, rows(i), :] + gate_ref[0, 0] * acc
                    acc = _rms(y) * gfin_ref[...]
                o_ref[0, rows(i), :] = acc
            ups.pop(u - 1, None)
            if u + 2 < len(units):
                ups[u + 2] = up(*units[u + 2])

    def body(p, carry):
        run([FF_BODY_CHUNKS * p + j for j in range(FF_BODY_CHUNKS)], False)
        return carry

    nloop = (nf - 1) // FF_BODY_CHUNKS
    lax.fori_loop(0, nloop, body, 0)
    run(list(range(FF_BODY_CHUNKS * nloop, nf)), True)


def _ffn(x1, g, mod, wup, cw, cb, wdn, gfin):
    bsz, s, d = x1.shape
    whole = lambda a: pl.BlockSpec(a.shape, lambda b: (0,) * a.ndim, pipeline_mode=pl.Buffered(1))
    vec = lambda which: pl.BlockSpec((1, 1, 1, d), lambda b: (b, which, 0, 0))
    return pl.pallas_call(
        functools.partial(_ffn_kernel, row_chunk=256),
        grid=(bsz,),
        in_specs=[pl.BlockSpec((1, s, d), lambda b: (b, 0, 0)),
                  whole(g), vec(MOD_SCALE_F), vec(MOD_SHIFT_F), vec(MOD_GATE_F),
                  whole(wup), whole(cw), whole(cb), whole(wdn), whole(gfin)],
        out_specs=pl.BlockSpec((1, s, d), lambda b: (b, 0, 0)),
        out_shape=jax.ShapeDtypeStruct((bsz, s, d), F32),
        scratch_shapes=[pltpu.VMEM((s, d), BF16)],
        compiler_params=pltpu.CompilerParams(dimension_semantics=("arbitrary",),
                                             vmem_limit_bytes=FFN_VMEM_LIMIT),
        name="ffn",
    )(x1, g, mod, mod, mod, wup, cw, cb, wdn, gfin)


def _rope_tables(s):
    half = HEAD_DIM // 2
    inv = ROPE_THETA ** (-np.arange(half, dtype=np.float64) / half)
    ang = np.arange(s, dtype=np.float64)[:, None] * inv[None, :]
    cos = np.cos(ang)
    sin = np.sin(ang)
    reps = LANES // HEAD_DIM
    return (jnp.asarray(np.tile(np.concatenate([cos, cos], axis=-1), (1, reps)), F32),
            jnp.asarray(np.tile(np.concatenate([-sin, sin], axis=-1), (1, reps)), F32))


def _sw_head_order(t, axis):
    shape = t.shape
    t = t.reshape(shape[:axis] + (SW_KV_HEADS, SW_GROUP, -1) + shape[axis + 1:])
    return jnp.swapaxes(t, axis, axis + 1).reshape(shape)


def kernel(x, c, w_ada, b_ada, g_attn, w_in, na_rpb, sw_sink, g_na_out, g_sw_out, w_out, g_ffn, w_up,
           conv_w, conv_b, w_down, g_final):
    bsz, s, d = x.shape
    depth = w_ada.shape[0]
    d_ff = w_down.shape[1]
    assert depth == 1, "the final rmsnorm is fused into the (only) layer's ffn call"
    assert d_ff % FF_CHUNK == 0 and s % GRID_W == 0 and s % SW_BLOCK == 0
    cos, sin = _rope_tables(s)
    sw_mask = _sw_mask_table()
    qb0 = 3 * NA_WIDTH
    for l in range(depth):
        mod = _ada(c, w_ada[l], b_ada[l]).reshape(bsz, 6, 1, d)

        wi = w_in[l].astype(BF16)
        wi = jnp.concatenate([wi[:, :qb0], _sw_head_order(wi[:, qb0:qb0 + SW_WIDTH], 1),
                              wi[:, qb0 + SW_WIDTH:]], axis=1)
        qa, ka, va, qb, kvb = _in_proj(x, g_attn[l].reshape(1, d), mod, wi, cos, sin)

        o_a = _na_attention(qa, ka, va, _na_bias_table(na_rpb[l].astype(F32) * LOG2E, s // GRID_W))
        wo = w_out[l].astype(BF16)
        wo = jnp.concatenate([wo[:NA_WIDTH], _sw_head_order(wo[NA_WIDTH:], 0)], axis=0)
        x = _sw_attention_out_proj(_sw_head_order(sw_sink[l].astype(F32) * LOG2E, 0), qb, kvb, sw_mask, x, o_a,
                                   g_na_out[l].reshape(1, -1), _sw_head_order(g_sw_out[l], 0).reshape(1, -1), wo, mod)

        nf = d_ff // FF_CHUNK
        cw = conv_w[l].reshape(CONV_W, nf, FF_CHUNK).transpose(1, 0, 2)
        x = _ffn(x, g_ffn[l].reshape(1, d), mod, w_up[l].astype(BF16), cw, conv_b[l].reshape(nf, 1, FF_CHUNK),
                 w_down[l].astype(BF16), g_final.reshape(1, d))
    return x
```

```python
import functools

import jax
import jax.numpy as jnp
import numpy as np
from jax import lax
from jax.experimental import pallas as pl
from jax.experimental.pallas import tpu as pltpu

F32 = jnp.float32
BF16 = jnp.bfloat16

HEAD_DIM = 64
LANES = 128
NA_HEADS = 8
NA_WIDTH = NA_HEADS * HEAD_DIM
GRID_W = 64
NA_ROWS_WIN = 8
NA_COLS = 16
SW_HEADS = 8
SW_KV_HEADS = 2
SW_GROUP = SW_HEADS // SW_KV_HEADS
SW_WIDTH = SW_HEADS * HEAD_DIM
SW_KV_WIDTH = SW_KV_HEADS * HEAD_DIM
SW_BLOCK = 128
CONV_W = 3
ROPE_THETA = 10000.0
EPS = 1e-6
NEG = -1e30
LOG2E = 1.4426950408889634
Q_SCALE = HEAD_DIM ** -0.5 * LOG2E

VMEM_LIMIT = 56 * 1024 * 1024
FFN_VMEM_LIMIT = 63 * 1024 * 1024

ADA_TN = 1024
IN_TM = 2048
IN_ROWS = 256
FF_CHUNK = 256
FF_BODY_CHUNKS = 3
FF_ROWS = 512
NA_BODY_PAIRS = 4
NA_AHEAD = 4
SW_STEP_BLOCKS = 8
SW_AHEAD = 3
OUT_ROWS = 512
OUT_LAG = 3


def _rms(x):
    return x * lax.rsqrt(jnp.mean(x * x, axis=-1, keepdims=True) + EPS)


def _nt_dot(a, b):
    return lax.dot_general(a, b, (((1,), (1,)), ((), ())), preferred_element_type=F32)


def _lo_mask(rows):
    return lax.broadcasted_iota(jnp.int32, (rows, LANES), 1) < HEAD_DIM


def _split_bf16(a):
    hi = a.astype(BF16)
    return hi, (a - hi.astype(F32)).astype(BF16)


def _ada_kernel(c_ref, w_ref, b_ref, o_ref):
    c = c_ref[...]
    s_hi, s_lo = _split_bf16(c * jax.nn.sigmoid(c))
    w_hi, w_lo = _split_bf16(w_ref[...])
    dot = functools.partial(jnp.dot, preferred_element_type=F32)
    o_ref[...] = dot(s_hi, w_hi) + dot(s_hi, w_lo) + dot(s_lo, w_hi) + b_ref[...]


def _ada(c, w, b):
    bsz, d = c.shape
    n = w.shape[1]
    tn = ADA_TN
    return pl.pallas_call(
        _ada_kernel,
        grid=(n // tn,),
        in_specs=[pl.BlockSpec((bsz, d), lambda j: (0, 0)),
                  pl.BlockSpec((d, tn), lambda j: (0, j)),
                  pl.BlockSpec((1, tn), lambda j: (0, j))],
        out_specs=pl.BlockSpec((bsz, tn), lambda j: (0, j)),
        out_shape=jax.ShapeDtypeStruct((bsz, n), F32),
        compiler_params=pltpu.CompilerParams(dimension_semantics=("arbitrary",),
                                             vmem_limit_bytes=VMEM_LIMIT),
        name="ada",
    )(c, w, b.reshape(1, n))


MOD_SHIFT_A, MOD_SCALE_A, MOD_GATE_A, MOD_SHIFT_F, MOD_SCALE_F, MOD_GATE_F = range(6)


def _mod_spec(which, d):
    return pl.BlockSpec((1, 1, 1, d), lambda b, t: (b, which, 0, 0))


def _rope(t, cos, sin_signed, first_half):
    rot = jnp.where(first_half, pltpu.roll(t, LANES - HEAD_DIM // 2, 1), pltpu.roll(t, HEAD_DIM // 2, 1))
    return t * cos + rot * sin_signed


def _in_proj_kernel(x_ref, g_ref, scale_ref, shift_ref, w_ref, cos_ref, sin_ref,
                    qa_ref, ka_ref, va_ref, qb_ref, kvb_ref):
    gain = g_ref[...]
    scale = 1.0 + scale_ref[0, 0]
    shift = shift_ref[0, 0]
    first_half = (lax.broadcasted_iota(jnp.int32, (IN_ROWS, LANES), 1) % HEAD_DIM) < HEAD_DIM // 2
    base = 3 * NA_WIDTH

    for i in range(x_ref.shape[1] // IN_ROWS):
        rows = slice(i * IN_ROWS, (i + 1) * IN_ROWS)
        h = ((_rms(x_ref[0, rows, :]) * gain) * scale + shift).astype(BF16)
        cos = cos_ref[rows, :]
        sin = sin_ref[rows, :]

        def proj(lo, hi):
            return jnp.dot(h, w_ref[:, lo:hi], preferred_element_type=F32)

        for ref, col0, mult in ((qa_ref, 0, Q_SCALE), (ka_ref, NA_WIDTH, None), (va_ref, 2 * NA_WIDTH, None)):
            t = proj(col0, col0 + NA_WIDTH)
            t = t if mult is None else t * mult
            for hp in range(NA_WIDTH // LANES):
                ref[0, hp, rows, :] = t[:, hp * LANES:(hp + 1) * LANES].astype(BF16)
        qb = proj(base, base + SW_WIDTH)
        for j in range(SW_WIDTH // LANES):
            blk = _rope(qb[:, j * LANES:(j + 1) * LANES], cos, sin, first_half) * Q_SCALE
            qb_ref[0, rows, j * LANES:(j + 1) * LANES] = blk.astype(BF16)
        kv = proj(base + SW_WIDTH, base + SW_WIDTH + 2 * SW_KV_WIDTH)
        kvb_ref[0, rows, :LANES] = _rope(kv[:, :LANES], cos, sin, first_half).astype(BF16)
        kvb_ref[0, rows, LANES:] = kv[:, LANES:].astype(BF16)


def _in_proj(x, g, mod, w, cos, sin):
    bsz, s, d = x.shape
    n = w.shape[1]
    tm = IN_TM
    row = lambda b, t: (b, t, 0)
    pairs = NA_WIDTH // LANES
    outs = [jax.ShapeDtypeStruct((bsz, pairs, s, LANES), BF16)] * 3 + [
        jax.ShapeDtypeStruct((bsz, s, SW_WIDTH), BF16),
        jax.ShapeDtypeStruct((bsz, s, 2 * SW_KV_WIDTH), BF16)]
    return pl.pallas_call(
        _in_proj_kernel,
        grid=(bsz, s // tm),
        in_specs=[pl.BlockSpec((1, tm, d), row),
                  pl.BlockSpec((1, d), lambda b, t: (0, 0)),
                  _mod_spec(MOD_SCALE_A, d),
                  _mod_spec(MOD_SHIFT_A, d),
                  pl.BlockSpec((d, n), lambda b, t: (0, 0)),
                  pl.BlockSpec((tm, LANES), lambda b, t: (t, 0)),
                  pl.BlockSpec((tm, LANES), lambda b, t: (t, 0))],
        out_specs=[pl.BlockSpec((1, pairs, tm, LANES), lambda b, t: (b, 0, t, 0))] * 3 + [
            pl.BlockSpec((1, tm, SW_WIDTH), row),
            pl.BlockSpec((1, tm, 2 * SW_KV_WIDTH), row)],
        out_shape=outs,
        compiler_params=pltpu.CompilerParams(dimension_semantics=("arbitrary", "arbitrary"),
                                             vmem_limit_bytes=VMEM_LIMIT),
        name="in_proj",
    )(x, g, mod, mod, w, cos, sin)


def _na_kernel(q_ref, k_ref, v_ref, bias_ref, o_ref, *, rows):
    lo = _lo_mask(GRID_W)
    win = NA_ROWS_WIN * GRID_W
    ones = jnp.ones((win, LANES), BF16)

    def scores(hp, r):
        rs = jnp.clip(r - NA_ROWS_WIN // 2, 0, rows - NA_ROWS_WIN)
        q = q_ref[0, hp, pl.ds(pl.multiple_of(r * GRID_W, GRID_W), GRID_W), :]
        k0 = pl.multiple_of(rs * GRID_W, GRID_W)
        zero = jnp.zeros_like(q)
        qs = jnp.concatenate([jnp.where(lo, q, zero), jnp.where(lo, zero, q)], axis=0)
        dr0 = NA_ROWS_WIN - 1 - (r - rs)
        bias = jnp.concatenate(
            [jnp.concatenate([bias_ref[2 * hp + head, dr0 + 2 * jj] for jj in range(NA_ROWS_WIN // 2)], axis=1)
             for head in range(2)], axis=0)
        return _nt_dot(qs, k_ref[0, hp, pl.ds(k0, win), :]) + bias, k0

    def finish(hp, r, s, k0):
        p = jnp.exp2(s - jnp.max(s, axis=-1, keepdims=True)).astype(BF16)
        oa = jnp.dot(p, jnp.concatenate([v_ref[0, hp, pl.ds(k0, win), :], ones], axis=1),
                     preferred_element_type=F32)
        o = oa[:, :LANES] * (1.0 / oa[:, LANES:])
        out = jnp.where(lo, o[:GRID_W], o[GRID_W:])
        o_ref[0, hp, pl.ds(pl.multiple_of(r * GRID_W, GRID_W), GRID_W), :] = out.astype(o_ref.dtype)

    def body(it, carry):
        units = [(NA_BODY_PAIRS * it + j, r) for j in range(NA_BODY_PAIRS) for r in range(rows)]
        staged = [scores(*units[i]) for i in range(NA_AHEAD)]
        for i, unit in enumerate(units):
            if i + NA_AHEAD < len(units):
                staged.append(scores(*units[i + NA_AHEAD]))
            finish(*unit, *staged[i])
            staged[i] = None
        return carry

    lax.fori_loop(0, q_ref.shape[1] // NA_BODY_PAIRS, body, 0)


def _na_attention(q, k, v, bias):
    bsz, pairs, s, _ = q.shape
    rows = s // GRID_W
    blk = pl.BlockSpec((1, pairs, s, LANES), lambda b: (b, 0, 0, 0))
    return pl.pallas_call(
        functools.partial(_na_kernel, rows=rows),
        grid=(bsz,),
        in_specs=[blk, blk, blk, pl.BlockSpec(bias.shape, lambda b: (0, 0, 0, 0))],
        out_specs=blk,
        out_shape=jax.ShapeDtypeStruct(q.shape, BF16),
        compiler_params=pltpu.CompilerParams(dimension_semantics=("arbitrary",),
                                             vmem_limit_bytes=VMEM_LIMIT),
        name="na_attn",
    )(q, k, v, bias)


def _na_bias_table(rpb, rows):
    assert rows >= 2 * NA_ROWS_WIN
    col = np.arange(GRID_W)
    cs = np.clip(col - NA_COLS // 2, 0, GRID_W - NA_COLS)
    col_ok = (col[None, :] >= cs[:, None]) & (col[None, :] < cs[:, None] + NA_COLS)
    dc = np.clip(col[None, :] - col[:, None] + NA_COLS - 1, 0, 2 * NA_COLS - 2)
    onehot = (np.arange(2 * NA_COLS - 1)[:, None] == dc.reshape(1, -1)).astype(np.float32)
    t = jnp.dot(rpb.reshape(-1, 2 * NA_COLS - 1), onehot, precision=lax.Precision.HIGHEST)
    t = jnp.where(col_ok[None, None], t.reshape(NA_HEADS, 2 * NA_ROWS_WIN - 1, GRID_W, GRID_W), NEG)
    return jnp.concatenate([t[:, :-1], t[:, 1:]], axis=-1)


def _sw_out_kernel(sink_ref, q_ref, k_ref, v_ref, mask_ref, x_ref, oa_ref, gna_ref, gsw_ref, w_ref, gate_ref,
                   o_ref, ob_ref, *, nblocks):
    lo = _lo_mask(SW_BLOCK)
    zero = jnp.zeros((SW_BLOCK, LANES), q_ref.dtype)
    ones = jnp.ones((3 * SW_BLOCK, LANES), BF16)
    npairs = SW_WIDTH // LANES

    def window(u):
        n = pl.program_id(1) * SW_STEP_BLOCKS + u
        start = pl.multiple_of(jnp.clip(n - 1, 0, nblocks - 3) * SW_BLOCK, SW_BLOCK)
        variant = jnp.where(n == 0, 0, jnp.where(n == nblocks - 1, 2, 1))
        return start, variant

    def scores(u, j):
        start, variant = window(u)
        qj = q_ref[0, u * SW_BLOCK:(u + 1) * SW_BLOCK, j * LANES:(j + 1) * LANES]
        qs = jnp.concatenate([jnp.where(lo, qj, zero), jnp.where(lo, zero, qj)], axis=0)
        s = _nt_dot(qs, k_ref[0, pl.ds(start, 3 * SW_BLOCK), :])
        return s.reshape(2, SW_BLOCK, 3 * SW_BLOCK) + mask_ref[variant]

    def finish(u, j, s):
        start, _ = window(u)
        ps, sinks = [], []
        for half in range(2):
            sg = s[half]
            sk = sink_ref[2 * j + half]
            m = jnp.maximum(jnp.max(sg, axis=-1, keepdims=True), sk)
            ps.append(jnp.exp2(sg - m).astype(BF16))
            sinks.append(jnp.exp2(sk - m))
        oa = jnp.dot(jnp.concatenate(ps, axis=0),
                     jnp.concatenate([v_ref[0, pl.ds(start, 3 * SW_BLOCK), :], ones], axis=1),
                     preferred_element_type=F32)
        o = oa[:, :LANES] * (1.0 / (oa[:, LANES:] + jnp.concatenate(sinks, axis=0)))
        ob_ref[u * SW_BLOCK:(u + 1) * SW_BLOCK, j * LANES:(j + 1) * LANES] = jnp.where(lo, o[:SW_BLOCK], o[SW_BLOCK:])

    def project(t):
        rows = slice(t * OUT_ROWS, (t + 1) * OUT_ROWS)
        oa = jnp.concatenate([oa_ref[0, hp, rows, :] for hp in range(NA_WIDTH // LANES)], axis=1)
        oa = (_rms(oa.astype(F32)) * gna_ref[...]).astype(BF16)
        ob = (_rms(ob_ref[rows, :]) * gsw_ref[...]).astype(BF16)
        mix = (jnp.dot(oa, w_ref[:NA_WIDTH], preferred_element_type=F32)
               + jnp.dot(ob, w_ref[NA_WIDTH:], preferred_element_type=F32))
        o_ref[0, rows, :] = x_ref[0, rows, :] + gate_ref[0, 0] * mix

    per_tile = npairs * OUT_ROWS // SW_BLOCK
    units = [(u, j) for u in range(SW_STEP_BLOCKS) for j in range(npairs)]
    staged = [scores(*units[i]) for i in range(SW_AHEAD)]
    for i, unit in enumerate(units):
        if i + SW_AHEAD < len(units):
            staged.append(scores(*units[i + SW_AHEAD]))
        finish(*unit, staged[i])
        staged[i] = None
        if i % per_tile == OUT_LAG and i > per_tile:
            project(i // per_tile - 1)
    project(len(units) // per_tile - 1)


def _sw_attention_out_proj(sink, q, kv, mask, x, oa, gna, gsw, w, mod):
    bsz, s, d = x.shape
    nblocks = s // SW_BLOCK
    assert nblocks >= 3 and nblocks % SW_STEP_BLOCKS == 0
    rows = SW_STEP_BLOCKS * SW_BLOCK
    tile = lambda b, n, sk: (b, n, 0)
    const = lambda a: pl.BlockSpec(a.shape, lambda b, n, sk: (0,) * a.ndim)
    grid_spec = pltpu.PrefetchScalarGridSpec(
        num_scalar_prefetch=1,
        grid=(bsz, nblocks // SW_STEP_BLOCKS),
        in_specs=[pl.BlockSpec((1, rows, SW_WIDTH), tile),
                  pl.BlockSpec((1, s, LANES), lambda b, n, sk: (b, 0, 0)),
                  pl.BlockSpec((1, s, LANES), lambda b, n, sk: (b, 0, 1)),
                  const(mask),
                  pl.BlockSpec((1, rows, d), tile),
                  pl.BlockSpec((1, NA_WIDTH // LANES, rows, LANES), lambda b, n, sk: (b, 0, n, 0)),
                  const(gna), const(gsw), const(w),
                  pl.BlockSpec((1, 1, 1, d), lambda b, n, sk: (b, MOD_GATE_A, 0, 0))],
        out_specs=pl.BlockSpec((1, rows, d), tile),
        scratch_shapes=[pltpu.VMEM((rows, SW_WIDTH), F32)],
    )
    return pl.pallas_call(
        functools.partial(_sw_out_kernel, nblocks=nblocks),
        grid_spec=grid_spec,
        out_shape=jax.ShapeDtypeStruct((bsz, s, d), F32),
        compiler_params=pltpu.CompilerParams(dimension_semantics=("arbitrary", "arbitrary"),
                                             vmem_limit_bytes=VMEM_LIMIT),
        name="sw_attn_out_proj",
    )(sink, q, kv, kv, mask, x, oa, gna, gsw, w, mod)


def _sw_mask_table():
    i = np.arange(SW_BLOCK)[:, None]
    j = np.arange(3 * SW_BLOCK)[None, :]
    ok = np.stack([np.abs(i - j) <= SW_BLOCK,
                   np.abs(i + SW_BLOCK - j) <= SW_BLOCK,
                   np.abs(i + 2 * SW_BLOCK - j) <= SW_BLOCK])
    return jnp.asarray(np.where(ok, 0.0, NEG), F32)


def _ffn_kernel(x_ref, g_ref, scale_ref, shift_ref, gate_ref, wup_ref, cw_ref, cb_ref, wdn_ref, gfin_ref,
                o_ref, h_ref, *, row_chunk):
    s = x_ref.shape[1]
    nf = cw_ref.shape[0]
    nblk = s // FF_ROWS

    gain = g_ref[...] * (1.0 + scale_ref[0, 0])

    def prologue(i, carry):
        r0 = pl.multiple_of(i * row_chunk, row_chunk)
        xr = x_ref[0, pl.ds(r0, row_chunk), :]
        h_ref[pl.ds(r0, row_chunk), :] = (_rms(xr) * gain + shift_ref[0, 0]).astype(BF16)
        o_ref[0, pl.ds(r0, row_chunk), :] = jnp.zeros_like(xr)
        return carry

    lax.fori_loop(0, s // row_chunk, prologue, 0)

    ridx = lax.broadcasted_iota(jnp.int32, (FF_ROWS, FF_CHUNK), 0)
    edge = jnp.zeros((1, FF_CHUNK), F32)

    def rows(i):
        return slice(i * FF_ROWS, (i + 1) * FF_ROWS)

    def up(c, i):
        h = h_ref[rows(i), :]
        val_cols = pl.ds(pl.multiple_of(c * FF_CHUNK, FF_CHUNK), FF_CHUNK)
        gate_cols = pl.ds(pl.multiple_of((nf + c) * FF_CHUNK, FF_CHUNK), FF_CHUNK)
        return (jnp.dot(h, wup_ref[:, val_cols], preferred_element_type=F32),
                jnp.dot(h, wup_ref[:, gate_cols], preferred_element_type=F32))

    def act(c, i, cur, before, after):
        val, gt = cur
        before = before[1][FF_ROWS - 1:] if i > 0 else edge
        after = after[1][:1] if i < nblk - 1 else edge
        prev = jnp.where(ridx == 0, before, pltpu.roll(gt, 1, 0))
        nxt = jnp.where(ridx == FF_ROWS - 1, after, pltpu.roll(gt, FF_ROWS - 1, 0))
        cw = cw_ref[c]
        gc = prev * cw[0:1] + gt * cw[1:2] + nxt * cw[2:3] + cb_ref[c]
        return (gc * jax.nn.sigmoid(gc) * val).astype(BF16)

    def run(chunks, last):
        units = [(c, i) for c in chunks for i in range(nblk)]
        ups = {0: up(*units[0]), 1: up(*units[1])}
        w_rows = pl.ds(pl.multiple_of(chunks[0] * FF_CHUNK, FF_CHUNK), len(chunks) * FF_CHUNK)
        gated = {i: [] for i in range(nblk)}
        for u, (c, i) in enumerate(units):
            gated[i].append(act(c, i, ups[u], ups.get(u - 1), ups.get(u + 1)))
            if u >= len(units) - nblk:
                a = jnp.concatenate(gated.pop(i), axis=1)
                acc = o_ref[0, rows(i), :] + jnp.dot(a, wdn_ref[w_rows, :], preferred_element_type=F32)
                if last:
                    y = x_ref[0, rows(i), :] + gate_ref[0, 0] * acc
                    acc = _rms(y) * gfin_ref[...]
                o_ref[0, rows(i), :] = acc
            ups.pop(u - 1, None)
            if u + 2 < len(units):
                ups[u + 2] = up(*units[u + 2])

    def body(p, carry):
        run([FF_BODY_CHUNKS * p + j for j in range(FF_BODY_CHUNKS)], False)
        return carry

    nloop = (nf - 1) // FF_BODY_CHUNKS
    lax.fori_loop(0, nloop, body, 0)
    run(list(range(FF_BODY_CHUNKS * nloop, nf)), True)


def _ffn(x1, g, mod, wup, cw, cb, wdn, gfin):
    bsz, s, d = x1.shape
    whole = lambda a: pl.BlockSpec(a.shape, lambda b: (0,) * a.ndim, pipeline_mode=pl.Buffered(1))
    vec = lambda which: pl.BlockSpec((1, 1, 1, d), lambda b: (b, which, 0, 0))
    return pl.pallas_call(
        functools.partial(_ffn_kernel, row_chunk=256),
        grid=(bsz,),
        in_specs=[pl.BlockSpec((1, s, d), lambda b: (b, 0, 0)),
                  whole(g), vec(MOD_SCALE_F), vec(MOD_SHIFT_F), vec(MOD_GATE_F),
                  whole(wup), whole(cw), whole(cb), whole(wdn), whole(gfin)],
        out_specs=pl.BlockSpec((1, s, d), lambda b: (b, 0, 0)),
        out_shape=jax.ShapeDtypeStruct((bsz, s, d), F32),
        scratch_shapes=[pltpu.VMEM((s, d), BF16)],
        compiler_params=pltpu.CompilerParams(dimension_semantics=("arbitrary",),
                                             vmem_limit_bytes=FFN_VMEM_LIMIT),
        name="ffn",
    )(x1, g, mod, mod, mod, wup, cw, cb, wdn, gfin)


def _rope_tables(s):
    half = HEAD_DIM // 2
    inv = ROPE_THETA ** (-np.arange(half, dtype=np.float64) / half)
    ang = np.arange(s, dtype=np.float64)[:, None] * inv[None, :]
    cos = np.cos(ang)
    sin = np.sin(ang)
    reps = LANES // HEAD_DIM
    return (jnp.asarray(np.tile(np.concatenate([cos, cos], axis=-1), (1, reps)), F32),
            jnp.asarray(np.tile(np.concatenate([-sin, sin], axis=-1), (1, reps)), F32))


def _sw_head_order(t, axis):
    shape = t.shape
    t = t.reshape(shape[:axis] + (SW_KV_HEADS, SW_GROUP, -1) + shape[axis + 1:])
    return jnp.swapaxes(t, axis, axis + 1).reshape(shape)


def kernel(x, c, w_ada, b_ada, g_attn, w_in, na_rpb, sw_sink, g_na_out, g_sw_out, w_out, g_ffn, w_up,
           conv_w, conv_b, w_down, g_final):
    bsz, s, d = x.shape
    depth = w_ada.shape[0]
    d_ff = w_down.shape[1]
    assert depth == 1, "the final rmsnorm is fused into the (only) layer's ffn call"
    assert d_ff % FF_CHUNK == 0 and s % GRID_W == 0 and s % SW_BLOCK == 0
    cos, sin = _rope_tables(s)
    sw_mask = _sw_mask_table()
    qb0 = 3 * NA_WIDTH
    for l in range(depth):
        mod = _ada(c, w_ada[l], b_ada[l]).reshape(bsz, 6, 1, d)

        wi = w_in[l].astype(BF16)
        wi = jnp.concatenate([wi[:, :qb0], _sw_head_order(wi[:, qb0:qb0 + SW_WIDTH], 1),
                              wi[:, qb0 + SW_WIDTH:]], axis=1)
        qa, ka, va, qb, kvb = _in_proj(x, g_attn[l].reshape(1, d), mod, wi, cos, sin)

        o_a = _na_attention(qa, ka, va, _na_bias_table(na_rpb[l].astype(F32) * LOG2E, s // GRID_W))
        wo = w_out[l].astype(BF16)
        wo = jnp.concatenate([wo[:NA_WIDTH], _sw_head_order(wo[NA_WIDTH:], 0)], axis=0)
        x = _sw_attention_out_proj(_sw_head_order(sw_sink[l].astype(F32) * LOG2E, 0), qb, kvb, sw_mask, x, o_a,
                                   g_na_out[l].reshape(1, -1), _sw_head_order(g_sw_out[l], 0).reshape(1, -1), wo, mod)

        nf = d_ff // FF_CHUNK
        cw = conv_w[l].reshape(CONV_W, nf, FF_CHUNK).transpose(1, 0, 2)
        x = _ffn(x, g_ffn[l].reshape(1, d), mod, w_up[l].astype(BF16), cw, conv_b[l].reshape(nf, 1, FF_CHUNK),
                 w_down[l].astype(BF16), g_final.reshape(1, d))
    return x
```

```python
import functools

import jax
import jax.numpy as jnp
import numpy as np
from jax import lax
from jax.experimental import pallas as pl
from jax.experimental.pallas import tpu as pltpu

F32 = jnp.float32
BF16 = jnp.bfloat16

HEAD_DIM = 64
LANES = 128
NA_HEADS = 8
NA_WIDTH = NA_HEADS * HEAD_DIM
GRID_W = 64
NA_ROWS_WIN = 8
NA_COLS = 16
SW_HEADS = 8
SW_KV_HEADS = 2
SW_GROUP = SW_HEADS // SW_KV_HEADS
SW_WIDTH = SW_HEADS * HEAD_DIM
SW_KV_WIDTH = SW_KV_HEADS * HEAD_DIM
SW_BLOCK = 128
CONV_W = 3
ROPE_THETA = 10000.0
EPS = 1e-6
NEG = -1e30
LOG2E = 1.4426950408889634
Q_SCALE = HEAD_DIM ** -0.5 * LOG2E

VMEM_LIMIT = 56 * 1024 * 1024
FFN_VMEM_LIMIT = 62 * 1024 * 1024

ADA_TN = 1024
IN_TM = 2048
IN_ROWS = 256
FF_CHUNK = 256
FF_BODY_CHUNKS = 3
FF_ROWS = 512
FF_SLAB = FF_BODY_CHUNKS * FF_CHUNK
NA_BODY_PAIRS = 4
NA_AHEAD = 4
SW_STEP_BLOCKS = 8
SW_AHEAD = 3
OUT_ROWS = 512
OUT_LAG = 3


def _rms(x):
    return x * lax.rsqrt(jnp.mean(x * x, axis=-1, keepdims=True) + EPS)


def _nt_dot(a, b):
    return lax.dot_general(a, b, (((1,), (1,)), ((), ())), preferred_element_type=F32)


def _lo_mask(rows):
    return lax.broadcasted_iota(jnp.int32, (rows, LANES), 1) < HEAD_DIM


def _split_bf16(a):
    hi = a.astype(BF16)
    return hi, (a - hi.astype(F32)).astype(BF16)


def _ada_kernel(c_ref, w_ref, b_ref, o_ref):
    c = c_ref[...]
    s_hi, s_lo = _split_bf16(c * jax.nn.sigmoid(c))
    w_hi, w_lo = _split_bf16(w_ref[...])
    dot = functools.partial(jnp.dot, preferred_element_type=F32)
    o_ref[...] = dot(s_hi, w_hi) + dot(s_hi, w_lo) + dot(s_lo, w_hi) + b_ref[...]


def _ada(c, w, b):
    bsz, d = c.shape
    n = w.shape[1]
    tn = ADA_TN
    return pl.pallas_call(
        _ada_kernel,
        grid=(n // tn,),
        in_specs=[pl.BlockSpec((bsz, d), lambda j: (0, 0)),
                  pl.BlockSpec((d, tn), lambda j: (0, j)),
                  pl.BlockSpec((1, tn), lambda j: (0, j))],
        out_specs=pl.BlockSpec((bsz, tn), lambda j: (0, j)),
        out_shape=jax.ShapeDtypeStruct((bsz, n), F32),
        compiler_params=pltpu.CompilerParams(dimension_semantics=("arbitrary",),
                                             vmem_limit_bytes=VMEM_LIMIT),
        name="ada",
    )(c, w, b.reshape(1, n))


MOD_SHIFT_A, MOD_SCALE_A, MOD_GATE_A, MOD_SHIFT_F, MOD_SCALE_F, MOD_GATE_F = range(6)


def _mod_spec(which, d):
    return pl.BlockSpec((1, 1, 1, d), lambda b, t: (b, which, 0, 0))


def _rope(t, cos, sin_signed, first_half):
    rot = jnp.where(first_half, pltpu.roll(t, LANES - HEAD_DIM // 2, 1), pltpu.roll(t, HEAD_DIM // 2, 1))
    return t * cos + rot * sin_signed


def _in_proj_kernel(x_ref, g_ref, scale_ref, shift_ref, w_ref, cos_ref, sin_ref,
                    qa_ref, ka_ref, va_ref, qb_ref, kvb_ref):
    gain = g_ref[...]
    scale = 1.0 + scale_ref[0, 0]
    shift = shift_ref[0, 0]
    first_half = (lax.broadcasted_iota(jnp.int32, (IN_ROWS, LANES), 1) % HEAD_DIM) < HEAD_DIM // 2
    base = 3 * NA_WIDTH

    for i in range(x_ref.shape[1] // IN_ROWS):
        rows = slice(i * IN_ROWS, (i + 1) * IN_ROWS)
        h = ((_rms(x_ref[0, rows, :]) * gain) * scale + shift).astype(BF16)
        cos = cos_ref[rows, :]
        sin = sin_ref[rows, :]

        def proj(lo, hi):
            return jnp.dot(h, w_ref[:, lo:hi], preferred_element_type=F32)

        for ref, col0, mult in ((qa_ref, 0, Q_SCALE), (ka_ref, NA_WIDTH, None), (va_ref, 2 * NA_WIDTH, None)):
            t = proj(col0, col0 + NA_WIDTH)
            t = t if mult is None else t * mult
            for hp in range(NA_WIDTH // LANES):
                ref[0, hp, rows, :] = t[:, hp * LANES:(hp + 1) * LANES].astype(BF16)
        qb = proj(base, base + SW_WIDTH)
        for j in range(SW_WIDTH // LANES):
            blk = _rope(qb[:, j * LANES:(j + 1) * LANES], cos, sin, first_half) * Q_SCALE
            qb_ref[0, rows, j * LANES:(j + 1) * LANES] = blk.astype(BF16)
        kv = proj(base + SW_WIDTH, base + SW_WIDTH + 2 * SW_KV_WIDTH)
        kvb_ref[0, rows, :LANES] = _rope(kv[:, :LANES], cos, sin, first_half).astype(BF16)
        kvb_ref[0, rows, LANES:] = kv[:, LANES:].astype(BF16)


def _in_proj(x, g, mod, w, cos, sin):
    bsz, s, d = x.shape
    n = w.shape[1]
    tm = IN_TM
    row = lambda b, t: (b, t, 0)
    pairs = NA_WIDTH // LANES
    outs = [jax.ShapeDtypeStruct((bsz, pairs, s, LANES), BF16)] * 3 + [
        jax.ShapeDtypeStruct((bsz, s, SW_WIDTH), BF16),
        jax.ShapeDtypeStruct((bsz, s, 2 * SW_KV_WIDTH), BF16)]
    return pl.pallas_call(
        _in_proj_kernel,
        grid=(bsz, s // tm),
        in_specs=[pl.BlockSpec((1, tm, d), row),
                  pl.BlockSpec((1, d), lambda b, t: (0, 0)),
                  _mod_spec(MOD_SCALE_A, d),
                  _mod_spec(MOD_SHIFT_A, d),
                  pl.BlockSpec((d, n), lambda b, t: (0, 0)),
                  pl.BlockSpec((tm, LANES), lambda b, t: (t, 0)),
                  pl.BlockSpec((tm, LANES), lambda b, t: (t, 0))],
        out_specs=[pl.BlockSpec((1, pairs, tm, LANES), lambda b, t: (b, 0, t, 0))] * 3 + [
            pl.BlockSpec((1, tm, SW_WIDTH), row),
            pl.BlockSpec((1, tm, 2 * SW_KV_WIDTH), row)],
        out_shape=outs,
        compiler_params=pltpu.CompilerParams(dimension_semantics=("arbitrary", "arbitrary"),
                                             vmem_limit_bytes=VMEM_LIMIT),
        name="in_proj",
    )(x, g, mod, mod, w, cos, sin)


def _na_kernel(q_ref, k_ref, v_ref, bias_ref, o_ref, *, rows):
    lo = _lo_mask(GRID_W)
    win = NA_ROWS_WIN * GRID_W
    ones = jnp.ones((win, LANES), BF16)

    def scores(hp, r):
        rs = jnp.clip(r - NA_ROWS_WIN // 2, 0, rows - NA_ROWS_WIN)
        q = q_ref[0, hp, pl.ds(pl.multiple_of(r * GRID_W, GRID_W), GRID_W), :]
        k0 = pl.multiple_of(rs * GRID_W, GRID_W)
        zero = jnp.zeros_like(q)
        qs = jnp.concatenate([jnp.where(lo, q, zero), jnp.where(lo, zero, q)], axis=0)
        dr0 = NA_ROWS_WIN - 1 - (r - rs)
        bias = jnp.concatenate(
            [jnp.concatenate([bias_ref[2 * hp + head, dr0 + 2 * jj] for jj in range(NA_ROWS_WIN // 2)], axis=1)
             for head in range(2)], axis=0)
        return _nt_dot(qs, k_ref[0, hp, pl.ds(k0, win), :]) + bias, k0

    def finish(hp, r, s, k0):
        p = jnp.exp2(s - jnp.max(s, axis=-1, keepdims=True)).astype(BF16)
        oa = jnp.dot(p, jnp.concatenate([v_ref[0, hp, pl.ds(k0, win), :], ones], axis=1),
                     preferred_element_type=F32)
        o = oa[:, :LANES] * (1.0 / oa[:, LANES:])
        out = jnp.where(lo, o[:GRID_W], o[GRID_W:])
        o_ref[0, hp, pl.ds(pl.multiple_of(r * GRID_W, GRID_W), GRID_W), :] = out.astype(o_ref.dtype)

    def body(it, carry):
        units = [(NA_BODY_PAIRS * it + j, r) for j in range(NA_BODY_PAIRS) for r in range(rows)]
        staged = [scores(*units[i]) for i in range(NA_AHEAD)]
        for i, unit in enumerate(units):
            if i + NA_AHEAD < len(units):
                staged.append(scores(*units[i + NA_AHEAD]))
            finish(*unit, *staged[i])
            staged[i] = None
        return carry

    lax.fori_loop(0, q_ref.shape[1] // NA_BODY_PAIRS, body, 0)


def _na_attention(q, k, v, bias):
    bsz, pairs, s, _ = q.shape
    rows = s // GRID_W
    blk = pl.BlockSpec((1, pairs, s, LANES), lambda b: (b, 0, 0, 0))
    return pl.pallas_call(
        functools.partial(_na_kernel, rows=rows),
        grid=(bsz,),
        in_specs=[blk, blk, blk, pl.BlockSpec(bias.shape, lambda b: (0, 0, 0, 0))],
        out_specs=blk,
        out_shape=jax.ShapeDtypeStruct(q.shape, BF16),
        compiler_params=pltpu.CompilerParams(dimension_semantics=("arbitrary",),
                                             vmem_limit_bytes=VMEM_LIMIT),
        name="na_attn",
    )(q, k, v, bias)


def _na_bias_table(rpb, rows):
    assert rows >= 2 * NA_ROWS_WIN
    col = np.arange(GRID_W)
    cs = np.clip(col - NA_COLS // 2, 0, GRID_W - NA_COLS)
    col_ok = (col[None, :] >= cs[:, None]) & (col[None, :] < cs[:, None] + NA_COLS)
    dc = np.clip(col[None, :] - col[:, None] + NA_COLS - 1, 0, 2 * NA_COLS - 2)
    onehot = (np.arange(2 * NA_COLS - 1)[:, None] == dc.reshape(1, -1)).astype(np.float32)
    t = jnp.dot(rpb.reshape(-1, 2 * NA_COLS - 1), onehot, precision=lax.Precision.HIGHEST)
    t = jnp.where(col_ok[None, None], t.reshape(NA_HEADS, 2 * NA_ROWS_WIN - 1, GRID_W, GRID_W), NEG)
    return jnp.concatenate([t[:, :-1], t[:, 1:]], axis=-1)


def _sw_out_kernel(sink_ref, q_ref, k_ref, v_ref, mask_ref, x_ref, oa_ref, gna_ref, gsw_ref, w_ref, gate_ref,
                   o_ref, ob_ref, *, nblocks):
    lo = _lo_mask(SW_BLOCK)
    zero = jnp.zeros((SW_BLOCK, LANES), q_ref.dtype)
    ones = jnp.ones((3 * SW_BLOCK, LANES), BF16)
    npairs = SW_WIDTH // LANES

    def window(u):
        n = pl.program_id(1) * SW_STEP_BLOCKS + u
        start = pl.multiple_of(jnp.clip(n - 1, 0, nblocks - 3) * SW_BLOCK, SW_BLOCK)
        variant = jnp.where(n == 0, 0, jnp.where(n == nblocks - 1, 2, 1))
        return start, variant

    def scores(u, j):
        start, variant = window(u)
        qj = q_ref[0, u * SW_BLOCK:(u + 1) * SW_BLOCK, j * LANES:(j + 1) * LANES]
        qs = jnp.concatenate([jnp.where(lo, qj, zero), jnp.where(lo, zero, qj)], axis=0)
        s = _nt_dot(qs, k_ref[0, pl.ds(start, 3 * SW_BLOCK), :])
        return s.reshape(2, SW_BLOCK, 3 * SW_BLOCK) + mask_ref[variant]

    def finish(u, j, s):
        start, _ = window(u)
        ps, sinks = [], []
        for half in range(2):
            sg = s[half]
            sk = sink_ref[2 * j + half]
            m = jnp.maximum(jnp.max(sg, axis=-1, keepdims=True), sk)
            ps.append(jnp.exp2(sg - m).astype(BF16))
            sinks.append(jnp.exp2(sk - m))
        oa = jnp.dot(jnp.concatenate(ps, axis=0),
                     jnp.concatenate([v_ref[0, pl.ds(start, 3 * SW_BLOCK), :], ones], axis=1),
                     preferred_element_type=F32)
        o = oa[:, :LANES] * (1.0 / (oa[:, LANES:] + jnp.concatenate(sinks, axis=0)))
        ob_ref[u * SW_BLOCK:(u + 1) * SW_BLOCK, j * LANES:(j + 1) * LANES] = jnp.where(lo, o[:SW_BLOCK], o[SW_BLOCK:])

    def project(t):
        rows = slice(t * OUT_ROWS, (t + 1) * OUT_ROWS)
        oa = jnp.concatenate([oa_ref[0, hp, rows, :] for hp in range(NA_WIDTH // LANES)], axis=1)
        oa = (_rms(oa.astype(F32)) * gna_ref[...]).astype(BF16)
        ob = (_rms(ob_ref[rows, :]) * gsw_ref[...]).astype(BF16)
        mix = (jnp.dot(oa, w_ref[:NA_WIDTH], preferred_element_type=F32)
               + jnp.dot(ob, w_ref[NA_WIDTH:], preferred_element_type=F32))
        o_ref[0, rows, :] = x_ref[0, rows, :] + gate_ref[0, 0] * mix

    per_tile = npairs * OUT_ROWS // SW_BLOCK
    units = [(u, j) for u in range(SW_STEP_BLOCKS) for j in range(npairs)]
    staged = [scores(*units[i]) for i in range(SW_AHEAD)]
    for i, unit in enumerate(units):
        if i + SW_AHEAD < len(units):
            staged.append(scores(*units[i + SW_AHEAD]))
        finish(*unit, staged[i])
        staged[i] = None
        if i % per_tile == OUT_LAG and i > per_tile:
            project(i // per_tile - 1)
    project(len(units) // per_tile - 1)


def _sw_attention_out_proj(sink, q, kv, mask, x, oa, gna, gsw, w, mod):
    bsz, s, d = x.shape
    nblocks = s // SW_BLOCK
    assert nblocks >= 3 and nblocks % SW_STEP_BLOCKS == 0
    rows = SW_STEP_BLOCKS * SW_BLOCK
    tile = lambda b, n, sk: (b, n, 0)
    const = lambda a: pl.BlockSpec(a.shape, lambda b, n, sk: (0,) * a.ndim)
    grid_spec = pltpu.PrefetchScalarGridSpec(
        num_scalar_prefetch=1,
        grid=(bsz, nblocks // SW_STEP_BLOCKS),
        in_specs=[pl.BlockSpec((1, rows, SW_WIDTH), tile),
                  pl.BlockSpec((1, s, LANES), lambda b, n, sk: (b, 0, 0)),
                  pl.BlockSpec((1, s, LANES), lambda b, n, sk: (b, 0, 1)),
                  const(mask),
                  pl.BlockSpec((1, rows, d), tile),
                  pl.BlockSpec((1, NA_WIDTH // LANES, rows, LANES), lambda b, n, sk: (b, 0, n, 0)),
                  const(gna), const(gsw), const(w),
                  pl.BlockSpec((1, 1, 1, d), lambda b, n, sk: (b, MOD_GATE_A, 0, 0))],
        out_specs=pl.BlockSpec((1, rows, d), tile),
        scratch_shapes=[pltpu.VMEM((rows, SW_WIDTH), F32)],
    )
    return pl.pallas_call(
        functools.partial(_sw_out_kernel, nblocks=nblocks),
        grid_spec=grid_spec,
        out_shape=jax.ShapeDtypeStruct((bsz, s, d), F32),
        compiler_params=pltpu.CompilerParams(dimension_semantics=("arbitrary", "arbitrary"),
                                             vmem_limit_bytes=VMEM_LIMIT),
        name="sw_attn_out_proj",
    )(sink, q, kv, kv, mask, x, oa, gna, gsw, w, mod)


def _sw_mask_table():
    i = np.arange(SW_BLOCK)[:, None]
    j = np.arange(3 * SW_BLOCK)[None, :]
    ok = np.stack([np.abs(i - j) <= SW_BLOCK,
                   np.abs(i + SW_BLOCK - j) <= SW_BLOCK,
                   np.abs(i + 2 * SW_BLOCK - j) <= SW_BLOCK])
    return jnp.asarray(np.where(ok, 0.0, NEG), F32)


def _ffn_kernel(x_ref, g_ref, scale_ref, shift_ref, gate_ref, wup_hbm, cw_ref, cb_ref, wdn_hbm, gfin_ref,
                o_ref, h_ref, wv_ref, wg_ref, wd_ref, sems, *, row_chunk):
    s = x_ref.shape[1]
    nf = cw_ref.shape[0]
    nblk = s // FF_ROWS
    nslabs = pl.cdiv(nf, FF_BODY_CHUNKS)
    assert nslabs % 2 == 0
    d_ff = nf * FF_CHUNK
    b = pl.program_id(0)

    def slab_start(p):
        if isinstance(p, int):
            return min(p * FF_SLAB, d_ff - FF_SLAB)
        return pl.multiple_of(jnp.minimum(p * FF_SLAB, d_ff - FF_SLAB), FF_CHUNK)

    def slab_copies(p, slot):
        c0 = slab_start(p)
        return (pltpu.make_async_copy(wup_hbm.at[:, pl.ds(c0, FF_SLAB)], wv_ref.at[slot], sems.at[slot, 0]),
                pltpu.make_async_copy(wup_hbm.at[:, pl.ds(d_ff + c0, FF_SLAB)], wg_ref.at[slot], sems.at[slot, 1]),
                pltpu.make_async_copy(wdn_hbm.at[pl.ds(c0, FF_SLAB), :], wd_ref.at[slot], sems.at[slot, 2]))

    def start(p, slot):
        for cp in slab_copies(p, slot):
            cp.start()

    def wait(p, slot):
        for cp in slab_copies(p, slot):
            cp.wait()

    ridx = lax.broadcasted_iota(jnp.int32, (FF_ROWS, FF_CHUNK), 0)
    edge = jnp.zeros((1, FF_CHUNK), F32)

    def rows(i):
        return slice(i * FF_ROWS, (i + 1) * FF_ROWS)

    def norm_rows(i):
        gain = g_ref[...] * (1.0 + scale_ref[0, 0])
        for r0 in range(i * FF_ROWS, (i + 1) * FF_ROWS, row_chunk):
            xr = x_ref[0, r0:r0 + row_chunk, :]
            h_ref[r0:r0 + row_chunk, :] = (_rms(xr) * gain + shift_ref[0, 0]).astype(BF16)

    def up(slot, lc, i):
        h = h_ref[rows(i), :]
        cols = slice(lc * FF_CHUNK, (lc + 1) * FF_CHUNK)
        return (jnp.dot(h, wv_ref[slot, :, cols], preferred_element_type=F32),
                jnp.dot(h, wg_ref[slot, :, cols], preferred_element_type=F32))

    def act(c, i, cur, before, after):
        val, gt = cur
        before = before[1][FF_ROWS - 1:] if i > 0 else edge
        after = after[1][:1] if i < nblk - 1 else edge
        prev = jnp.where(ridx == 0, before, pltpu.roll(gt, 1, 0))
        nxt = jnp.where(ridx == FF_ROWS - 1, after, pltpu.roll(gt, FF_ROWS - 1, 0))
        cw = cw_ref[c]
        gc = prev * cw[0:1] + gt * cw[1:2] + nxt * cw[2:3] + cb_ref[c]
        return (gc * jax.nn.sigmoid(gc) * val).astype(BF16)

    def run(p, slot, local_chunks, first=False, last=False):
        chunk0 = slab_start(p) // FF_CHUNK
        units = [(lc, i) for lc in local_chunks for i in range(nblk)]

        def issue(u):
            if first and u < nblk:
                norm_rows(units[u][1])
            return up(slot, *units[u])

        ups = {0: issue(0), 1: issue(1)}
        w_rows = slice(local_chunks[0] * FF_CHUNK, (local_chunks[-1] + 1) * FF_CHUNK)
        gated = {i: [] for i in range(nblk)}
        for u, (lc, i) in enumerate(units):
            gated[i].append(act(chunk0 + lc, i, ups[u], ups.get(u - 1), ups.get(u + 1)))
            if u >= len(units) - nblk:
                a = jnp.concatenate(gated.pop(i), axis=1)
                acc = jnp.dot(a, wd_ref[slot, w_rows, :], preferred_element_type=F32)
                if not first:
                    acc = o_ref[0, rows(i), :] + acc
                if last:
                    y = x_ref[0, rows(i), :] + gate_ref[0, 0] * acc
                    acc = _rms(y) * gfin_ref[...]
                o_ref[0, rows(i), :] = acc
            ups.pop(u - 1, None)
            if u + 2 < len(units):
                ups[u + 2] = issue(u + 2)

    whole = list(range(FF_BODY_CHUNKS))

    @pl.when(b == 0)
    def _():
        start(0, 0)

    wait(0, 0)
    start(1, 1)
    run(0, 0, whole, first=True)
    wait(1, 1)

    def body(p, carry):
        slot = p % 2
        start(p + 1, 1 - slot)
        run(p, slot, whole)
        wait(p + 1, 1 - slot)
        return carry

    lax.fori_loop(1, nslabs - 1, body, 0)

    @pl.when(b + 1 < pl.num_programs(0))
    def _():
        start(0, 0)

    done = FF_BODY_CHUNKS * nslabs - nf
    run(nslabs - 1, (nslabs - 1) % 2, whole[done:], last=True)


def _ffn(x1, g, mod, wup, cw, cb, wdn, gfin):
    bsz, s, d = x1.shape
    whole = lambda a: pl.BlockSpec(a.shape, lambda b: (0,) * a.ndim, pipeline_mode=pl.Buffered(1))
    vec = lambda which: pl.BlockSpec((1, 1, 1, d), lambda b: (b, which, 0, 0))
    hbm = pl.BlockSpec(memory_space=pl.ANY)
    return pl.pallas_call(
        functools.partial(_ffn_kernel, row_chunk=256),
        grid=(bsz,),
        in_specs=[pl.BlockSpec((1, s, d), lambda b: (b, 0, 0)),
                  whole(g), vec(MOD_SCALE_F), vec(MOD_SHIFT_F), vec(MOD_GATE_F),
                  hbm, whole(cw), whole(cb), hbm, whole(gfin)],
        out_specs=pl.BlockSpec((1, s, d), lambda b: (b, 0, 0)),
        out_shape=jax.ShapeDtypeStruct((bsz, s, d), F32),
        scratch_shapes=[pltpu.VMEM((s, d), BF16),
                        pltpu.VMEM((2, d, FF_SLAB), BF16), pltpu.VMEM((2, d, FF_SLAB), BF16),
                        pltpu.VMEM((2, FF_SLAB, d), BF16), pltpu.SemaphoreType.DMA((2, 3))],
        compiler_params=pltpu.CompilerParams(dimension_semantics=("arbitrary",),
                                             vmem_limit_bytes=FFN_VMEM_LIMIT),
        name="ffn",
    )(x1, g, mod, mod, mod, wup, cw, cb, wdn, gfin)


def _rope_tables(s):
    half = HEAD_DIM // 2
    inv = ROPE_THETA ** (-np.arange(half, dtype=np.float64) / half)
    ang = np.arange(s, dtype=np.float64)[:, None] * inv[None, :]
    cos = np.cos(ang)
    sin = np.sin(ang)
    reps = LANES // HEAD_DIM
    return (jnp.asarray(np.tile(np.concatenate([cos, cos], axis=-1), (1, reps)), F32),
            jnp.asarray(np.tile(np.concatenate([-sin, sin], axis=-1), (1, reps)), F32))


def _sw_head_order(t, axis):
    shape = t.shape
    t = t.reshape(shape[:axis] + (SW_KV_HEADS, SW_GROUP, -1) + shape[axis + 1:])
    return jnp.swapaxes(t, axis, axis + 1).reshape(shape)


def kernel(x, c, w_ada, b_ada, g_attn, w_in, na_rpb, sw_sink, g_na_out, g_sw_out, w_out, g_ffn, w_up,
           conv_w, conv_b, w_down, g_final):
    bsz, s, d = x.shape
    depth = w_ada.shape[0]
    d_ff = w_down.shape[1]
    assert depth == 1, "the final rmsnorm is fused into the (only) layer's ffn call"
    assert d_ff % FF_CHUNK == 0 and s % GRID_W == 0 and s % SW_BLOCK == 0
    cos, sin = _rope_tables(s)
    sw_mask = _sw_mask_table()
    qb0 = 3 * NA_WIDTH
    for l in range(depth):
        mod = _ada(c, w_ada[l], b_ada[l]).reshape(bsz, 6, 1, d)

        wi = w_in[l].astype(BF16)
        wi = jnp.concatenate([wi[:, :qb0], _sw_head_order(wi[:, qb0:qb0 + SW_WIDTH], 1),
                              wi[:, qb0 + SW_WIDTH:]], axis=1)
        qa, ka, va, qb, kvb = _in_proj(x, g_attn[l].reshape(1, d), mod, wi, cos, sin)

        o_a = _na_attention(qa, ka, va, _na_bias_table(na_rpb[l].astype(F32) * LOG2E, s // GRID_W))
        wo = w_out[l].astype(BF16)
        wo = jnp.concatenate([wo[:NA_WIDTH], _sw_head_order(wo[NA_WIDTH:], 0)], axis=0)
        x = _sw_attention_out_proj(_sw_head_order(sw_sink[l].astype(F32) * LOG2E, 0), qb, kvb, sw_mask, x, o_a,
                                   g_na_out[l].reshape(1, -1), _sw_head_order(g_sw_out[l], 0).reshape(1, -1), wo, mod)

        nf = d_ff // FF_CHUNK
        cw = conv_w[l].reshape(CONV_W, nf, FF_CHUNK).transpose(1, 0, 2)
        x = _ffn(x, g_ffn[l].reshape(1, d), mod, w_up[l].astype(BF16), cw, conv_b[l].reshape(nf, 1, FF_CHUNK),
                 w_down[l].astype(BF16), g_final.reshape(1, d))
    return x
```

```python
import functools

import jax
import jax.numpy as jnp
import numpy as np
from jax import lax
from jax.experimental import pallas as pl
from jax.experimental.pallas import tpu as pltpu

F32 = jnp.float32
BF16 = jnp.bfloat16

HEAD_DIM = 64
LANES = 128
NA_HEADS = 8
NA_WIDTH = NA_HEADS * HEAD_DIM
GRID_W = 64
NA_ROWS_WIN = 8
NA_COLS = 16
SW_HEADS = 8
SW_KV_HEADS = 2
SW_GROUP = SW_HEADS // SW_KV_HEADS
SW_WIDTH = SW_HEADS * HEAD_DIM
SW_KV_WIDTH = SW_KV_HEADS * HEAD_DIM
SW_BLOCK = 128
CONV_W = 3
ROPE_THETA = 10000.0
EPS = 1e-6
NEG = -1e30
LOG2E = 1.4426950408889634
Q_SCALE = HEAD_DIM ** -0.5 * LOG2E

VMEM_LIMIT = 56 * 1024 * 1024
FFN_VMEM_LIMIT = 62 * 1024 * 1024

ADA_TN = 1024
IN_TM = 2048
IN_ROWS = 256
FF_CHUNK = 256
FF_BODY_CHUNKS = 3
FF_ROWS = 512
FF_SLAB = FF_BODY_CHUNKS * FF_CHUNK
NA_BODY_PAIRS = 4
NA_AHEAD = 4
SW_STEP_BLOCKS = 8
SW_AHEAD = 3
OUT_ROWS = 512
OUT_LAG = 3


def _rms(x):
    return x * lax.rsqrt(jnp.mean(x * x, axis=-1, keepdims=True) + EPS)


def _nt_dot(a, b):
    return lax.dot_general(a, b, (((1,), (1,)), ((), ())), preferred_element_type=F32)


def _lo_mask(rows):
    return lax.broadcasted_iota(jnp.int32, (rows, LANES), 1) < HEAD_DIM


def _split_bf16(a):
    hi = a.astype(BF16)
    return hi, (a - hi.astype(F32)).astype(BF16)


def _ada_kernel(c_ref, w_ref, b_ref, o_ref):
    c = c_ref[...]
    s_hi, s_lo = _split_bf16(c * jax.nn.sigmoid(c))
    w_hi, w_lo = _split_bf16(w_ref[...])
    dot = functools.partial(jnp.dot, preferred_element_type=F32)
    o_ref[...] = dot(s_hi, w_hi) + dot(s_hi, w_lo) + dot(s_lo, w_hi) + b_ref[...]


def _ada(c, w, b):
    bsz, d = c.shape
    n = w.shape[1]
    tn = ADA_TN
    return pl.pallas_call(
        _ada_kernel,
        grid=(n // tn,),
        in_specs=[pl.BlockSpec((bsz, d), lambda j: (0, 0)),
                  pl.BlockSpec((d, tn), lambda j: (0, j)),
                  pl.BlockSpec((1, tn), lambda j: (0, j))],
        out_specs=pl.BlockSpec((bsz, tn), lambda j: (0, j)),
        out_shape=jax.ShapeDtypeStruct((bsz, n), F32),
        compiler_params=pltpu.CompilerParams(dimension_semantics=("arbitrary",),
                                             vmem_limit_bytes=VMEM_LIMIT),
        name="ada",
    )(c, w, b.reshape(1, n))


MOD_SHIFT_A, MOD_SCALE_A, MOD_GATE_A, MOD_SHIFT_F, MOD_SCALE_F, MOD_GATE_F = range(6)


def _mod_spec(which, d):
    return pl.BlockSpec((1, 1, 1, d), lambda b, t: (b, which, 0, 0))


def _rope(t, cos, sin_signed, first_half):
    rot = jnp.where(first_half, pltpu.roll(t, LANES - HEAD_DIM // 2, 1), pltpu.roll(t, HEAD_DIM // 2, 1))
    return t * cos + rot * sin_signed


def _in_proj_kernel(x_ref, g_ref, scale_ref, shift_ref, w_ref, cos_ref, sin_ref,
                    qa_ref, ka_ref, va_ref, qb_ref, kvb_ref):
    gain = g_ref[...]
    scale = 1.0 + scale_ref[0, 0]
    shift = shift_ref[0, 0]
    first_half = (lax.broadcasted_iota(jnp.int32, (IN_ROWS, LANES), 1) % HEAD_DIM) < HEAD_DIM // 2
    base = 3 * NA_WIDTH

    for i in range(x_ref.shape[1] // IN_ROWS):
        rows = slice(i * IN_ROWS, (i + 1) * IN_ROWS)
        h = ((_rms(x_ref[0, rows, :]) * gain) * scale + shift).astype(BF16)
        cos = cos_ref[rows, :]
        sin = sin_ref[rows, :]

        def proj(lo, hi):
            return jnp.dot(h, w_ref[:, lo:hi], preferred_element_type=F32)

        for ref, col0, mult in ((qa_ref, 0, Q_SCALE), (ka_ref, NA_WIDTH, None), (va_ref, 2 * NA_WIDTH, None)):
            t = proj(col0, col0 + NA_WIDTH)
            t = t if mult is None else t * mult
            for hp in range(NA_WIDTH // LANES):
                ref[0, hp, rows, :] = t[:, hp * LANES:(hp + 1) * LANES].astype(BF16)
        qb = proj(base, base + SW_WIDTH)
        for j in range(SW_WIDTH // LANES):
            blk = _rope(qb[:, j * LANES:(j + 1) * LANES], cos, sin, first_half) * Q_SCALE
            qb_ref[0, rows, j * LANES:(j + 1) * LANES] = blk.astype(BF16)
        kv = proj(base + SW_WIDTH, base + SW_WIDTH + 2 * SW_KV_WIDTH)
        kvb_ref[0, rows, :LANES] = _rope(kv[:, :LANES], cos, sin, first_half).astype(BF16)
        kvb_ref[0, rows, LANES:] = kv[:, LANES:].astype(BF16)


def _in_proj(x, g, mod, w, cos, sin):
    bsz, s, d = x.shape
    n = w.shape[1]
    tm = IN_TM
    row = lambda b, t: (b, t, 0)
    pairs = NA_WIDTH // LANES
    outs = [jax.ShapeDtypeStruct((bsz, pairs, s, LANES), BF16)] * 3 + [
        jax.ShapeDtypeStruct((bsz, s, SW_WIDTH), BF16),
        jax.ShapeDtypeStruct((bsz, s, 2 * SW_KV_WIDTH), BF16)]
    return pl.pallas_call(
        _in_proj_kernel,
        grid=(bsz, s // tm),
        in_specs=[pl.BlockSpec((1, tm, d), row),
                  pl.BlockSpec((1, d), lambda b, t: (0, 0)),
                  _mod_spec(MOD_SCALE_A, d),
                  _mod_spec(MOD_SHIFT_A, d),
                  pl.BlockSpec((d, n), lambda b, t: (0, 0)),
                  pl.BlockSpec((tm, LANES), lambda b, t: (t, 0)),
                  pl.BlockSpec((tm, LANES), lambda b, t: (t, 0))],
        out_specs=[pl.BlockSpec((1, pairs, tm, LANES), lambda b, t: (b, 0, t, 0))] * 3 + [
            pl.BlockSpec((1, tm, SW_WIDTH), row),
            pl.BlockSpec((1, tm, 2 * SW_KV_WIDTH), row)],
        out_shape=outs,
        compiler_params=pltpu.CompilerParams(dimension_semantics=("arbitrary", "arbitrary"),
                                             vmem_limit_bytes=VMEM_LIMIT),
        name="in_proj",
    )(x, g, mod, mod, w, cos, sin)


def _na_kernel(q_ref, k_ref, v_ref, bias_ref, wup_ref, wdn_ref, o_ref, wup_bf_ref, wdn_bf_ref, *, rows):
    lo = _lo_mask(GRID_W)
    win = NA_ROWS_WIN * GRID_W
    ones = jnp.ones((win, LANES), BF16)

    def scores(hp, r):
        rs = jnp.clip(r - NA_ROWS_WIN // 2, 0, rows - NA_ROWS_WIN)
        q = q_ref[0, hp, pl.ds(pl.multiple_of(r * GRID_W, GRID_W), GRID_W), :]
        k0 = pl.multiple_of(rs * GRID_W, GRID_W)
        zero = jnp.zeros_like(q)
        qs = jnp.concatenate([jnp.where(lo, q, zero), jnp.where(lo, zero, q)], axis=0)
        dr0 = NA_ROWS_WIN - 1 - (r - rs)
        bias = jnp.concatenate(
            [jnp.concatenate([bias_ref[2 * hp + head, dr0 + 2 * jj] for jj in range(NA_ROWS_WIN // 2)], axis=1)
             for head in range(2)], axis=0)
        return _nt_dot(qs, k_ref[0, hp, pl.ds(k0, win), :]) + bias, k0

    def finish(hp, r, s, k0):
        p = jnp.exp2(s - jnp.max(s, axis=-1, keepdims=True)).astype(BF16)
        oa = jnp.dot(p, jnp.concatenate([v_ref[0, hp, pl.ds(k0, win), :], ones], axis=1),
                     preferred_element_type=F32)
        o = oa[:, :LANES] * (1.0 / oa[:, LANES:])
        out = jnp.where(lo, o[:GRID_W], o[GRID_W:])
        o_ref[0, hp, pl.ds(pl.multiple_of(r * GRID_W, GRID_W), GRID_W), :] = out.astype(o_ref.dtype)

    def body(it, carry):
        units = [(NA_BODY_PAIRS * it + j, r) for j in range(NA_BODY_PAIRS) for r in range(rows)]
        wup_bf_ref[...] = wup_ref[...].astype(BF16)
        wdn_bf_ref[...] = wdn_ref[...].astype(BF16)
        staged = [scores(*units[i]) for i in range(NA_AHEAD)]
        for i, unit in enumerate(units):
            if i + NA_AHEAD < len(units):
                staged.append(scores(*units[i + NA_AHEAD]))
            finish(*unit, *staged[i])
            staged[i] = None
        return carry

    lax.fori_loop(0, q_ref.shape[1] // NA_BODY_PAIRS, body, 0)


def _na_attention(q, k, v, bias, wup, wdn):
    bsz, pairs, s, _ = q.shape
    rows = s // GRID_W
    assert pairs == NA_BODY_PAIRS and wup.shape[0] % (16 * bsz) == 0 and wdn.shape[0] % (16 * bsz) == 0
    blk = pl.BlockSpec((1, pairs, s, LANES), lambda b: (b, 0, 0, 0))
    row_share = lambda w: pl.BlockSpec((w.shape[0] // bsz, w.shape[1]), lambda b: (b, 0))
    return pl.pallas_call(
        functools.partial(_na_kernel, rows=rows),
        grid=(bsz,),
        in_specs=[blk, blk, blk, pl.BlockSpec(bias.shape, lambda b: (0, 0, 0, 0)), row_share(wup), row_share(wdn)],
        out_specs=[blk, row_share(wup), row_share(wdn)],
        out_shape=[jax.ShapeDtypeStruct(q.shape, BF16), jax.ShapeDtypeStruct(wup.shape, BF16),
                   jax.ShapeDtypeStruct(wdn.shape, BF16)],
        compiler_params=pltpu.CompilerParams(dimension_semantics=("arbitrary",),
                                             vmem_limit_bytes=VMEM_LIMIT),
        name="na_attn",
    )(q, k, v, bias, wup, wdn)


def _na_bias_table(rpb, rows):
    assert rows >= 2 * NA_ROWS_WIN
    col = np.arange(GRID_W)
    cs = np.clip(col - NA_COLS // 2, 0, GRID_W - NA_COLS)
    col_ok = (col[None, :] >= cs[:, None]) & (col[None, :] < cs[:, None] + NA_COLS)
    dc = np.clip(col[None, :] - col[:, None] + NA_COLS - 1, 0, 2 * NA_COLS - 2)
    onehot = (np.arange(2 * NA_COLS - 1)[:, None] == dc.reshape(1, -1)).astype(np.float32)
    t = jnp.dot(rpb.reshape(-1, 2 * NA_COLS - 1), onehot, precision=lax.Precision.HIGHEST)
    t = jnp.where(col_ok[None, None], t.reshape(NA_HEADS, 2 * NA_ROWS_WIN - 1, GRID_W, GRID_W), NEG)
    return jnp.concatenate([t[:, :-1], t[:, 1:]], axis=-1)


def _sw_out_kernel(sink_ref, q_ref, k_ref, v_ref, mask_ref, x_ref, oa_ref, gna_ref, gsw_ref, w_ref, gate_ref,
                   o_ref, ob_ref, *, nblocks):
    lo = _lo_mask(SW_BLOCK)
    zero = jnp.zeros((SW_BLOCK, LANES), q_ref.dtype)
    ones = jnp.ones((3 * SW_BLOCK, LANES), BF16)
    npairs = SW_WIDTH // LANES

    def window(u):
        n = pl.program_id(1) * SW_STEP_BLOCKS + u
        start = pl.multiple_of(jnp.clip(n - 1, 0, nblocks - 3) * SW_BLOCK, SW_BLOCK)
        variant = jnp.where(n == 0, 0, jnp.where(n == nblocks - 1, 2, 1))
        return start, variant

    def scores(u, j):
        start, variant = window(u)
        qj = q_ref[0, u * SW_BLOCK:(u + 1) * SW_BLOCK, j * LANES:(j + 1) * LANES]
        qs = jnp.concatenate([jnp.where(lo, qj, zero), jnp.where(lo, zero, qj)], axis=0)
        s = _nt_dot(qs, k_ref[0, pl.ds(start, 3 * SW_BLOCK), :])
        return s.reshape(2, SW_BLOCK, 3 * SW_BLOCK) + mask_ref[variant]

    def finish(u, j, s):
        start, _ = window(u)
        ps, sinks = [], []
        for half in range(2):
            sg = s[half]
            sk = sink_ref[2 * j + half]
            m = jnp.maximum(jnp.max(sg, axis=-1, keepdims=True), sk)
            ps.append(jnp.exp2(sg - m).astype(BF16))
            sinks.append(jnp.exp2(sk - m))
        oa = jnp.dot(jnp.concatenate(ps, axis=0),
                     jnp.concatenate([v_ref[0, pl.ds(start, 3 * SW_BLOCK), :], ones], axis=1),
                     preferred_element_type=F32)
        o = oa[:, :LANES] * (1.0 / (oa[:, LANES:] + jnp.concatenate(sinks, axis=0)))
        ob_ref[u * SW_BLOCK:(u + 1) * SW_BLOCK, j * LANES:(j + 1) * LANES] = jnp.where(lo, o[:SW_BLOCK], o[SW_BLOCK:])

    def project(t):
        rows = slice(t * OUT_ROWS, (t + 1) * OUT_ROWS)
        oa = jnp.concatenate([oa_ref[0, hp, rows, :] for hp in range(NA_WIDTH // LANES)], axis=1)
        oa = (_rms(oa.astype(F32)) * gna_ref[...]).astype(BF16)
        ob = (_rms(ob_ref[rows, :]) * gsw_ref[...]).astype(BF16)
        mix = (jnp.dot(oa, w_ref[:NA_WIDTH], preferred_element_type=F32)
               + jnp.dot(ob, w_ref[NA_WIDTH:], preferred_element_type=F32))
        o_ref[0, rows, :] = x_ref[0, rows, :] + gate_ref[0, 0] * mix

    per_tile = npairs * OUT_ROWS // SW_BLOCK
    units = [(u, j) for u in range(SW_STEP_BLOCKS) for j in range(npairs)]
    staged = [scores(*units[i]) for i in range(SW_AHEAD)]
    for i, unit in enumerate(units):
        if i + SW_AHEAD < len(units):
            staged.append(scores(*units[i + SW_AHEAD]))
        finish(*unit, staged[i])
        staged[i] = None
        if i % per_tile == OUT_LAG and i > per_tile:
            project(i // per_tile - 1)
    project(len(units) // per_tile - 1)


def _sw_attention_out_proj(sink, q, kv, mask, x, oa, gna, gsw, w, mod):
    bsz, s, d = x.shape
    nblocks = s // SW_BLOCK
    assert nblocks >= 3 and nblocks % SW_STEP_BLOCKS == 0
    rows = SW_STEP_BLOCKS * SW_BLOCK
    tile = lambda b, n, sk: (b, n, 0)
    const = lambda a: pl.BlockSpec(a.shape, lambda b, n, sk: (0,) * a.ndim)
    grid_spec = pltpu.PrefetchScalarGridSpec(
        num_scalar_prefetch=1,
        grid=(bsz, nblocks // SW_STEP_BLOCKS),
        in_specs=[pl.BlockSpec((1, rows, SW_WIDTH), tile),
                  pl.BlockSpec((1, s, LANES), lambda b, n, sk: (b, 0, 0)),
                  pl.BlockSpec((1, s, LANES), lambda b, n, sk: (b, 0, 1)),
                  const(mask),
                  pl.BlockSpec((1, rows, d), tile),
                  pl.BlockSpec((1, NA_WIDTH // LANES, rows, LANES), lambda b, n, sk: (b, 0, n, 0)),
                  const(gna), const(gsw), const(w),
                  pl.BlockSpec((1, 1, 1, d), lambda b, n, sk: (b, MOD_GATE_A, 0, 0))],
        out_specs=pl.BlockSpec((1, rows, d), tile),
        scratch_shapes=[pltpu.VMEM((rows, SW_WIDTH), F32)],
    )
    return pl.pallas_call(
        functools.partial(_sw_out_kernel, nblocks=nblocks),
        grid_spec=grid_spec,
        out_shape=jax.ShapeDtypeStruct((bsz, s, d), F32),
        compiler_params=pltpu.CompilerParams(dimension_semantics=("arbitrary", "arbitrary"),
                                             vmem_limit_bytes=VMEM_LIMIT),
        name="sw_attn_out_proj",
    )(sink, q, kv, kv, mask, x, oa, gna, gsw, w, mod)


def _sw_mask_table():
    i = np.arange(SW_BLOCK)[:, None]
    j = np.arange(3 * SW_BLOCK)[None, :]
    ok = np.stack([np.abs(i - j) <= SW_BLOCK,
                   np.abs(i + SW_BLOCK - j) <= SW_BLOCK,
                   np.abs(i + 2 * SW_BLOCK - j) <= SW_BLOCK])
    return jnp.asarray(np.where(ok, 0.0, NEG), F32)


def _ffn_kernel(x_ref, g_ref, scale_ref, shift_ref, gate_ref, wup_hbm, cw_ref, cb_ref, wdn_hbm, gfin_ref,
                o_ref, h_ref, wv_ref, wg_ref, wd_ref, sems, *, row_chunk):
    s = x_ref.shape[1]
    nf = cw_ref.shape[0]
    nblk = s // FF_ROWS
    nslabs = pl.cdiv(nf, FF_BODY_CHUNKS)
    assert nslabs % 2 == 0
    d_ff = nf * FF_CHUNK
    b = pl.program_id(0)

    def slab_start(p):
        if isinstance(p, int):
            return min(p * FF_SLAB, d_ff - FF_SLAB)
        return pl.multiple_of(jnp.minimum(p * FF_SLAB, d_ff - FF_SLAB), FF_CHUNK)

    def slab_copies(p, slot):
        c0 = slab_start(p)
        return (pltpu.make_async_copy(wup_hbm.at[:, pl.ds(c0, FF_SLAB)], wv_ref.at[slot], sems.at[slot, 0]),
                pltpu.make_async_copy(wup_hbm.at[:, pl.ds(d_ff + c0, FF_SLAB)], wg_ref.at[slot], sems.at[slot, 1]),
                pltpu.make_async_copy(wdn_hbm.at[pl.ds(c0, FF_SLAB), :], wd_ref.at[slot], sems.at[slot, 2]))

    def start(p, slot):
        for cp in slab_copies(p, slot):
            cp.start()

    def wait(p, slot):
        for cp in slab_copies(p, slot):
            cp.wait()

    ridx = lax.broadcasted_iota(jnp.int32, (FF_ROWS, FF_CHUNK), 0)
    edge = jnp.zeros((1, FF_CHUNK), F32)

    def rows(i):
        return slice(i * FF_ROWS, (i + 1) * FF_ROWS)

    def norm_rows(i):
        gain = g_ref[...] * (1.0 + scale_ref[0, 0])
        for r0 in range(i * FF_ROWS, (i + 1) * FF_ROWS, row_chunk):
            xr = x_ref[0, r0:r0 + row_chunk, :]
            h_ref[r0:r0 + row_chunk, :] = (_rms(xr) * gain + shift_ref[0, 0]).astype(BF16)

    def up(slot, lc, i):
        h = h_ref[rows(i), :]
        cols = slice(lc * FF_CHUNK, (lc + 1) * FF_CHUNK)
        return (jnp.dot(h, wv_ref[slot, :, cols], preferred_element_type=F32),
                jnp.dot(h, wg_ref[slot, :, cols], preferred_element_type=F32))

    def act(c, i, cur, before, after):
        val, gt = cur
        before = before[1][FF_ROWS - 1:] if i > 0 else edge
        after = after[1][:1] if i < nblk - 1 else edge
        prev = jnp.where(ridx == 0, before, pltpu.roll(gt, 1, 0))
        nxt = jnp.where(ridx == FF_ROWS - 1, after, pltpu.roll(gt, FF_ROWS - 1, 0))
        cw = cw_ref[c]
        gc = prev * cw[0:1] + gt * cw[1:2] + nxt * cw[2:3] + cb_ref[c]
        return (gc * jax.nn.sigmoid(gc) * val).astype(BF16)

    def run(p, slot, local_chunks, first=False, last=False):
        chunk0 = slab_start(p) // FF_CHUNK
        units = [(lc, i) for lc in local_chunks for i in range(nblk)]

        def issue(u):
            if first and u < nblk:
                norm_rows(units[u][1])
            return up(slot, *units[u])

        ups = {0: issue(0), 1: issue(1)}
        w_rows = slice(local_chunks[0] * FF_CHUNK, (local_chunks[-1] + 1) * FF_CHUNK)
        gated = {i: [] for i in range(nblk)}
        for u, (lc, i) in enumerate(units):
            gated[i].append(act(chunk0 + lc, i, ups[u], ups.get(u - 1), ups.get(u + 1)))
            if u >= len(units) - nblk:
                a = jnp.concatenate(gated.pop(i), axis=1)
                acc = jnp.dot(a, wd_ref[slot, w_rows, :], preferred_element_type=F32)
                if not first:
                    acc = o_ref[0, rows(i), :] + acc
                if last:
                    y = x_ref[0, rows(i), :] + gate_ref[0, 0] * acc
                    acc = _rms(y) * gfin_ref[...]
                o_ref[0, rows(i), :] = acc
            ups.pop(u - 1, None)
            if u + 2 < len(units):
                ups[u + 2] = issue(u + 2)

    whole = list(range(FF_BODY_CHUNKS))

    @pl.when(b == 0)
    def _():
        start(0, 0)

    wait(0, 0)
    start(1, 1)
    run(0, 0, whole, first=True)
    wait(1, 1)

    def body(p, carry):
        slot = p % 2
        start(p + 1, 1 - slot)
        run(p, slot, whole)
        wait(p + 1, 1 - slot)
        return carry

    lax.fori_loop(1, nslabs - 1, body, 0)

    @pl.when(b + 1 < pl.num_programs(0))
    def _():
        start(0, 0)

    done = FF_BODY_CHUNKS * nslabs - nf
    run(nslabs - 1, (nslabs - 1) % 2, whole[done:], last=True)


def _ffn(x1, g, mod, wup, cw, cb, wdn, gfin):
    bsz, s, d = x1.shape
    whole = lambda a: pl.BlockSpec(a.shape, lambda b: (0,) * a.ndim, pipeline_mode=pl.Buffered(1))
    vec = lambda which: pl.BlockSpec((1, 1, 1, d), lambda b: (b, which, 0, 0))
    hbm = pl.BlockSpec(memory_space=pl.ANY)
    return pl.pallas_call(
        functools.partial(_ffn_kernel, row_chunk=256),
        grid=(bsz,),
        in_specs=[pl.BlockSpec((1, s, d), lambda b: (b, 0, 0)),
                  whole(g), vec(MOD_SCALE_F), vec(MOD_SHIFT_F), vec(MOD_GATE_F),
                  hbm, whole(cw), whole(cb), hbm, whole(gfin)],
        out_specs=pl.BlockSpec((1, s, d), lambda b: (b, 0, 0)),
        out_shape=jax.ShapeDtypeStruct((bsz, s, d), F32),
        scratch_shapes=[pltpu.VMEM((s, d), BF16),
                        pltpu.VMEM((2, d, FF_SLAB), BF16), pltpu.VMEM((2, d, FF_SLAB), BF16),
                        pltpu.VMEM((2, FF_SLAB, d), BF16), pltpu.SemaphoreType.DMA((2, 3))],
        compiler_params=pltpu.CompilerParams(dimension_semantics=("arbitrary",),
                                             vmem_limit_bytes=FFN_VMEM_LIMIT),
        name="ffn",
    )(x1, g, mod, mod, mod, wup, cw, cb, wdn, gfin)


def _rope_tables(s):
    half = HEAD_DIM // 2
    inv = ROPE_THETA ** (-np.arange(half, dtype=np.float64) / half)
    ang = np.arange(s, dtype=np.float64)[:, None] * inv[None, :]
    cos = np.cos(ang)
    sin = np.sin(ang)
    reps = LANES // HEAD_DIM
    return (jnp.asarray(np.tile(np.concatenate([cos, cos], axis=-1), (1, reps)), F32),
            jnp.asarray(np.tile(np.concatenate([-sin, sin], axis=-1), (1, reps)), F32))


def _sw_head_order(t, axis):
    shape = t.shape
    t = t.reshape(shape[:axis] + (SW_KV_HEADS, SW_GROUP, -1) + shape[axis + 1:])
    return jnp.swapaxes(t, axis, axis + 1).reshape(shape)


def kernel(x, c, w_ada, b_ada, g_attn, w_in, na_rpb, sw_sink, g_na_out, g_sw_out, w_out, g_ffn, w_up,
           conv_w, conv_b, w_down, g_final):
    bsz, s, d = x.shape
    depth = w_ada.shape[0]
    d_ff = w_down.shape[1]
    assert depth == 1, "the final rmsnorm is fused into the (only) layer's ffn call"
    assert d_ff % FF_CHUNK == 0 and s % GRID_W == 0 and s % SW_BLOCK == 0
    cos, sin = _rope_tables(s)
    sw_mask = _sw_mask_table()
    qb0 = 3 * NA_WIDTH
    for l in range(depth):
        mod = _ada(c, w_ada[l], b_ada[l]).reshape(bsz, 6, 1, d)

        wi = w_in[l].astype(BF16)
        wi = jnp.concatenate([wi[:, :qb0], _sw_head_order(wi[:, qb0:qb0 + SW_WIDTH], 1),
                              wi[:, qb0 + SW_WIDTH:]], axis=1)
        qa, ka, va, qb, kvb = _in_proj(x, g_attn[l].reshape(1, d), mod, wi, cos, sin)

        o_a, wu, wd = _na_attention(qa, ka, va, _na_bias_table(na_rpb[l].astype(F32) * LOG2E, s // GRID_W),
                                    w_up[l], w_down[l])
        wo = w_out[l].astype(BF16)
        wo = jnp.concatenate([wo[:NA_WIDTH], _sw_head_order(wo[NA_WIDTH:], 0)], axis=0)
        x = _sw_attention_out_proj(_sw_head_order(sw_sink[l].astype(F32) * LOG2E, 0), qb, kvb, sw_mask, x, o_a,
                                   g_na_out[l].reshape(1, -1), _sw_head_order(g_sw_out[l], 0).reshape(1, -1), wo, mod)

        nf = d_ff // FF_CHUNK
        cw = conv_w[l].reshape(CONV_W, nf, FF_CHUNK).transpose(1, 0, 2)
        x = _ffn(x, g_ffn[l].reshape(1, d), mod, wu, cw, conv_b[l].reshape(nf, 1, FF_CHUNK), wd,
                 g_final.reshape(1, d))
    return x
```

```python
import functools

import jax
import jax.numpy as jnp
import numpy as np
from jax import lax
from jax.experimental import pallas as pl
from jax.experimental.pallas import tpu as pltpu

F32 = jnp.float32
BF16 = jnp.bfloat16

HEAD_DIM = 64
LANES = 128
NA_HEADS = 8
NA_WIDTH = NA_HEADS * HEAD_DIM
GRID_W = 64
NA_ROWS_WIN = 8
NA_COLS = 16
SW_HEADS = 8
SW_KV_HEADS = 2
SW_GROUP = SW_HEADS // SW_KV_HEADS
SW_WIDTH = SW_HEADS * HEAD_DIM
SW_KV_WIDTH = SW_KV_HEADS * HEAD_DIM
SW_BLOCK = 128
CONV_W = 3
ROPE_THETA = 10000.0
EPS = 1e-6
NEG = -1e30
LOG2E = 1.4426950408889634
Q_SCALE = HEAD_DIM ** -0.5 * LOG2E

VMEM_LIMIT = 56 * 1024 * 1024
FFN_VMEM_LIMIT = 62 * 1024 * 1024

ADA_TN = 1024
IN_TM = 2048
IN_ROWS = 256
FF_CHUNK = 256
FF_BODY_CHUNKS = 3
FF_ROWS = 512
FF_SLAB = FF_BODY_CHUNKS * FF_CHUNK
NA_BODY_PAIRS = 4
NA_AHEAD = 4
SW_STEP_BLOCKS = 16
SW_AHEAD = 3
OUT_ROWS = 512
OUT_LAG = 3


def _rms(x):
    return x * lax.rsqrt(jnp.mean(x * x, axis=-1, keepdims=True) + EPS)


def _nt_dot(a, b):
    return lax.dot_general(a, b, (((1,), (1,)), ((), ())), preferred_element_type=F32)


def _lo_mask(rows):
    return lax.broadcasted_iota(jnp.int32, (rows, LANES), 1) < HEAD_DIM


def _split_bf16(a):
    hi = a.astype(BF16)
    return hi, (a - hi.astype(F32)).astype(BF16)


def _ada_kernel(c_ref, w_ref, b_ref, o_ref):
    c = c_ref[...]
    s_hi, s_lo = _split_bf16(c * jax.nn.sigmoid(c))
    w_hi, w_lo = _split_bf16(w_ref[...])
    dot = functools.partial(jnp.dot, preferred_element_type=F32)
    o_ref[...] = dot(s_hi, w_hi) + dot(s_hi, w_lo) + dot(s_lo, w_hi) + b_ref[...]


def _ada(c, w, b):
    bsz, d = c.shape
    n = w.shape[1]
    tn = ADA_TN
    return pl.pallas_call(
        _ada_kernel,
        grid=(n // tn,),
        in_specs=[pl.BlockSpec((bsz, d), lambda j: (0, 0)),
                  pl.BlockSpec((d, tn), lambda j: (0, j)),
                  pl.BlockSpec((1, tn), lambda j: (0, j))],
        out_specs=pl.BlockSpec((bsz, tn), lambda j: (0, j)),
        out_shape=jax.ShapeDtypeStruct((bsz, n), F32),
        compiler_params=pltpu.CompilerParams(dimension_semantics=("arbitrary",),
                                             vmem_limit_bytes=VMEM_LIMIT),
        name="ada",
    )(c, w, b.reshape(1, n))


MOD_SHIFT_A, MOD_SCALE_A, MOD_GATE_A, MOD_SHIFT_F, MOD_SCALE_F, MOD_GATE_F = range(6)


def _mod_spec(which, d):
    return pl.BlockSpec((1, 1, 1, d), lambda b, t: (b, which, 0, 0))


def _rope(t, cos, sin_signed, first_half):
    rot = jnp.where(first_half, pltpu.roll(t, LANES - HEAD_DIM // 2, 1), pltpu.roll(t, HEAD_DIM // 2, 1))
    return t * cos + rot * sin_signed


def _in_proj_kernel(x_ref, g_ref, scale_ref, shift_ref, w_ref, cos_ref, sin_ref,
                    qa_ref, ka_ref, va_ref, qb_ref, kvb_ref):
    gain = g_ref[...]
    scale = 1.0 + scale_ref[0, 0]
    shift = shift_ref[0, 0]
    first_half = (lax.broadcasted_iota(jnp.int32, (IN_ROWS, LANES), 1) % HEAD_DIM) < HEAD_DIM // 2
    base = 3 * NA_WIDTH

    for i in range(x_ref.shape[1] // IN_ROWS):
        rows = slice(i * IN_ROWS, (i + 1) * IN_ROWS)
        h = ((_rms(x_ref[0, rows, :]) * gain) * scale + shift).astype(BF16)
        cos = cos_ref[rows, :]
        sin = sin_ref[rows, :]

        def proj(lo, hi):
            return jnp.dot(h, w_ref[:, lo:hi], preferred_element_type=F32)

        for ref, col0, mult in ((qa_ref, 0, Q_SCALE), (ka_ref, NA_WIDTH, None), (va_ref, 2 * NA_WIDTH, None)):
            t = proj(col0, col0 + NA_WIDTH)
            t = t if mult is None else t * mult
            for hp in range(NA_WIDTH // LANES):
                ref[0, hp, rows, :] = t[:, hp * LANES:(hp + 1) * LANES].astype(BF16)
        qb = proj(base, base + SW_WIDTH)
        for j in range(SW_WIDTH // LANES):
            blk = _rope(qb[:, j * LANES:(j + 1) * LANES], cos, sin, first_half) * Q_SCALE
            qb_ref[0, rows, j * LANES:(j + 1) * LANES] = blk.astype(BF16)
        kv = proj(base + SW_WIDTH, base + SW_WIDTH + 2 * SW_KV_WIDTH)
        kvb_ref[0, rows, :LANES] = _rope(kv[:, :LANES], cos, sin, first_half).astype(BF16)
        kvb_ref[0, rows, LANES:] = kv[:, LANES:].astype(BF16)


def _in_proj(x, g, mod, w, cos, sin):
    bsz, s, d = x.shape
    n = w.shape[1]
    tm = IN_TM
    row = lambda b, t: (b, t, 0)
    pairs = NA_WIDTH // LANES
    outs = [jax.ShapeDtypeStruct((bsz, pairs, s, LANES), BF16)] * 3 + [
        jax.ShapeDtypeStruct((bsz, s, SW_WIDTH), BF16),
        jax.ShapeDtypeStruct((bsz, s, 2 * SW_KV_WIDTH), BF16)]
    return pl.pallas_call(
        _in_proj_kernel,
        grid=(bsz, s // tm),
        in_specs=[pl.BlockSpec((1, tm, d), row),
                  pl.BlockSpec((1, d), lambda b, t: (0, 0)),
                  _mod_spec(MOD_SCALE_A, d),
                  _mod_spec(MOD_SHIFT_A, d),
                  pl.BlockSpec((d, n), lambda b, t: (0, 0)),
                  pl.BlockSpec((tm, LANES), lambda b, t: (t, 0)),
                  pl.BlockSpec((tm, LANES), lambda b, t: (t, 0))],
        out_specs=[pl.BlockSpec((1, pairs, tm, LANES), lambda b, t: (b, 0, t, 0))] * 3 + [
            pl.BlockSpec((1, tm, SW_WIDTH), row),
            pl.BlockSpec((1, tm, 2 * SW_KV_WIDTH), row)],
        out_shape=outs,
        compiler_params=pltpu.CompilerParams(dimension_semantics=("arbitrary", "arbitrary"),
                                             vmem_limit_bytes=VMEM_LIMIT),
        name="in_proj",
    )(x, g, mod, mod, w, cos, sin)


def _na_kernel(q_ref, k_ref, v_ref, bias_ref, wup_ref, wdn_ref, wout_ref, o_ref, wup_bf_ref, wdn_bf_ref, wout_bf_ref,
               *, rows):
    lo = _lo_mask(GRID_W)
    win = NA_ROWS_WIN * GRID_W
    ones = jnp.ones((win, LANES), BF16)

    def scores(hp, r):
        rs = jnp.clip(r - NA_ROWS_WIN // 2, 0, rows - NA_ROWS_WIN)
        q = q_ref[0, hp, pl.ds(pl.multiple_of(r * GRID_W, GRID_W), GRID_W), :]
        k0 = pl.multiple_of(rs * GRID_W, GRID_W)
        zero = jnp.zeros_like(q)
        qs = jnp.concatenate([jnp.where(lo, q, zero), jnp.where(lo, zero, q)], axis=0)
        dr0 = NA_ROWS_WIN - 1 - (r - rs)
        bias = jnp.concatenate(
            [jnp.concatenate([bias_ref[2 * hp + head, dr0 + 2 * jj] for jj in range(NA_ROWS_WIN // 2)], axis=1)
             for head in range(2)], axis=0)
        return _nt_dot(qs, k_ref[0, hp, pl.ds(k0, win), :]) + bias, k0

    def finish(hp, r, s, k0):
        p = jnp.exp2(s - jnp.max(s, axis=-1, keepdims=True)).astype(BF16)
        oa = jnp.dot(p, jnp.concatenate([v_ref[0, hp, pl.ds(k0, win), :], ones], axis=1),
                     preferred_element_type=F32)
        o = oa[:, :LANES] * (1.0 / oa[:, LANES:])
        out = jnp.where(lo, o[:GRID_W], o[GRID_W:])
        o_ref[0, hp, pl.ds(pl.multiple_of(r * GRID_W, GRID_W), GRID_W), :] = out.astype(o_ref.dtype)

    def body(it, carry):
        units = [(NA_BODY_PAIRS * it + j, r) for j in range(NA_BODY_PAIRS) for r in range(rows)]
        wup_bf_ref[...] = wup_ref[...].astype(BF16)
        wdn_bf_ref[...] = wdn_ref[...].astype(BF16)
        wout_bf_ref[...] = wout_ref[...].astype(BF16)
        staged = [scores(*units[i]) for i in range(NA_AHEAD)]
        for i, unit in enumerate(units):
            if i + NA_AHEAD < len(units):
                staged.append(scores(*units[i + NA_AHEAD]))
            finish(*unit, *staged[i])
            staged[i] = None
        return carry

    lax.fori_loop(0, q_ref.shape[1] // NA_BODY_PAIRS, body, 0)


def _na_attention(q, k, v, bias, wup, wdn, wout):
    bsz, pairs, s, _ = q.shape
    rows = s // GRID_W
    assert pairs == NA_BODY_PAIRS and wup.shape[0] % (16 * bsz) == 0 and wdn.shape[0] % (16 * bsz) == 0
    blk = pl.BlockSpec((1, pairs, s, LANES), lambda b: (b, 0, 0, 0))
    row_share = lambda w: pl.BlockSpec((w.shape[0] // bsz, w.shape[1]), lambda b: (b, 0))
    assert wout.shape[0] == bsz * HEAD_DIM and NA_HEADS + SW_HEADS == bsz

    def wout_src(b):
        j = b - NA_HEADS
        return jnp.where(b < NA_HEADS, b, NA_HEADS + (j % SW_KV_HEADS) * SW_GROUP + j // SW_KV_HEADS), 0

    return pl.pallas_call(
        functools.partial(_na_kernel, rows=rows),
        grid=(bsz,),
        in_specs=[blk, blk, blk, pl.BlockSpec(bias.shape, lambda b: (0, 0, 0, 0)), row_share(wup), row_share(wdn),
                  pl.BlockSpec((HEAD_DIM, wout.shape[1]), wout_src)],
        out_specs=[blk, row_share(wup), row_share(wdn), row_share(wout)],
        out_shape=[jax.ShapeDtypeStruct(q.shape, BF16), jax.ShapeDtypeStruct(wup.shape, BF16),
                   jax.ShapeDtypeStruct(wdn.shape, BF16), jax.ShapeDtypeStruct(wout.shape, BF16)],
        compiler_params=pltpu.CompilerParams(dimension_semantics=("arbitrary",),
                                             vmem_limit_bytes=VMEM_LIMIT),
        name="na_attn",
    )(q, k, v, bias, wup, wdn, wout)


def _na_bias_table(rpb, rows):
    assert rows >= 2 * NA_ROWS_WIN
    col = np.arange(GRID_W)
    cs = np.clip(col - NA_COLS // 2, 0, GRID_W - NA_COLS)
    col_ok = (col[None, :] >= cs[:, None]) & (col[None, :] < cs[:, None] + NA_COLS)
    dc = np.clip(col[None, :] - col[:, None] + NA_COLS - 1, 0, 2 * NA_COLS - 2)
    onehot = (np.arange(2 * NA_COLS - 1)[:, None] == dc.reshape(1, -1)).astype(np.float32)
    t = jnp.dot(rpb.reshape(-1, 2 * NA_COLS - 1), onehot, precision=lax.Precision.HIGHEST)
    t = jnp.where(col_ok[None, None], t.reshape(NA_HEADS, 2 * NA_ROWS_WIN - 1, GRID_W, GRID_W), NEG)
    return jnp.concatenate([t[:, :-1], t[:, 1:]], axis=-1)


def _sw_out_kernel(sink_ref, q_ref, k_ref, v_ref, mask_ref, x_ref, oa_ref, gna_ref, gsw_ref, w_ref, gate_ref,
                   o_ref, ob_ref, *, nblocks):
    lo = _lo_mask(SW_BLOCK)
    zero = jnp.zeros((SW_BLOCK, LANES), q_ref.dtype)
    ones = jnp.ones((3 * SW_BLOCK, LANES), BF16)
    npairs = SW_WIDTH // LANES

    def window(u):
        n = pl.program_id(1) * SW_STEP_BLOCKS + u
        start = pl.multiple_of(jnp.clip(n - 1, 0, nblocks - 3) * SW_BLOCK, SW_BLOCK)
        variant = jnp.where(n == 0, 0, jnp.where(n == nblocks - 1, 2, 1))
        return start, variant

    def scores(u, j):
        start, variant = window(u)
        qj = q_ref[0, u * SW_BLOCK:(u + 1) * SW_BLOCK, j * LANES:(j + 1) * LANES]
        qs = jnp.concatenate([jnp.where(lo, qj, zero), jnp.where(lo, zero, qj)], axis=0)
        s = _nt_dot(qs, k_ref[0, pl.ds(start, 3 * SW_BLOCK), :])
        return s.reshape(2, SW_BLOCK, 3 * SW_BLOCK) + mask_ref[variant]

    def finish(u, j, s):
        start, _ = window(u)
        ps, sinks = [], []
        for half in range(2):
            sg = s[half]
            sk = sink_ref[2 * j + half]
            m = jnp.maximum(jnp.max(sg, axis=-1, keepdims=True), sk)
            ps.append(jnp.exp2(sg - m).astype(BF16))
            sinks.append(jnp.exp2(sk - m))
        oa = jnp.dot(jnp.concatenate(ps, axis=0),
                     jnp.concatenate([v_ref[0, pl.ds(start, 3 * SW_BLOCK), :], ones], axis=1),
                     preferred_element_type=F32)
        o = oa[:, :LANES] * (1.0 / (oa[:, LANES:] + jnp.concatenate(sinks, axis=0)))
        ob_ref[u * SW_BLOCK:(u + 1) * SW_BLOCK, j * LANES:(j + 1) * LANES] = jnp.where(lo, o[:SW_BLOCK], o[SW_BLOCK:])

    def project(t):
        rows = slice(t * OUT_ROWS, (t + 1) * OUT_ROWS)
        oa = jnp.concatenate([oa_ref[0, hp, rows, :] for hp in range(NA_WIDTH // LANES)], axis=1)
        oa = (_rms(oa.astype(F32)) * gna_ref[...]).astype(BF16)
        ob = (_rms(ob_ref[rows, :]) * gsw_ref[...]).astype(BF16)
        mix = (jnp.dot(oa, w_ref[:NA_WIDTH], preferred_element_type=F32)
               + jnp.dot(ob, w_ref[NA_WIDTH:], preferred_element_type=F32))
        o_ref[0, rows, :] = x_ref[0, rows, :] + gate_ref[0, 0] * mix

    per_tile = npairs * OUT_ROWS // SW_BLOCK
    units = [(u, j) for u in range(SW_STEP_BLOCKS) for j in range(npairs)]
    staged = [scores(*units[i]) for i in range(SW_AHEAD)]
    for i, unit in enumerate(units):
        if i + SW_AHEAD < len(units):
            staged.append(scores(*units[i + SW_AHEAD]))
        finish(*unit, staged[i])
        staged[i] = None
        if i % per_tile == OUT_LAG and i > per_tile:
            project(i // per_tile - 1)
    project(len(units) // per_tile - 1)


def _sw_attention_out_proj(sink, q, kv, mask, x, oa, gna, gsw, w, mod):
    bsz, s, d = x.shape
    nblocks = s // SW_BLOCK
    assert nblocks >= 3 and nblocks % SW_STEP_BLOCKS == 0
    rows = SW_STEP_BLOCKS * SW_BLOCK
    tile = lambda b, n, sk: (b, n, 0)
    const = lambda a: pl.BlockSpec(a.shape, lambda b, n, sk: (0,) * a.ndim)
    grid_spec = pltpu.PrefetchScalarGridSpec(
        num_scalar_prefetch=1,
        grid=(bsz, nblocks // SW_STEP_BLOCKS),
        in_specs=[pl.BlockSpec((1, rows, SW_WIDTH), tile),
                  pl.BlockSpec((1, s, LANES), lambda b, n, sk: (b, 0, 0)),
                  pl.BlockSpec((1, s, LANES), lambda b, n, sk: (b, 0, 1)),
                  const(mask),
                  pl.BlockSpec((1, rows, d), tile),
                  pl.BlockSpec((1, NA_WIDTH // LANES, rows, LANES), lambda b, n, sk: (b, 0, n, 0)),
                  const(gna), const(gsw), const(w),
                  pl.BlockSpec((1, 1, 1, d), lambda b, n, sk: (b, MOD_GATE_A, 0, 0))],
        out_specs=pl.BlockSpec((1, rows, d), tile),
        scratch_shapes=[pltpu.VMEM((rows, SW_WIDTH), F32)],
    )
    return pl.pallas_call(
        functools.partial(_sw_out_kernel, nblocks=nblocks),
        grid_spec=grid_spec,
        out_shape=jax.ShapeDtypeStruct((bsz, s, d), F32),
        compiler_params=pltpu.CompilerParams(dimension_semantics=("arbitrary", "arbitrary"),
                                             vmem_limit_bytes=VMEM_LIMIT),
        name="sw_attn_out_proj",
    )(sink, q, kv, kv, mask, x, oa, gna, gsw, w, mod)


def _sw_mask_table():
    i = np.arange(SW_BLOCK)[:, None]
    j = np.arange(3 * SW_BLOCK)[None, :]
    ok = np.stack([np.abs(i - j) <= SW_BLOCK,
                   np.abs(i + SW_BLOCK - j) <= SW_BLOCK,
                   np.abs(i + 2 * SW_BLOCK - j) <= SW_BLOCK])
    return jnp.asarray(np.where(ok, 0.0, NEG), F32)


def _ffn_kernel(x_ref, g_ref, scale_ref, shift_ref, gate_ref, wup_hbm, cw_ref, cb_ref, wdn_hbm, gfin_ref,
                o_ref, h_ref, wv_ref, wg_ref, wd_ref, sems, *, row_chunk):
    s = x_ref.shape[1]
    nf = cw_ref.shape[0]
    nblk = s // FF_ROWS
    nslabs = pl.cdiv(nf, FF_BODY_CHUNKS)
    assert nslabs % 2 == 0
    d_ff = nf * FF_CHUNK
    b = pl.program_id(0)

    def slab_start(p):
        if isinstance(p, int):
            return min(p * FF_SLAB, d_ff - FF_SLAB)
        return pl.multiple_of(jnp.minimum(p * FF_SLAB, d_ff - FF_SLAB), FF_CHUNK)

    def slab_copies(p, slot):
        c0 = slab_start(p)
        return (pltpu.make_async_copy(wup_hbm.at[:, pl.ds(c0, FF_SLAB)], wv_ref.at[slot], sems.at[slot, 0]),
                pltpu.make_async_copy(wup_hbm.at[:, pl.ds(d_ff + c0, FF_SLAB)], wg_ref.at[slot], sems.at[slot, 1]),
                pltpu.make_async_copy(wdn_hbm.at[pl.ds(c0, FF_SLAB), :], wd_ref.at[slot], sems.at[slot, 2]))

    def start(p, slot):
        for cp in slab_copies(p, slot):
            cp.start()

    def wait(p, slot):
        for cp in slab_copies(p, slot):
            cp.wait()

    ridx = lax.broadcasted_iota(jnp.int32, (FF_ROWS, FF_CHUNK), 0)
    edge = jnp.zeros((1, FF_CHUNK), F32)

    def rows(i):
        return slice(i * FF_ROWS, (i + 1) * FF_ROWS)

    def norm_rows(i):
        gain = g_ref[...] * (1.0 + scale_ref[0, 0])
        for r0 in range(i * FF_ROWS, (i + 1) * FF_ROWS, row_chunk):
            xr = x_ref[0, r0:r0 + row_chunk, :]
            h_ref[r0:r0 + row_chunk, :] = (_rms(xr) * gain + shift_ref[0, 0]).astype(BF16)

    def up(slot, lc, i):
        h = h_ref[rows(i), :]
        cols = slice(lc * FF_CHUNK, (lc + 1) * FF_CHUNK)
        return (jnp.dot(h, wv_ref[slot, :, cols], preferred_element_type=F32),
                jnp.dot(h, wg_ref[slot, :, cols], preferred_element_type=F32))

    def act(c, i, cur, before, after):
        val, gt = cur
        before = before[1][FF_ROWS - 1:] if i > 0 else edge
        after = after[1][:1] if i < nblk - 1 else edge
        prev = jnp.where(ridx == 0, before, pltpu.roll(gt, 1, 0))
        nxt = jnp.where(ridx == FF_ROWS - 1, after, pltpu.roll(gt, FF_ROWS - 1, 0))
        cw = cw_ref[c]
        gc = prev * cw[0:1] + gt * cw[1:2] + nxt * cw[2:3] + cb_ref[c]
        return (gc * jax.nn.sigmoid(gc) * val).astype(BF16)

    def run(p, slot, local_chunks, first=False, last=False):
        chunk0 = slab_start(p) // FF_CHUNK
        units = [(lc, i) for lc in local_chunks for i in range(nblk)]

        def issue(u):
            if first and u < nblk:
                norm_rows(units[u][1])
            return up(slot, *units[u])

        ups = {0: issue(0), 1: issue(1)}
        w_rows = slice(local_chunks[0] * FF_CHUNK, (local_chunks[-1] + 1) * FF_CHUNK)
        gated = {i: [] for i in range(nblk)}
        for u, (lc, i) in enumerate(units):
            gated[i].append(act(chunk0 + lc, i, ups[u], ups.get(u - 1), ups.get(u + 1)))
            if u >= len(units) - nblk:
                a = jnp.concatenate(gated.pop(i), axis=1)
                acc = jnp.dot(a, wd_ref[slot, w_rows, :], preferred_element_type=F32)
                if not first:
                    acc = o_ref[0, rows(i), :] + acc
                if last:
                    y = x_ref[0, rows(i), :] + gate_ref[0, 0] * acc
                    acc = _rms(y) * gfin_ref[...]
                o_ref[0, rows(i), :] = acc
            ups.pop(u - 1, None)
            if u + 2 < len(units):
                ups[u + 2] = issue(u + 2)

    whole = list(range(FF_BODY_CHUNKS))

    @pl.when(b == 0)
    def _():
        start(0, 0)

    wait(0, 0)
    start(1, 1)
    run(0, 0, whole, first=True)
    wait(1, 1)

    def body(p, carry):
        slot = p % 2
        start(p + 1, 1 - slot)
        run(p, slot, whole)
        wait(p + 1, 1 - slot)
        return carry

    lax.fori_loop(1, nslabs - 1, body, 0)

    @pl.when(b + 1 < pl.num_programs(0))
    def _():
        start(0, 0)

    done = FF_BODY_CHUNKS * nslabs - nf
    run(nslabs - 1, (nslabs - 1) % 2, whole[done:], last=True)


def _ffn(x1, g, mod, wup, cw, cb, wdn, gfin):
    bsz, s, d = x1.shape
    whole = lambda a: pl.BlockSpec(a.shape, lambda b: (0,) * a.ndim, pipeline_mode=pl.Buffered(1))
    vec = lambda which: pl.BlockSpec((1, 1, 1, d), lambda b: (b, which, 0, 0))
    hbm = pl.BlockSpec(memory_space=pl.ANY)
    return pl.pallas_call(
        functools.partial(_ffn_kernel, row_chunk=256),
        grid=(bsz,),
        in_specs=[pl.BlockSpec((1, s, d), lambda b: (b, 0, 0)),
                  whole(g), vec(MOD_SCALE_F), vec(MOD_SHIFT_F), vec(MOD_GATE_F),
                  hbm, whole(cw), whole(cb), hbm, whole(gfin)],
        out_specs=pl.BlockSpec((1, s, d), lambda b: (b, 0, 0)),
        out_shape=jax.ShapeDtypeStruct((bsz, s, d), F32),
        scratch_shapes=[pltpu.VMEM((s, d), BF16),
                        pltpu.VMEM((2, d, FF_SLAB), BF16), pltpu.VMEM((2, d, FF_SLAB), BF16),
                        pltpu.VMEM((2, FF_SLAB, d), BF16), pltpu.SemaphoreType.DMA((2, 3))],
        compiler_params=pltpu.CompilerParams(dimension_semantics=("arbitrary",),
                                             vmem_limit_bytes=FFN_VMEM_LIMIT),
        name="ffn",
    )(x1, g, mod, mod, mod, wup, cw, cb, wdn, gfin)


def _rope_tables(s):
    half = HEAD_DIM // 2
    inv = ROPE_THETA ** (-np.arange(half, dtype=np.float64) / half)
    ang = np.arange(s, dtype=np.float64)[:, None] * inv[None, :]
    cos = np.cos(ang)
    sin = np.sin(ang)
    reps = LANES // HEAD_DIM
    return (jnp.asarray(np.tile(np.concatenate([cos, cos], axis=-1), (1, reps)), F32),
            jnp.asarray(np.tile(np.concatenate([-sin, sin], axis=-1), (1, reps)), F32))


def _sw_head_order(t, axis):
    shape = t.shape
    t = t.reshape(shape[:axis] + (SW_KV_HEADS, SW_GROUP, -1) + shape[axis + 1:])
    return jnp.swapaxes(t, axis, axis + 1).reshape(shape)


def kernel(x, c, w_ada, b_ada, g_attn, w_in, na_rpb, sw_sink, g_na_out, g_sw_out, w_out, g_ffn, w_up,
           conv_w, conv_b, w_down, g_final):
    bsz, s, d = x.shape
    depth = w_ada.shape[0]
    d_ff = w_down.shape[1]
    assert depth == 1, "the final rmsnorm is fused into the (only) layer's ffn call"
    assert d_ff % FF_CHUNK == 0 and s % GRID_W == 0 and s % SW_BLOCK == 0
    cos, sin = _rope_tables(s)
    sw_mask = _sw_mask_table()
    qb0 = 3 * NA_WIDTH
    for l in range(depth):
        mod = _ada(c, w_ada[l], b_ada[l]).reshape(bsz, 6, 1, d)

        wi = w_in[l].astype(BF16)
        wi = jnp.concatenate([wi[:, :qb0], _sw_head_order(wi[:, qb0:qb0 + SW_WIDTH], 1),
                              wi[:, qb0 + SW_WIDTH:]], axis=1)
        qa, ka, va, qb, kvb = _in_proj(x, g_attn[l].reshape(1, d), mod, wi, cos, sin)

        o_a, wu, wd, wo = _na_attention(qa, ka, va, _na_bias_table(na_rpb[l].astype(F32) * LOG2E, s // GRID_W),
                                        w_up[l], w_down[l], w_out[l])
        x = _sw_attention_out_proj(_sw_head_order(sw_sink[l].astype(F32) * LOG2E, 0), qb, kvb, sw_mask, x, o_a,
                                   g_na_out[l].reshape(1, -1), _sw_head_order(g_sw_out[l], 0).reshape(1, -1), wo, mod)

        nf = d_ff // FF_CHUNK
        cw = conv_w[l].reshape(CONV_W, nf, FF_CHUNK).transpose(1, 0, 2)
        x = _ffn(x, g_ffn[l].reshape(1, d), mod, wu, cw, conv_b[l].reshape(nf, 1, FF_CHUNK), wd,
                 g_final.reshape(1, d))
    return x
```

```python
import functools

import jax
import jax.numpy as jnp
import numpy as np
from jax import lax
from jax.experimental import pallas as pl
from jax.experimental.pallas import tpu as pltpu

F32 = jnp.float32
BF16 = jnp.bfloat16

HEAD_DIM = 64
LANES = 128
NA_HEADS = 8
NA_WIDTH = NA_HEADS * HEAD_DIM
GRID_W = 64
NA_ROWS_WIN = 8
NA_COLS = 16
SW_HEADS = 8
SW_KV_HEADS = 2
SW_GROUP = SW_HEADS // SW_KV_HEADS
SW_WIDTH = SW_HEADS * HEAD_DIM
SW_KV_WIDTH = SW_KV_HEADS * HEAD_DIM
SW_BLOCK = 128
CONV_W = 3
ROPE_THETA = 10000.0
EPS = 1e-6
NEG = -1e30
LOG2E = 1.4426950408889634
Q_SCALE = HEAD_DIM ** -0.5 * LOG2E

VMEM_LIMIT = 56 * 1024 * 1024
FFN_VMEM_LIMIT = 62 * 1024 * 1024

ADA_TN = 1024
IN_TM = 2048
IN_ROWS = 256
FF_CHUNK = 256
FF_BODY_CHUNKS = 3
FF_ROWS = 512
FF_SLAB = FF_BODY_CHUNKS * FF_CHUNK
NA_BODY_PAIRS = 4
NA_AHEAD = 4
SW_STEP_BLOCKS = 16
SW_AHEAD = 3
OUT_ROWS = 512
OUT_LAG = 3


def _rms(x):
    return x * lax.rsqrt(jnp.mean(x * x, axis=-1, keepdims=True) + EPS)


def _nt_dot(a, b):
    return lax.dot_general(a, b, (((1,), (1,)), ((), ())), preferred_element_type=F32)


def _lo_mask(rows):
    return lax.broadcasted_iota(jnp.int32, (rows, LANES), 1) < HEAD_DIM


def _split_bf16(a):
    hi = a.astype(BF16)
    return hi, (a - hi.astype(F32)).astype(BF16)


def _ada_kernel(c_ref, w_ref, b_ref, o_ref):
    c = c_ref[...]
    s_hi, s_lo = _split_bf16(c * jax.nn.sigmoid(c))
    w_hi, w_lo = _split_bf16(w_ref[...])
    dot = functools.partial(jnp.dot, preferred_element_type=F32)
    o_ref[...] = dot(s_hi, w_hi) + dot(s_hi, w_lo) + dot(s_lo, w_hi) + b_ref[...]


def _ada(c, w, b):
    bsz, d = c.shape
    n = w.shape[1]
    tn = ADA_TN
    return pl.pallas_call(
        _ada_kernel,
        grid=(n // tn,),
        in_specs=[pl.BlockSpec((bsz, d), lambda j: (0, 0)),
                  pl.BlockSpec((d, tn), lambda j: (0, j)),
                  pl.BlockSpec((1, tn), lambda j: (0, j))],
        out_specs=pl.BlockSpec((bsz, tn), lambda j: (0, j)),
        out_shape=jax.ShapeDtypeStruct((bsz, n), F32),
        compiler_params=pltpu.CompilerParams(dimension_semantics=("arbitrary",),
                                             vmem_limit_bytes=VMEM_LIMIT),
        name="ada",
    )(c, w, b.reshape(1, n))


MOD_SHIFT_A, MOD_SCALE_A, MOD_GATE_A, MOD_SHIFT_F, MOD_SCALE_F, MOD_GATE_F = range(6)


def _mod_spec(which, d):
    return pl.BlockSpec((1, 1, 1, d), lambda b, t: (b, which, 0, 0))


def _rope(t, cos, sin_signed, first_half):
    rot = jnp.where(first_half, pltpu.roll(t, LANES - HEAD_DIM // 2, 1), pltpu.roll(t, HEAD_DIM // 2, 1))
    return t * cos + rot * sin_signed


def _in_proj_kernel(x_ref, g_ref, scale_ref, shift_ref, w_ref, cos_ref, sin_ref,
                    qa_ref, ka_ref, va_ref, qb_ref, kvb_ref):
    gain = g_ref[...]
    scale = 1.0 + scale_ref[0, 0]
    shift = shift_ref[0, 0]
    first_half = (lax.broadcasted_iota(jnp.int32, (IN_ROWS, LANES), 1) % HEAD_DIM) < HEAD_DIM // 2
    base = 3 * NA_WIDTH

    for i in range(x_ref.shape[1] // IN_ROWS):
        rows = slice(i * IN_ROWS, (i + 1) * IN_ROWS)
        h = ((_rms(x_ref[0, rows, :]) * gain) * scale + shift).astype(BF16)
        cos = cos_ref[rows, :]
        sin = sin_ref[rows, :]

        def proj(lo, hi):
            return jnp.dot(h, w_ref[:, lo:hi], preferred_element_type=F32)

        for ref, col0, mult in ((qa_ref, 0, Q_SCALE), (ka_ref, NA_WIDTH, None), (va_ref, 2 * NA_WIDTH, None)):
            t = proj(col0, col0 + NA_WIDTH)
            t = t if mult is None else t * mult
            for hp in range(NA_WIDTH // LANES):
                ref[0, hp, rows, :] = t[:, hp * LANES:(hp + 1) * LANES].astype(BF16)
        qb = proj(base, base + SW_WIDTH)
        for j in range(SW_WIDTH // LANES):
            blk = _rope(qb[:, j * LANES:(j + 1) * LANES], cos, sin, first_half) * Q_SCALE
            qb_ref[0, rows, j * LANES:(j + 1) * LANES] = blk.astype(BF16)
        kv = proj(base + SW_WIDTH, base + SW_WIDTH + 2 * SW_KV_WIDTH)
        kvb_ref[0, rows, :LANES] = _rope(kv[:, :LANES], cos, sin, first_half).astype(BF16)
        kvb_ref[0, rows, LANES:] = kv[:, LANES:].astype(BF16)


def _in_proj(x, g, mod, w, cos, sin):
    bsz, s, d = x.shape
    n = w.shape[1]
    tm = IN_TM
    row = lambda b, t: (b, t, 0)
    pairs = NA_WIDTH // LANES
    outs = [jax.ShapeDtypeStruct((bsz, pairs, s, LANES), BF16)] * 3 + [
        jax.ShapeDtypeStruct((bsz, s, SW_WIDTH), BF16),
        jax.ShapeDtypeStruct((bsz, s, 2 * SW_KV_WIDTH), BF16)]
    return pl.pallas_call(
        _in_proj_kernel,
        grid=(bsz, s // tm),
        in_specs=[pl.BlockSpec((1, tm, d), row),
                  pl.BlockSpec((1, d), lambda b, t: (0, 0)),
                  _mod_spec(MOD_SCALE_A, d),
                  _mod_spec(MOD_SHIFT_A, d),
                  pl.BlockSpec((d, n), lambda b, t: (0, 0)),
                  pl.BlockSpec((tm, LANES), lambda b, t: (t, 0)),
                  pl.BlockSpec((tm, LANES), lambda b, t: (t, 0))],
        out_specs=[pl.BlockSpec((1, pairs, tm, LANES), lambda b, t: (b, 0, t, 0))] * 3 + [
            pl.BlockSpec((1, tm, SW_WIDTH), row),
            pl.BlockSpec((1, tm, 2 * SW_KV_WIDTH), row)],
        out_shape=outs,
        compiler_params=pltpu.CompilerParams(dimension_semantics=("arbitrary", "arbitrary"),
                                             vmem_limit_bytes=VMEM_LIMIT),
        name="in_proj",
    )(x, g, mod, mod, w, cos, sin)


def _na_kernel(q_ref, k_ref, v_ref, bias_ref, wup_ref, wdn_ref, o_ref, wup_bf_ref, wdn_bf_ref, *, rows):
    lo = _lo_mask(GRID_W)
    win = NA_ROWS_WIN * GRID_W
    ones = jnp.ones((win, LANES), BF16)

    def scores(hp, r):
        rs = jnp.clip(r - NA_ROWS_WIN // 2, 0, rows - NA_ROWS_WIN)
        q = q_ref[0, hp, pl.ds(pl.multiple_of(r * GRID_W, GRID_W), GRID_W), :]
        k0 = pl.multiple_of(rs * GRID_W, GRID_W)
        zero = jnp.zeros_like(q)
        qs = jnp.concatenate([jnp.where(lo, q, zero), jnp.where(lo, zero, q)], axis=0)
        dr0 = NA_ROWS_WIN - 1 - (r - rs)
        bias = jnp.concatenate(
            [jnp.concatenate([bias_ref[2 * hp + head, dr0 + 2 * jj] for jj in range(NA_ROWS_WIN // 2)], axis=1)
             for head in range(2)], axis=0)
        return _nt_dot(qs, k_ref[0, hp, pl.ds(k0, win), :]) + bias, k0

    def finish(hp, r, s, k0):
        p = jnp.exp2(s - jnp.max(s, axis=-1, keepdims=True)).astype(BF16)
        oa = jnp.dot(p, jnp.concatenate([v_ref[0, hp, pl.ds(k0, win), :], ones], axis=1),
                     preferred_element_type=F32)
        o = oa[:, :LANES] * (1.0 / oa[:, LANES:])
        out = jnp.where(lo, o[:GRID_W], o[GRID_W:])
        o_ref[0, hp, pl.ds(pl.multiple_of(r * GRID_W, GRID_W), GRID_W), :] = out.astype(o_ref.dtype)

    def body(it, carry):
        units = [(NA_BODY_PAIRS * it + j, r) for j in range(NA_BODY_PAIRS) for r in range(rows)]
        wup_bf_ref[...] = wup_ref[...].astype(BF16)
        wdn_bf_ref[...] = wdn_ref[...].astype(BF16)
        staged = [scores(*units[i]) for i in range(NA_AHEAD)]
        for i, unit in enumerate(units):
            if i + NA_AHEAD < len(units):
                staged.append(scores(*units[i + NA_AHEAD]))
            finish(*unit, *staged[i])
            staged[i] = None
        return carry

    lax.fori_loop(0, q_ref.shape[1] // NA_BODY_PAIRS, body, 0)


def _na_attention(q, k, v, bias, wup, wdn):
    bsz, pairs, s, _ = q.shape
    rows = s // GRID_W
    assert pairs == NA_BODY_PAIRS and wup.shape[0] % (16 * bsz) == 0 and wdn.shape[0] % (16 * bsz) == 0
    blk = pl.BlockSpec((1, pairs, s, LANES), lambda b: (b, 0, 0, 0))
    row_share = lambda w: pl.BlockSpec((w.shape[0] // bsz, w.shape[1]), lambda b: (b, 0))
    return pl.pallas_call(
        functools.partial(_na_kernel, rows=rows),
        grid=(bsz,),
        in_specs=[blk, blk, blk, pl.BlockSpec(bias.shape, lambda b: (0, 0, 0, 0)), row_share(wup), row_share(wdn)],
        out_specs=[blk, row_share(wup), row_share(wdn)],
        out_shape=[jax.ShapeDtypeStruct(q.shape, BF16), jax.ShapeDtypeStruct(wup.shape, BF16),
                   jax.ShapeDtypeStruct(wdn.shape, BF16)],
        compiler_params=pltpu.CompilerParams(dimension_semantics=("arbitrary",),
                                             vmem_limit_bytes=VMEM_LIMIT),
        name="na_attn",
    )(q, k, v, bias, wup, wdn)


def _na_bias_table(rpb, rows):
    assert rows >= 2 * NA_ROWS_WIN
    col = np.arange(GRID_W)
    cs = np.clip(col - NA_COLS // 2, 0, GRID_W - NA_COLS)
    col_ok = (col[None, :] >= cs[:, None]) & (col[None, :] < cs[:, None] + NA_COLS)
    dc = np.clip(col[None, :] - col[:, None] + NA_COLS - 1, 0, 2 * NA_COLS - 2)
    onehot = (np.arange(2 * NA_COLS - 1)[:, None] == dc.reshape(1, -1)).astype(np.float32)
    t = jnp.dot(rpb.reshape(-1, 2 * NA_COLS - 1), onehot, precision=lax.Precision.HIGHEST)
    t = jnp.where(col_ok[None, None], t.reshape(NA_HEADS, 2 * NA_ROWS_WIN - 1, GRID_W, GRID_W), NEG)
    return jnp.concatenate([t[:, :-1], t[:, 1:]], axis=-1)


def _sw_out_kernel(sink_ref, q_ref, k_ref, v_ref, mask_ref, x_ref, oa_ref, gna_ref, gsw_ref, w_ref, gate_ref,
                   o_ref, ob_ref, *, nblocks):
    lo = _lo_mask(SW_BLOCK)
    zero = jnp.zeros((SW_BLOCK, LANES), q_ref.dtype)
    ones = jnp.ones((3 * SW_BLOCK, LANES), BF16)
    npairs = SW_WIDTH // LANES

    def window(u):
        n = pl.program_id(1) * SW_STEP_BLOCKS + u
        start = pl.multiple_of(jnp.clip(n - 1, 0, nblocks - 3) * SW_BLOCK, SW_BLOCK)
        variant = jnp.where(n == 0, 0, jnp.where(n == nblocks - 1, 2, 1))
        return start, variant

    def scores(u, j):
        start, variant = window(u)
        qj = q_ref[0, u * SW_BLOCK:(u + 1) * SW_BLOCK, j * LANES:(j + 1) * LANES]
        qs = jnp.concatenate([jnp.where(lo, qj, zero), jnp.where(lo, zero, qj)], axis=0)
        s = _nt_dot(qs, k_ref[0, pl.ds(start, 3 * SW_BLOCK), :])
        return s.reshape(2, SW_BLOCK, 3 * SW_BLOCK) + mask_ref[variant]

    def finish(u, j, s):
        start, _ = window(u)
        ps, sinks = [], []
        for half in range(2):
            sg = s[half]
            sk = sink_ref[2 * j + half]
            m = jnp.maximum(jnp.max(sg, axis=-1, keepdims=True), sk)
            ps.append(jnp.exp2(sg - m).astype(BF16))
            sinks.append(jnp.exp2(sk - m))
        oa = jnp.dot(jnp.concatenate(ps, axis=0),
                     jnp.concatenate([v_ref[0, pl.ds(start, 3 * SW_BLOCK), :], ones], axis=1),
                     preferred_element_type=F32)
        o = oa[:, :LANES] * (1.0 / (oa[:, LANES:] + jnp.concatenate(sinks, axis=0)))
        ob_ref[u * SW_BLOCK:(u + 1) * SW_BLOCK, j * LANES:(j + 1) * LANES] = jnp.where(lo, o[:SW_BLOCK], o[SW_BLOCK:])

    def project(t):
        rows = slice(t * OUT_ROWS, (t + 1) * OUT_ROWS)
        oa = jnp.concatenate([oa_ref[0, hp, rows, :] for hp in range(NA_WIDTH // LANES)], axis=1)
        oa = (_rms(oa.astype(F32)) * gna_ref[...]).astype(BF16)
        ob = (_rms(ob_ref[rows, :]) * gsw_ref[...]).astype(BF16)
        mix = (jnp.dot(oa, w_ref[:NA_WIDTH], preferred_element_type=F32)
               + jnp.dot(ob, w_ref[NA_WIDTH:], preferred_element_type=F32))
        o_ref[0, rows, :] = x_ref[0, rows, :] + gate_ref[0, 0] * mix

    per_tile = npairs * OUT_ROWS // SW_BLOCK
    units = [(u, j) for u in range(SW_STEP_BLOCKS) for j in range(npairs)]
    staged = [scores(*units[i]) for i in range(SW_AHEAD)]
    for i, unit in enumerate(units):
        if i + SW_AHEAD < len(units):
            staged.append(scores(*units[i + SW_AHEAD]))
        finish(*unit, staged[i])
        staged[i] = None
        if i % per_tile == OUT_LAG and i > per_tile:
            project(i // per_tile - 1)
    project(len(units) // per_tile - 1)


def _sw_attention_out_proj(sink, q, kv, mask, x, oa, gna, gsw, w, mod):
    bsz, s, d = x.shape
    nblocks = s // SW_BLOCK
    assert nblocks >= 3 and nblocks % SW_STEP_BLOCKS == 0
    rows = SW_STEP_BLOCKS * SW_BLOCK
    tile = lambda b, n, sk: (b, n, 0)
    const = lambda a: pl.BlockSpec(a.shape, lambda b, n, sk: (0,) * a.ndim)
    grid_spec = pltpu.PrefetchScalarGridSpec(
        num_scalar_prefetch=1,
        grid=(bsz, nblocks // SW_STEP_BLOCKS),
        in_specs=[pl.BlockSpec((1, rows, SW_WIDTH), tile),
                  pl.BlockSpec((1, s, LANES), lambda b, n, sk: (b, 0, 0)),
                  pl.BlockSpec((1, s, LANES), lambda b, n, sk: (b, 0, 1)),
                  const(mask),
                  pl.BlockSpec((1, rows, d), tile),
                  pl.BlockSpec((1, NA_WIDTH // LANES, rows, LANES), lambda b, n, sk: (b, 0, n, 0)),
                  const(gna), const(gsw), const(w),
                  pl.BlockSpec((1, 1, 1, d), lambda b, n, sk: (b, MOD_GATE_A, 0, 0))],
        out_specs=pl.BlockSpec((1, rows, d), tile),
        scratch_shapes=[pltpu.VMEM((rows, SW_WIDTH), F32)],
    )
    return pl.pallas_call(
        functools.partial(_sw_out_kernel, nblocks=nblocks),
        grid_spec=grid_spec,
        out_shape=jax.ShapeDtypeStruct((bsz, s, d), F32),
        compiler_params=pltpu.CompilerParams(dimension_semantics=("arbitrary", "arbitrary"),
                                             vmem_limit_bytes=VMEM_LIMIT),
        name="sw_attn_out_proj",
    )(sink, q, kv, kv, mask, x, oa, gna, gsw, w, mod)


def _sw_mask_table():
    i = np.arange(SW_BLOCK)[:, None]
    j = np.arange(3 * SW_BLOCK)[None, :]
    ok = np.stack([np.abs(i - j) <= SW_BLOCK,
                   np.abs(i + SW_BLOCK - j) <= SW_BLOCK,
                   np.abs(i + 2 * SW_BLOCK - j) <= SW_BLOCK])
    return jnp.asarray(np.where(ok, 0.0, NEG), F32)


def _ffn_kernel(x_ref, g_ref, scale_ref, shift_ref, gate_ref, wup_hbm, cw_ref, cb_ref, wdn_hbm, gfin_ref,
                o_ref, h_ref, wv_ref, wg_ref, wd_ref, sems, *, row_chunk):
    s = x_ref.shape[1]
    nf = cw_ref.shape[0]
    nblk = s // FF_ROWS
    nslabs = pl.cdiv(nf, FF_BODY_CHUNKS)
    assert nslabs % 2 == 0
    d_ff = nf * FF_CHUNK
    b = pl.program_id(0)

    def slab_start(p):
        if isinstance(p, int):
            return min(p * FF_SLAB, d_ff - FF_SLAB)
        return pl.multiple_of(jnp.minimum(p * FF_SLAB, d_ff - FF_SLAB), FF_CHUNK)

    def slab_copies(p, slot):
        c0 = slab_start(p)
        return (pltpu.make_async_copy(wup_hbm.at[:, pl.ds(c0, FF_SLAB)], wv_ref.at[slot], sems.at[slot, 0]),
                pltpu.make_async_copy(wup_hbm.at[:, pl.ds(d_ff + c0, FF_SLAB)], wg_ref.at[slot], sems.at[slot, 1]),
                pltpu.make_async_copy(wdn_hbm.at[pl.ds(c0, FF_SLAB), :], wd_ref.at[slot], sems.at[slot, 2]))

    def start(p, slot):
        for cp in slab_copies(p, slot):
            cp.start()

    def wait(p, slot):
        for cp in slab_copies(p, slot):
            cp.wait()

    ridx = lax.broadcasted_iota(jnp.int32, (FF_ROWS, FF_CHUNK), 0)
    edge = jnp.zeros((1, FF_CHUNK), F32)

    def rows(i):
        return slice(i * FF_ROWS, (i + 1) * FF_ROWS)

    def norm_rows(i):
        gain = g_ref[...] * (1.0 + scale_ref[0, 0])
        for r0 in range(i * FF_ROWS, (i + 1) * FF_ROWS, row_chunk):
            xr = x_ref[0, r0:r0 + row_chunk, :]
            h_ref[r0:r0 + row_chunk, :] = (_rms(xr) * gain + shift_ref[0, 0]).astype(BF16)

    def up(slot, lc, i):
        h = h_ref[rows(i), :]
        cols = slice(lc * FF_CHUNK, (lc + 1) * FF_CHUNK)
        return (jnp.dot(h, wv_ref[slot, :, cols], preferred_element_type=F32),
                jnp.dot(h, wg_ref[slot, :, cols], preferred_element_type=F32))

    def act(c, i, cur, before, after):
        val, gt = cur
        before = before[1][FF_ROWS - 1:] if i > 0 else edge
        after = after[1][:1] if i < nblk - 1 else edge
        prev = jnp.where(ridx == 0, before, pltpu.roll(gt, 1, 0))
        nxt = jnp.where(ridx == FF_ROWS - 1, after, pltpu.roll(gt, FF_ROWS - 1, 0))
        cw = cw_ref[c]
        gc = prev * cw[0:1] + gt * cw[1:2] + nxt * cw[2:3] + cb_ref[c]
        return (gc * jax.nn.sigmoid(gc) * val).astype(BF16)

    def run(p, slot, local_chunks, first=False, last=False):
        chunk0 = slab_start(p) // FF_CHUNK
        units = [(lc, i) for lc in local_chunks for i in range(nblk)]

        def issue(u):
            if first and u < nblk:
                norm_rows(units[u][1])
            return up(slot, *units[u])

        ups = {0: issue(0), 1: issue(1)}
        w_rows = slice(local_chunks[0] * FF_CHUNK, (local_chunks[-1] + 1) * FF_CHUNK)
        gated = {i: [] for i in range(nblk)}
        for u, (lc, i) in enumerate(units):
            gated[i].append(act(chunk0 + lc, i, ups[u], ups.get(u - 1), ups.get(u + 1)))
            if u >= len(units) - nblk:
                a = jnp.concatenate(gated.pop(i), axis=1)
                acc = jnp.dot(a, wd_ref[slot, w_rows, :], preferred_element_type=F32)
                if not first:
                    acc = o_ref[0, rows(i), :] + acc
                if last:
                    y = x_ref[0, rows(i), :] + gate_ref[0, 0] * acc
                    acc = _rms(y) * gfin_ref[...]
                o_ref[0, rows(i), :] = acc
            ups.pop(u - 1, None)
            if u + 2 < len(units):
                ups[u + 2] = issue(u + 2)

    whole = list(range(FF_BODY_CHUNKS))

    @pl.when(b == 0)
    def _():
        start(0, 0)

    wait(0, 0)
    start(1, 1)
    run(0, 0, whole, first=True)
    wait(1, 1)

    def body(p, carry):
        slot = p % 2
        start(p + 1, 1 - slot)
        run(p, slot, whole)
        wait(p + 1, 1 - slot)
        return carry

    lax.fori_loop(1, nslabs - 1, body, 0)

    @pl.when(b + 1 < pl.num_programs(0))
    def _():
        start(0, 0)

    done = FF_BODY_CHUNKS * nslabs - nf
    run(nslabs - 1, (nslabs - 1) % 2, whole[done:], last=True)


def _ffn(x1, g, mod, wup, cw, cb, wdn, gfin):
    bsz, s, d = x1.shape
    whole = lambda a: pl.BlockSpec(a.shape, lambda b: (0,) * a.ndim, pipeline_mode=pl.Buffered(1))
    vec = lambda which: pl.BlockSpec((1, 1, 1, d), lambda b: (b, which, 0, 0))
    hbm = pl.BlockSpec(memory_space=pl.ANY)
    return pl.pallas_call(
        functools.partial(_ffn_kernel, row_chunk=256),
        grid=(bsz,),
        in_specs=[pl.BlockSpec((1, s, d), lambda b: (b, 0, 0)),
                  whole(g), vec(MOD_SCALE_F), vec(MOD_SHIFT_F), vec(MOD_GATE_F),
                  hbm, whole(cw), whole(cb), hbm, whole(gfin)],
        out_specs=pl.BlockSpec((1, s, d), lambda b: (b, 0, 0)),
        out_shape=jax.ShapeDtypeStruct((bsz, s, d), F32),
        scratch_shapes=[pltpu.VMEM((s, d), BF16),
                        pltpu.VMEM((2, d, FF_SLAB), BF16), pltpu.VMEM((2, d, FF_SLAB), BF16),
                        pltpu.VMEM((2, FF_SLAB, d), BF16), pltpu.SemaphoreType.DMA((2, 3))],
        compiler_params=pltpu.CompilerParams(dimension_semantics=("arbitrary",),
                                             vmem_limit_bytes=FFN_VMEM_LIMIT),
        name="ffn",
    )(x1, g, mod, mod, mod, wup, cw, cb, wdn, gfin)


def _rope_tables(s):
    half = HEAD_DIM // 2
    inv = ROPE_THETA ** (-np.arange(half, dtype=np.float64) / half)
    ang = np.arange(s, dtype=np.float64)[:, None] * inv[None, :]
    cos = np.cos(ang)
    sin = np.sin(ang)
    reps = LANES // HEAD_DIM
    return (jnp.asarray(np.tile(np.concatenate([cos, cos], axis=-1), (1, reps)), F32),
            jnp.asarray(np.tile(np.concatenate([-sin, sin], axis=-1), (1, reps)), F32))


def _sw_head_order(t, axis):
    shape = t.shape
    t = t.reshape(shape[:axis] + (SW_KV_HEADS, SW_GROUP, -1) + shape[axis + 1:])
    return jnp.swapaxes(t, axis, axis + 1).reshape(shape)


def kernel(x, c, w_ada, b_ada, g_attn, w_in, na_rpb, sw_sink, g_na_out, g_sw_out, w_out, g_ffn, w_up,
           conv_w, conv_b, w_down, g_final):
    bsz, s, d = x.shape
    depth = w_ada.shape[0]
    d_ff = w_down.shape[1]
    assert depth == 1, "the final rmsnorm is fused into the (only) layer's ffn call"
    assert d_ff % FF_CHUNK == 0 and s % GRID_W == 0 and s % SW_BLOCK == 0
    cos, sin = _rope_tables(s)
    sw_mask = _sw_mask_table()
    qb0 = 3 * NA_WIDTH
    for l in range(depth):
        mod = _ada(c, w_ada[l], b_ada[l]).reshape(bsz, 6, 1, d)

        wi = w_in[l].astype(BF16)
        wi = jnp.concatenate([wi[:, :qb0], _sw_head_order(wi[:, qb0:qb0 + SW_WIDTH], 1),
                              wi[:, qb0 + SW_WIDTH:]], axis=1)
        qa, ka, va, qb, kvb = _in_proj(x, g_attn[l].reshape(1, d), mod, wi, cos, sin)

        o_a, wu, wd = _na_attention(qa, ka, va, _na_bias_table(na_rpb[l].astype(F32) * LOG2E, s // GRID_W),
                                    w_up[l], w_down[l])
        wo = w_out[l].astype(BF16)
        wo = jnp.concatenate([wo[:NA_WIDTH], _sw_head_order(wo[NA_WIDTH:], 0)], axis=0)
        x = _sw_attention_out_proj(_sw_head_order(sw_sink[l].astype(F32) * LOG2E, 0), qb, kvb, sw_mask, x, o_a,
                                   g_na_out[l].reshape(1, -1), _sw_head_order(g_sw_out[l], 0).reshape(1, -1), wo, mod)

        nf = d_ff // FF_CHUNK
        cw = conv_w[l].reshape(CONV_W, nf, FF_CHUNK).transpose(1, 0, 2)
        x = _ffn(x, g_ffn[l].reshape(1, d), mod, wu, cw, conv_b[l].reshape(nf, 1, FF_CHUNK), wd,
                 g_final.reshape(1, d))
    return x
```
